```python
import jax
import jax.numpy as jnp
from jax import lax
import numpy as np

D_MODEL = 4096
BATCH = 4
SEQ = 2048
DEPTH = 4
DEC_BATCH = 8
DEC_SEQ = 8
PAST_LEN = 8192
PAGE_SIZE = 128

N_MIXERS = 3
MIX_WIDTH = 3 * D_MODEL // 4
HEAD_DIM = 128
N_HEADS = MIX_WIDTH // HEAD_DIM
MEM_TOKENS = 256
MEM_HEADS = 4
MEM_WIDTH = D_MODEL - MIX_WIDTH
MEM_HEAD_DIM = MEM_WIDTH // MEM_HEADS
D_FF = 4 * D_MODEL
CONV_W = 3
GDN_CONV_W = 4
GDN_CHUNK = 64
Q_BLOCK = 128
RMS_EPS = 1e-6
FORGET_BIAS = 3.0
N_FOX = (DEPTH + 2) // 3
N_CONV = (DEPTH + 1) // 3
N_GDN = DEPTH // 3
FOX_IN = 3 * MIX_WIDTH + N_HEADS + MEM_WIDTH
CONV_IN = 3 * MIX_WIDTH + MEM_WIDTH
GDN_IN = 4 * MIX_WIDTH + 2 * N_HEADS + MEM_WIDTH

kernel_name = "hybrid_fox_conv_gdn_memory_decode_step"


def rmsnorm(x, g):
    xf = x.astype(jnp.float32)
    y = xf * lax.rsqrt(jnp.mean(xf * xf, axis=-1, keepdims=True) + RMS_EPS)
    return (y * g.astype(jnp.float32)).astype(x.dtype)


def l2norm(x):
    xf = x.astype(jnp.float32)
    return xf * lax.rsqrt(jnp.sum(xf * xf, axis=-1, keepdims=True) + 1e-6)


def causal_dwconv(u_pad, w):
    width = w.shape[0]
    t = u_pad.shape[1] - (width - 1)
    return sum(u_pad[:, k:k + t] * w[k] for k in range(width))


def sqrelu_mlp(h, w_up, w_down):
    return jnp.square(jax.nn.relu(h @ w_up)) @ w_down


def memory_kv(mem, g, w):
    b = mem.shape[0]
    k, v = jnp.split(rmsnorm(mem, g) @ w, 2, axis=-1)
    shp = (b, MEM_TOKENS, MEM_HEADS, MEM_HEAD_DIM)
    return k.reshape(shp), v.reshape(shp)


def merge_heads(mix, qm, mem_k, mem_v, w_out):
    b, t, _ = mix.shape
    q = qm.reshape(b, t, MEM_HEADS, MEM_HEAD_DIM)
    s = jnp.einsum('bqhd,bmhd->bhqm', q, mem_k).astype(jnp.float32) * MEM_HEAD_DIM ** -0.5
    p = jax.nn.softmax(s, axis=-1).astype(mem_v.dtype)
    m = jnp.einsum('bhqm,bmhd->bqhd', p, mem_v).reshape(b, t, MEM_WIDTH)
    return jnp.concatenate([mix, m.astype(mix.dtype)], axis=-1) @ w_out


def fox_split(proj, b_f):
    b, t, _ = proj.shape
    q, k, v, f, qm = jnp.split(proj, [MIX_WIDTH, 2 * MIX_WIDTH, 3 * MIX_WIDTH, 3 * MIX_WIDTH + N_HEADS], axis=-1)
    logf = jax.nn.log_sigmoid((f + b_f).astype(jnp.float32))
    shp = (b, t, N_HEADS, HEAD_DIM)
    return q.reshape(shp), k.reshape(shp), v.reshape(shp), logf, qm


def fox_block(q, c_q, pos_q, k, v, c_k):
    s = jnp.einsum('bqhd,bkhd->bhqk', q, k).astype(jnp.float32) * HEAD_DIM ** -0.5
    bias = jnp.transpose(c_q, (0, 2, 1))[:, :, :, None] - jnp.transpose(c_k, (0, 2, 1))[:, :, None, :]
    causal = jnp.arange(k.shape[1])[None, :] <= pos_q[:, None]
    s = jnp.where(causal, s + bias, -jnp.inf)
    p = jax.nn.softmax(s, axis=-1).astype(v.dtype)
    return jnp.einsum('bhqk,bkhd->bqhd', p, v)


def fox_prompt(q, k, v, logf):
    b, t, h, dh = q.shape
    c = jnp.cumsum(logf, axis=1)
    nb = t // Q_BLOCK
    q_blocks = q.reshape(b, nb, Q_BLOCK, h, dh).swapaxes(0, 1)
    c_blocks = c.reshape(b, nb, Q_BLOCK, h).swapaxes(0, 1)

    def one_block(args):
        blk, q_b, c_b = args
        return fox_block(q_b, c_b, blk * Q_BLOCK + jnp.arange(Q_BLOCK), k, v, c)

    o = lax.map(one_block, (jnp.arange(nb), q_blocks, c_blocks))
    return o.swapaxes(0, 1).reshape(b, t, h * dh)


def fox_sample(q, k_new, v_new, logf_new, k_pool, v_pool, logf_pool, page_table):
    db, t = q.shape[:2]
    past = page_table.shape[1] * k_pool.shape[1]
    k_past = k_pool[page_table].reshape(db, past, N_HEADS, HEAD_DIM)
    v_past = v_pool[page_table].reshape(db, past, N_HEADS, HEAD_DIM)
    lf_past = logf_pool[page_table].reshape(db, past, N_HEADS).astype(jnp.float32)
    k_all = jnp.concatenate([k_past, k_new], axis=1)
    v_all = jnp.concatenate([v_past, v_new], axis=1)
    c_all = jnp.cumsum(jnp.concatenate([lf_past, logf_new], axis=1), axis=1)
    o = fox_block(q, c_all[:, past:], past + jnp.arange(t), k_all, v_all, c_all)
    return o.reshape(db, t, MIX_WIDTH)


def shortconv_mixer(proj, buf, w_conv):
    gate_b, gate_c, h, qm = jnp.split(proj, [MIX_WIDTH, 2 * MIX_WIDTH, 3 * MIX_WIDTH], axis=-1)
    u_pad = jnp.concatenate([buf, gate_c * h], axis=1)
    y = gate_b * causal_dwconv(u_pad, w_conv)
    return y, qm, u_pad[:, u_pad.shape[1] - (CONV_W - 1):]


def gated_delta_chunked(q, k, v, g, beta, s0):
    b, t, h, _ = q.shape
    dv = v.shape[-1]
    c = GDN_CHUNK
    n = -(-t // c)
    pad = n * c - t

    def blocks(x):
        x = jnp.pad(x, [(0, 0), (0, pad)] + [(0, 0)] * (x.ndim - 2))
        x = x.reshape((b, n, c) + x.shape[2:])
        return x.transpose((1, 0, 3, 2) + tuple(range(4, x.ndim)))

    incl = jnp.tril(jnp.ones((c, c), bool))
    strict = jnp.tril(jnp.ones((c, c), bool), -1)
    eye = jnp.eye(c, dtype=jnp.float32)

    def step(s, inp):
        q_c, k_c, v_c, g_c, b_c = inp
        gc = jnp.cumsum(g_c, axis=-1)
        decay = jnp.exp(jnp.where(incl, gc[..., :, None] - gc[..., None, :], -jnp.inf))
        kb = k_c * b_c[..., None]
        a = jnp.where(strict, jnp.einsum('bhid,bhjd->bhij', kb, k_c) * decay, 0.0)
        t_inv = lax.linalg.triangular_solve(eye + a, jnp.broadcast_to(eye, a.shape), left_side=True, lower=True)
        u = jnp.einsum('bhij,bhjd->bhid', t_inv, v_c * b_c[..., None])
        w = jnp.einsum('bhij,bhjd->bhid', t_inv, kb * jnp.exp(gc)[..., None])
        v_new = u - jnp.einsum('bhid,bhde->bhie', w, s)
        qk = jnp.einsum('bhid,bhjd->bhij', q_c, k_c) * decay
        o = jnp.einsum('bhid,bhde->bhie', q_c * jnp.exp(gc)[..., None], s) + jnp.einsum('bhij,bhje->bhie', qk, v_new)
        g_last = gc[..., -1]
        s = s * jnp.exp(g_last)[..., None, None] + jnp.einsum(
            'bhid,bhie->bhde', k_c * jnp.exp(g_last[..., None] - gc)[..., None], v_new)
        return s, o

    s, o = lax.scan(step, s0, (blocks(q), blocks(k), blocks(v), blocks(g), blocks(beta)))
    o = o.transpose(1, 0, 3, 2, 4).reshape(b, n * c, h, dv)[:, :t]
    return o, s


def gdn_mixer(proj, conv_buf, s0, w_conv, a_log, dt_bias, g_out):
    b, t, _ = proj.shape
    qkv, z, a, bt, qm = jnp.split(
        proj, [3 * MIX_WIDTH, 4 * MIX_WIDTH, 4 * MIX_WIDTH + N_HEADS, 4 * MIX_WIDTH + 2 * N_HEADS], axis=-1)
    qkv_pad = jnp.concatenate([conv_buf, qkv], axis=1)
    qkv_c = jax.nn.silu(causal_dwconv(qkv_pad, w_conv))
    q, k, v = jnp.split(qkv_c, 3, axis=-1)
    shp = (b, t, N_HEADS, HEAD_DIM)
    q = l2norm(q.reshape(shp)) * HEAD_DIM ** -0.5
    k = l2norm(k.reshape(shp))
    v = v.reshape(shp).astype(jnp.float32)
    g = -jnp.exp(a_log.astype(jnp.float32)) * jax.nn.softplus((a + dt_bias).astype(jnp.float32))
    beta = jax.nn.sigmoid(bt.astype(jnp.float32))
    o, s = gated_delta_chunked(q, k, v, g, beta, s0.astype(jnp.float32))
    o = rmsnorm(o, g_out) * jax.nn.silu(z.reshape(shp).astype(jnp.float32))
    return o.reshape(b, t, MIX_WIDTH).astype(proj.dtype), qm, qkv_pad[:, qkv_pad.shape[1] - (GDN_CONV_W - 1):], s


def setup_inputs(seed: int = 0) -> dict:
    key = jax.random.key(seed)
    ks = iter(jax.random.split(key, 40))

    def nrm(shape, scale=1.0):
        return scale * jax.random.normal(next(ks), shape, jnp.float32)

    def gain(shape):
        return 1.0 + nrm(shape, 0.01)

    n_pages = PAST_LEN // PAGE_SIZE
    n_used = DEC_BATCH * n_pages
    n_pool = (n_used * 5 + 3) // 4
    page_table = jax.random.permutation(next(ks), n_pool)[:n_used].reshape(DEC_BATCH, n_pages).astype(jnp.int32)
    return {
        'x_prompt': nrm((BATCH, SEQ, D_MODEL)),
        'x_sample': nrm((DEC_BATCH, DEC_SEQ, D_MODEL)),
        'cache_fox_k': nrm((N_FOX, n_pool, PAGE_SIZE, N_HEADS, HEAD_DIM)),
        'cache_fox_v': nrm((N_FOX, n_pool, PAGE_SIZE, N_HEADS, HEAD_DIM)),
        'cache_fox_logf': jax.nn.log_sigmoid(FORGET_BIAS + nrm((N_FOX, n_pool, PAGE_SIZE, N_HEADS))),
        'cache_mem_k': nrm((DEPTH, DEC_BATCH, MEM_TOKENS, MEM_HEADS, MEM_HEAD_DIM)),
        'cache_mem_v': nrm((DEPTH, DEC_BATCH, MEM_TOKENS, MEM_HEADS, MEM_HEAD_DIM)),
        'state_conv': nrm((N_CONV, DEC_BATCH, CONV_W - 1, MIX_WIDTH)),
        'state_gdn_conv': nrm((N_GDN, DEC_BATCH, GDN_CONV_W - 1, 3 * MIX_WIDTH)),
        'state_gdn_s': nrm((N_GDN, DEC_BATCH, N_HEADS, HEAD_DIM, HEAD_DIM), 0.1),
        'page_table': page_table,
        'mem_prompt': nrm((BATCH, MEM_TOKENS, D_MODEL)),
        'g_mix_pre': gain((DEPTH, D_MODEL)),
        'g_mix_post': gain((DEPTH, D_MODEL)),
        'g_mlp_pre': gain((DEPTH, D_MODEL)),
        'g_mlp_post': gain((DEPTH, D_MODEL)),
        'g_mem': gain((DEPTH, D_MODEL)),
        'w_mem_kv': nrm((DEPTH, D_MODEL, 2 * MEM_WIDTH), D_MODEL ** -0.5),
        'w_out': nrm((DEPTH, MIX_WIDTH + MEM_WIDTH, D_MODEL), (MIX_WIDTH + MEM_WIDTH) ** -0.5),
        'w_mlp_up': nrm((DEPTH, D_MODEL, D_FF), D_MODEL ** -0.5),
        'w_mlp_down': nrm((DEPTH, D_FF, D_MODEL), D_FF ** -0.5),
        'w_in_fox': nrm((N_FOX, D_MODEL, FOX_IN), D_MODEL ** -0.5),
        'b_fox_f': FORGET_BIAS + nrm((N_FOX, N_HEADS), 0.1),
        'w_in_conv': nrm((N_CONV, D_MODEL, CONV_IN), D_MODEL ** -0.5),
        'w_conv': nrm((N_CONV, CONV_W, MIX_WIDTH), CONV_W ** -0.5),
        'w_in_gdn': nrm((N_GDN, D_MODEL, GDN_IN), D_MODEL ** -0.5),
        'w_gdn_conv': nrm((N_GDN, GDN_CONV_W, 3 * MIX_WIDTH), GDN_CONV_W ** -0.5),
        'gdn_a_log': jnp.log(jax.random.uniform(next(ks), (N_GDN, N_HEADS), jnp.float32, 1.0, 16.0)),
        'gdn_dt_bias': nrm((N_GDN, N_HEADS), 0.1),
        'g_gdn_out': gain((N_GDN, HEAD_DIM)),
    }


def reference(x_prompt, x_sample, cache_fox_k, cache_fox_v, cache_fox_logf, cache_mem_k, cache_mem_v,
              state_conv, state_gdn_conv, state_gdn_s, page_table, mem_prompt,
              g_mix_pre, g_mix_post, g_mlp_pre, g_mlp_post, g_mem, w_mem_kv, w_out, w_mlp_up, w_mlp_down,
              w_in_fox, b_fox_f, w_in_conv, w_conv, w_in_gdn, w_gdn_conv, gdn_a_log, gdn_dt_bias, g_gdn_out):
    yp, ys = x_prompt, x_sample
    bp = x_prompt.shape[0]
    fox_k_p, fox_v_p, fox_lf_p, fox_k_s, fox_v_s, fox_lf_s = [], [], [], [], [], []
    conv_p, conv_s, gconv_p, gconv_s, gs_p, gs_s = [], [], [], [], [], []
    mem_k_p, mem_v_p = [], []
    for i in range(DEPTH):
        kind, j = i % N_MIXERS, i // N_MIXERS
        hp = rmsnorm(yp, g_mix_pre[i])
        hs = rmsnorm(ys, g_mix_pre[i])
        mk_p, mv_p = memory_kv(mem_prompt, g_mem[i], w_mem_kv[i])
        mem_k_p.append(mk_p)
        mem_v_p.append(mv_p)
        if kind == 0:
            qp, kp, vp, lfp, qmp = fox_split(hp @ w_in_fox[j], b_fox_f[j])
            mix_p = fox_prompt(qp, kp, vp, lfp)
            qs, k_s, v_s, lfs, qms = fox_split(hs @ w_in_fox[j], b_fox_f[j])
            mix_s = fox_sample(qs, k_s, v_s, lfs, cache_fox_k[j], cache_fox_v[j], cache_fox_logf[j], page_table)
            fox_k_p.append(kp)
            fox_v_p.append(vp)
            fox_lf_p.append(lfp)
            fox_k_s.append(k_s)
            fox_v_s.append(v_s)
            fox_lf_s.append(lfs)
        elif kind == 1:
            zero_buf = jnp.zeros((bp, CONV_W - 1, MIX_WIDTH), x_prompt.dtype)
            mix_p, qmp, buf_p = shortconv_mixer(hp @ w_in_conv[j], zero_buf, w_conv[j])
            mix_s, qms, buf_s = shortconv_mixer(hs @ w_in_conv[j], state_conv[j], w_conv[j])
            conv_p.append(buf_p)
            conv_s.append(buf_s)
        else:
            zero_buf = jnp.zeros((bp, GDN_CONV_W - 1, 3 * MIX_WIDTH), x_prompt.dtype)
            zero_s = jnp.zeros((bp, N_HEADS, HEAD_DIM, HEAD_DIM), jnp.float32)
            mix_p, qmp, gb_p, s_p = gdn_mixer(hp @ w_in_gdn[j], zero_buf, zero_s, w_gdn_conv[j],
                                              gdn_a_log[j], gdn_dt_bias[j], g_gdn_out[j])
            mix_s, qms, gb_s, s_s = gdn_mixer(hs @ w_in_gdn[j], state_gdn_conv[j], state_gdn_s[j], w_gdn_conv[j],
                                              gdn_a_log[j], gdn_dt_bias[j], g_gdn_out[j])
            gconv_p.append(gb_p)
            gconv_s.append(gb_s)
            gs_p.append(s_p)
            gs_s.append(s_s)
        yp = yp + rmsnorm(merge_heads(mix_p, qmp, mk_p, mv_p, w_out[i]), g_mix_post[i])
        ys = ys + rmsnorm(merge_heads(mix_s, qms, cache_mem_k[i], cache_mem_v[i], w_out[i]), g_mix_post[i])
        yp = yp + rmsnorm(sqrelu_mlp(rmsnorm(yp, g_mlp_pre[i]), w_mlp_up[i], w_mlp_down[i]), g_mlp_post[i])
        ys = ys + rmsnorm(sqrelu_mlp(rmsnorm(ys, g_mlp_pre[i]), w_mlp_up[i], w_mlp_down[i]), g_mlp_post[i])
    new_fox_k_prompt = jnp.stack(fox_k_p)
    new_fox_v_prompt = jnp.stack(fox_v_p)
    new_fox_logf_prompt = jnp.stack(fox_lf_p)
    new_conv_prompt = jnp.stack(conv_p)
    new_gdn_conv_prompt = jnp.stack(gconv_p)
    new_gdn_s_prompt = jnp.stack(gs_p)
    new_mem_k_prompt = jnp.stack(mem_k_p)
    new_mem_v_prompt = jnp.stack(mem_v_p)
    new_fox_k_sample = jnp.stack(fox_k_s)
    new_fox_v_sample = jnp.stack(fox_v_s)
    new_fox_logf_sample = jnp.stack(fox_lf_s)
    new_conv_sample = jnp.stack(conv_s)
    new_gdn_conv_sample = jnp.stack(gconv_s)
    new_gdn_s_sample = jnp.stack(gs_s)
    return (yp, ys, new_fox_k_prompt, new_fox_v_prompt, new_fox_logf_prompt, new_conv_prompt,
            new_gdn_conv_prompt, new_gdn_s_prompt, new_mem_k_prompt, new_mem_v_prompt,
            new_fox_k_sample, new_fox_v_sample, new_fox_logf_sample, new_conv_sample,
            new_gdn_conv_sample, new_gdn_s_sample)
```

```python
import functools

import jax
import jax.numpy as jnp
from jax import lax
from jax.experimental import pallas as pl
from jax.experimental.pallas import tpu as pltpu

F32 = jnp.float32
BF16 = jnp.bfloat16
RMS_EPS = 1e-6
L2_EPS = 1e-6
HEAD_DIM = 128
LANES = 128
SUBLANES = 8
GDN_CHUNK = 128
V7X_VMEM_BYTES = 64 * 1024 * 1024
VMEM_CAP = V7X_VMEM_BYTES - 6 * 1024 * 1024
HIGHEST = lax.Precision.HIGHEST
NT_DIMS = (((1,), (1,)), ((), ()))
TN_DIMS = (((0,), (0,)), ((), ()))


def _params(semantics, vmem_bytes):
    limit = int(min(VMEM_CAP, max(16 * 1024 * 1024, vmem_bytes * 5 // 4 + (2 << 20))))
    return pltpu.CompilerParams(dimension_semantics=semantics, vmem_limit_bytes=limit)


def _tile(dim, pref, align=LANES):
    if dim <= pref:
        return dim
    for cand in range(pref - pref % align, 0, -align):
        if dim % cand == 0:
            return cand
    raise ValueError((dim, pref, align))


def _dot(a, b):
    return jnp.dot(a.astype(BF16), b.astype(BF16), preferred_element_type=F32)


def _dot_nt(a, b):
    return lax.dot_general(a.astype(BF16), b.astype(BF16), NT_DIMS, preferred_element_type=F32)


def _dot_tn(a, b):
    return lax.dot_general(a.astype(BF16), b.astype(BF16), TN_DIMS, preferred_element_type=F32)


def _split(a):
    hi = a.astype(BF16)
    lo = (a - hi.astype(F32)).astype(BF16)
    return hi, lo


def _dot3(a, b):
    ah, al = _split(a)
    bh, bl = _split(b)
    d = functools.partial(jnp.dot, preferred_element_type=F32)
    return d(ah, bh) + (d(ah, bl) + d(al, bh))


def _log_sigmoid(x):
    return jnp.minimum(x, 0.0) - jnp.log1p(jnp.exp(-jnp.abs(x)))


def _softplus(x):
    return jnp.maximum(x, 0.0) + jnp.log1p(jnp.exp(-jnp.abs(x)))


def _silu(x):
    return x * jax.nn.sigmoid(x)


def _pick_lane(blk, idx):
    lane = lax.broadcasted_iota(jnp.int32, blk.shape, 1)
    return jnp.sum(jnp.where(lane == idx, blk, 0.0), axis=-1, keepdims=True)


def _norm_cast_kernel(y_ref, g_ref, o_ref):
    y = y_ref[...]
    inv = lax.rsqrt(jnp.mean(y * y, axis=-1, keepdims=True) + RMS_EPS)
    o_ref[...] = (y * inv * g_ref[...]).astype(o_ref.dtype)


def _norm_cast(y, g):
    m, d = y.shape
    tr = _tile(m, 256)
    return pl.pallas_call(
        _norm_cast_kernel,
        grid=(m // tr,),
        in_specs=[pl.BlockSpec((tr, d), lambda i: (i, 0)), pl.BlockSpec((1, d), lambda i: (0, 0))],
        out_specs=pl.BlockSpec((tr, d), lambda i: (i, 0)),
        out_shape=jax.ShapeDtypeStruct((m, d), BF16),
        compiler_params=_params(("parallel",), 2 * tr * d * 6),
        name="norm_cast",
    )(y, g.reshape(1, d))


def _resid_norm_kernel(d_ref, y_ref, gp_ref, gn_ref, yo_ref, xo_ref):
    d = d_ref[...]
    inv = lax.rsqrt(jnp.mean(d * d, axis=-1, keepdims=True) + RMS_EPS)
    yn = y_ref[...] + d * inv * gp_ref[...]
    yo_ref[...] = yn
    inv2 = lax.rsqrt(jnp.mean(yn * yn, axis=-1, keepdims=True) + RMS_EPS)
    xo_ref[...] = (yn * inv2 * gn_ref[...]).astype(xo_ref.dtype)


def _resid_norm(delta, y, g_post, g_next):
    m, d = y.shape
    tr = _tile(m, 256)
    row = pl.BlockSpec((tr, d), lambda i: (i, 0))
    vec = pl.BlockSpec((1, d), lambda i: (0, 0))
    return pl.pallas_call(
        _resid_norm_kernel,
        grid=(m // tr,),
        in_specs=[row, row, vec, vec],
        out_specs=[row, row],
        out_shape=[jax.ShapeDtypeStruct((m, d), F32), jax.ShapeDtypeStruct((m, d), BF16)],
        compiler_params=_params(("parallel",), 2 * tr * d * 14),
        name="resid_norm",
    )(delta, y, g_post.reshape(1, d), g_next.reshape(1, d))


def _mm_kernel(x_ref, w_ref, o_ref, *scratch, nk, act):
    part = jnp.dot(x_ref[...], w_ref[...].astype(BF16), preferred_element_type=F32)

    def finish(v):
        if act == "relu2":
            v = jnp.square(jnp.maximum(v, 0.0))
        o_ref[...] = v.astype(o_ref.dtype)

    if nk == 1:
        finish(part)
        return
    (acc_ref,) = scratch
    k = pl.program_id(2)

    @pl.when(k == 0)
    def _():
        acc_ref[...] = part

    @pl.when(k > 0)
    def _():
        acc_ref[...] += part

    @pl.when(k == nk - 1)
    def _():
        finish(acc_ref[...])


def _matmul(x, w, *, layer=None, col0=0, n=None, out_dtype=F32, act=None, tm=1024, tn=512, tk=4096):
    m, kdim = x.shape
    n = w.shape[-1] - col0 if n is None else n
    tm, tn, tk = _tile(m, tm), _tile(n, tn), _tile(kdim, tk)
    assert col0 % tn == 0 and w.shape[-2] == kdim
    nk = kdim // tk
    cb = col0 // tn
    if layer is None:
        w_spec = pl.BlockSpec((tk, tn), lambda i, j, k: (k, cb + j))
    else:
        w_spec = pl.BlockSpec((None, tk, tn), lambda i, j, k: (layer, k, cb + j))
    osz = jnp.dtype(out_dtype).itemsize
    vmem = 2 * (tm * tk * 2 + tk * tn * 4 + tm * tn * osz) + tk * tn * 2 + 2 * tm * tn * 4
    return pl.pallas_call(
        functools.partial(_mm_kernel, nk=nk, act=act),
        grid=(m // tm, n // tn, nk),
        in_specs=[pl.BlockSpec((tm, tk), lambda i, j, k: (i, k)), w_spec],
        out_specs=pl.BlockSpec((tm, tn), lambda i, j, k: (i, j)),
        out_shape=jax.ShapeDtypeStruct((m, n), out_dtype),
        scratch_shapes=[pltpu.VMEM((tm, tn), F32)] if nk > 1 else [],
        compiler_params=_params(("parallel", "parallel", "arbitrary"), vmem),
        name="matmul",
    )(x, w)


def _mem_attn_kernel(q_ref, k_ref, v_ref, o_ref, *, heads):
    dh = q_ref.shape[-1] // heads
    scale = dh ** -0.5
    for hh in range(heads):
        sl = slice(hh * dh, (hh + 1) * dh)
        s = _dot_nt(q_ref[:, sl], k_ref[:, sl]) * scale
        p = jnp.exp(s - jnp.max(s, axis=-1, keepdims=True))
        o = _dot(p, v_ref[:, sl]) / jnp.sum(p, axis=-1, keepdims=True)
        o_ref[:, sl] = o.astype(o_ref.dtype)


def _mem_attn(tail, k_arr, v_arr, kv_index, batch, t, heads, mem_width, mt):
    tt = _tile(t, 512)
    nt = t // tt
    lead = (None,) * (k_arr.ndim - 2)
    k_spec = pl.BlockSpec(lead + (mt, mem_width), lambda b, i: kv_index(b, 0))
    v_spec = pl.BlockSpec(lead + (mt, mem_width), lambda b, i: kv_index(b, 1))
    return pl.pallas_call(
        functools.partial(_mem_attn_kernel, heads=heads),
        grid=(batch, nt),
        in_specs=[pl.BlockSpec((tt, mem_width), lambda b, i: (b * nt + i, 0)), k_spec, v_spec],
        out_specs=pl.BlockSpec((tt, mem_width), lambda b, i: (b * nt + i, 0)),
        out_shape=jax.ShapeDtypeStruct((batch * t, mem_width), BF16),
        compiler_params=_params(("parallel", "parallel"), 2 * (tt * mem_width * 6 + 2 * mt * mem_width * 4) + 4 * tt * mt * 4),
        name="mem_attn",
    )(tail, k_arr, v_arr)


def _logf_kernel(f_ref, b_ref, o_ref):
    o_ref[...] = _log_sigmoid(f_ref[...] + b_ref[...])


def _fox_logf(tail, gate_block, bias_pad):
    m = tail.shape[0]
    tr = _tile(m, 1024)
    return pl.pallas_call(
        _logf_kernel,
        grid=(m // tr,),
        in_specs=[pl.BlockSpec((tr, LANES), lambda i: (i, gate_block)), pl.BlockSpec((1, LANES), lambda i: (0, 0))],
        out_specs=pl.BlockSpec((tr, LANES), lambda i: (i, 0)),
        out_shape=jax.ShapeDtypeStruct((m, LANES), F32),
        compiler_params=_params(("parallel",), 4 * tr * LANES * 4),
        name="fox_logf",
    )(tail, bias_pad)


def _cumsum_kernel(x_ref, o_ref, carry_ref, *, carry):
    tr = x_ref.shape[0]
    r = lax.broadcasted_iota(jnp.int32, (tr, tr), 0)
    c = lax.broadcasted_iota(jnp.int32, (tr, tr), 1)
    lower = (r >= c).astype(F32)
    cs = jnp.dot(lower, x_ref[...], precision=HIGHEST, preferred_element_type=F32)
    if carry:
        @pl.when(pl.program_id(1) == 0)
        def _():
            carry_ref[...] = jnp.zeros_like(carry_ref)

        cs = cs + carry_ref[...]
        carry_ref[...] = cs[tr - 1:tr, :]
    o_ref[...] = cs


def _cumsum_rows(x, batch, t, *, carry):
    tr = LANES
    nt = t // tr
    return pl.pallas_call(
        functools.partial(_cumsum_kernel, carry=carry),
        grid=(batch, nt),
        in_specs=[pl.BlockSpec((tr, LANES), lambda b, i: (b * nt + i, 0))],
        out_specs=pl.BlockSpec((tr, LANES), lambda b, i: (b * nt + i, 0)),
        out_shape=jax.ShapeDtypeStruct(x.shape, F32),
        scratch_shapes=[pltpu.VMEM((1, LANES), F32)],
        compiler_params=_params(("parallel", "arbitrary"), 8 * tr * LANES * 4),
        name="cumsum_rows",
    )(x)


def _fox_prompt_kernel(q_ref, k_ref, v_ref, c_ref, crow_ref, o_ref):
    h = pl.program_id(1)
    qi = pl.program_id(2)
    tq, t = q_ref.shape[0], k_ref.shape[0]
    s = _dot_nt(q_ref[...], k_ref[...]) * HEAD_DIM ** -0.5
    bias = _pick_lane(c_ref[...], h) - crow_ref[...]
    row = qi * tq + lax.broadcasted_iota(jnp.int32, (tq, t), 0)
    col = lax.broadcasted_iota(jnp.int32, (tq, t), 1)
    s = jnp.where(col <= row, s + bias, -jnp.inf)
    p = jnp.exp(s - jnp.max(s, axis=-1, keepdims=True))
    o = _dot(p, v_ref[...]) / jnp.sum(p, axis=-1, keepdims=True)
    o_ref[...] = o.astype(o_ref.dtype)


def _fox_prompt(proj, c_tok, c_row, batch, t, heads):
    tq = _tile(t, 256)
    nq = t // tq
    blk = lambda off: pl.BlockSpec((tq, HEAD_DIM), lambda b, h, i: (b * nq + i, off + h))
    full = lambda off: pl.BlockSpec((t, HEAD_DIM), lambda b, h, i: (b, off + h))
    return pl.pallas_call(
        _fox_prompt_kernel,
        grid=(batch, heads, nq),
        in_specs=[blk(0), full(heads), full(2 * heads),
                  pl.BlockSpec((tq, LANES), lambda b, h, i: (b * nq + i, 0)),
                  pl.BlockSpec((None, None, 1, t), lambda b, h, i: (b, h, 0, 0))],
        out_specs=blk(0),
        out_shape=jax.ShapeDtypeStruct((batch * t, heads * HEAD_DIM), BF16),
        compiler_params=_params(("parallel", "parallel", "parallel"), 4 * t * HEAD_DIM * 4 + 6 * tq * t * 4),
        name="fox_prompt",
    )(proj, proj, proj, c_tok, c_row)


def _fox_decode_kernel(pt_ref, q_ref, kn_ref, vn_ref, lfn_ref, kp_ref, vp_ref, lfp_ref, o_ref,
                       m_ref, l_ref, acc_ref, carry_ref, *, n_pages, heads, t_new, page):
    del pt_ref
    p = pl.program_id(1)
    scale = HEAD_DIM ** -0.5
    r = lax.broadcasted_iota(jnp.int32, (page, page), 0)
    c = lax.broadcasted_iota(jnp.int32, (page, page), 1)
    upper = (r <= c).astype(F32)

    @pl.when(p == 0)
    def _():
        m_ref[...] = jnp.full_like(m_ref, -jnp.inf)
        l_ref[...] = jnp.zeros_like(l_ref)
        acc_ref[...] = jnp.zeros_like(acc_ref)
        carry_ref[...] = jnp.zeros_like(carry_ref)

    def attend(k_blk, v_blk, lf_t, mask):
        c_t = jnp.dot(lf_t, upper, precision=HIGHEST, preferred_element_type=F32) + carry_ref[...]
        carry_ref[...] = c_t[:, page - 1:page]
        for h in range(heads):
            sl = slice(h * HEAD_DIM, (h + 1) * HEAD_DIM)
            s = _dot_nt(q_ref[:, sl], k_blk[:, sl]) * scale - c_t[h:h + 1, :]
            if mask is not None:
                s = jnp.where(mask, s, -jnp.inf)
            m_old = m_ref[h]
            m_new = jnp.maximum(m_old, jnp.max(s, axis=-1, keepdims=True))
            alpha = jnp.exp(m_old - m_new)
            pr = jnp.exp(s - m_new)
            l_ref[h] = alpha * l_ref[h] + jnp.sum(pr, axis=-1, keepdims=True)
            acc_ref[h] = alpha * acc_ref[h] + _dot(pr, v_blk[:, sl])
            m_ref[h] = m_new

    @pl.when(p < n_pages)
    def _():
        attend(kp_ref[...], vp_ref[...], lfp_ref[...], None)

    @pl.when(p == n_pages)
    def _():
        pad = jnp.zeros((page - t_new, heads * HEAD_DIM), F32)
        k_blk = jnp.concatenate([kn_ref[...], pad], axis=0)
        v_blk = jnp.concatenate([vn_ref[...], pad], axis=0)
        qrow = lax.broadcasted_iota(jnp.int32, (t_new, page), 0)
        kcol = lax.broadcasted_iota(jnp.int32, (t_new, page), 1)
        attend(k_blk, v_blk, lfn_ref[...], kcol <= qrow)
        for h in range(heads):
            sl = slice(h * HEAD_DIM, (h + 1) * HEAD_DIM)
            o_ref[:, sl] = (acc_ref[h] / l_ref[h]).astype(o_ref.dtype)


def _fox_decode(proj, lfn_t, k_pool, v_pool, lf_pool_t, page_table, layer, batch, t_new, heads):
    n_pages = page_table.shape[1]
    page = k_pool.shape[2]
    width = heads * HEAD_DIM
    assert page == LANES and t_new <= page
    last = n_pages - 1
    new = lambda off: pl.BlockSpec((t_new, width), lambda b, p, pt: (b, off))
    pool = pl.BlockSpec((None, None, page, width), lambda b, p, pt: (layer, pt[b, jnp.minimum(p, last)], 0, 0))
    lf_pool = pl.BlockSpec((None, None, heads, page), lambda b, p, pt: (layer, pt[b, jnp.minimum(p, last)], 0, 0))
    grid_spec = pltpu.PrefetchScalarGridSpec(
        num_scalar_prefetch=1,
        grid=(batch, n_pages + 1),
        in_specs=[new(0), new(1), new(2), pl.BlockSpec((None, heads, page), lambda b, p, pt: (b, 0, 0)),
                  pool, pool, lf_pool],
        out_specs=pl.BlockSpec((t_new, width), lambda b, p, pt: (b, 0)),
        scratch_shapes=[pltpu.VMEM((heads, t_new, 1), F32), pltpu.VMEM((heads, t_new, 1), F32),
                        pltpu.VMEM((heads, t_new, HEAD_DIM), F32), pltpu.VMEM((heads, 1), F32)],
    )
    return pl.pallas_call(
        functools.partial(_fox_decode_kernel, n_pages=n_pages, heads=heads, t_new=t_new, page=page),
        grid_spec=grid_spec,
        out_shape=jax.ShapeDtypeStruct((batch * t_new, width), BF16),
        compiler_params=_params(("parallel", "arbitrary"), 4 * page * width * 4 + 8 * t_new * width * 4 + (4 << 20)),
        name="fox_decode",
    )(page_table, proj, proj, proj, lfn_t, k_pool, v_pool, lf_pool_t)


def _shortconv_kernel(gb_ref, gc_ref, h_ref, buf_ref, w_ref, o_ref, tail_ref, us_ref, *, width):
    tt = gb_ref.shape[0]

    @pl.when(pl.program_id(2) == 0)
    def _():
        us_ref[0:SUBLANES, :] = buf_ref[...]

    us_ref[SUBLANES:SUBLANES + tt, :] = gc_ref[...] * h_ref[...]
    conv = None
    for k in range(width):
        off = SUBLANES - (width - 1) + k
        term = w_ref[k:k + 1, :] * us_ref[off:off + tt, :]
        conv = term if conv is None else conv + term
    o_ref[...] = (gb_ref[...] * conv).astype(o_ref.dtype)
    last = us_ref[tt:tt + SUBLANES, :]
    tail_ref[...] = last
    us_ref[0:SUBLANES, :] = last


def _shortconv(proj, buf8, w8, batch, t, mix_width, width):
    tt = _tile(t, 512)
    tc = _tile(mix_width, 512)
    nt, nc = t // tt, mix_width // tc
    col = lambda off: pl.BlockSpec((tt, tc), lambda b, c, i: (b * nt + i, off * nc + c))
    return pl.pallas_call(
        functools.partial(_shortconv_kernel, width=width),
        grid=(batch, nc, nt),
        in_specs=[col(0), col(1), col(2),
                  pl.BlockSpec((None, SUBLANES, tc), lambda b, c, i: (b, 0, c)),
                  pl.BlockSpec((SUBLANES, tc), lambda b, c, i: (0, c))],
        out_specs=[col(0), pl.BlockSpec((None, SUBLANES, tc), lambda b, c, i: (b, 0, c))],
        out_shape=[jax.ShapeDtypeStruct((batch * t, mix_width), BF16),
                   jax.ShapeDtypeStruct((batch, SUBLANES, mix_width), F32)],
        scratch_shapes=[pltpu.VMEM((tt + SUBLANES, tc), F32)],
        compiler_params=_params(("parallel", "parallel", "arbitrary"), 12 * tt * tc * 4),
        name="shortconv",
    )(proj, proj, proj, buf8, w8)


def _gdn_conv_kernel(x_ref, buf_ref, w_ref, o_ref, xs_ref, *, width, l2norm, scale):
    tt, tc = x_ref.shape

    @pl.when(pl.program_id(2) == 0)
    def _():
        xs_ref[0:SUBLANES, :] = buf_ref[...]

    xs_ref[SUBLANES:SUBLANES + tt, :] = x_ref[...]
    conv = None
    for k in range(width):
        off = SUBLANES - (width - 1) + k
        term = w_ref[k:k + 1, :] * xs_ref[off:off + tt, :]
        conv = term if conv is None else conv + term
    y = _silu(conv)
    if l2norm:
        for g in range(tc // HEAD_DIM):
            sl = slice(g * HEAD_DIM, (g + 1) * HEAD_DIM)
            seg = y[:, sl]
            seg = seg * lax.rsqrt(jnp.sum(seg * seg, axis=-1, keepdims=True) + L2_EPS)
            o_ref[:, sl] = seg * scale if scale != 1.0 else seg
    else:
        o_ref[...] = y
    xs_ref[0:SUBLANES, :] = xs_ref[tt:tt + SUBLANES, :]


def _gdn_conv(proj, buf8, w8, part, batch, t, mix_width, width, *, l2norm, scale=1.0):
    tt = _tile(t, 512)
    tc = _tile(mix_width, 512)
    nt, nc = t // tt, mix_width // tc
    return pl.pallas_call(
        functools.partial(_gdn_conv_kernel, width=width, l2norm=l2norm, scale=scale),
        grid=(batch, nc, nt),
        in_specs=[pl.BlockSpec((tt, tc), lambda b, c, i: (b * nt + i, part * nc + c)),
                  pl.BlockSpec((None, SUBLANES, tc), lambda b, c, i: (b, 0, part * nc + c)),
                  pl.BlockSpec((SUBLANES, tc), lambda b, c, i: (0, part * nc + c))],
        out_specs=pl.BlockSpec((tt, tc), lambda b, c, i: (b * nt + i, c)),
        out_shape=jax.ShapeDtypeStruct((batch * t, mix_width), F32),
        scratch_shapes=[pltpu.VMEM((tt + SUBLANES, tc), F32)],
        compiler_params=_params(("parallel", "parallel", "arbitrary"), 10 * tt * tc * 4),
        name="gdn_conv",
    )(proj, buf8, w8)


def _gdn_gate_kernel(a_ref, bt_ref, alog_ref, dt_ref, g_ref, beta_ref):
    g_ref[...] = -jnp.exp(alog_ref[...]) * _softplus(a_ref[...] + dt_ref[...])
    beta_ref[...] = jax.nn.sigmoid(bt_ref[...])


def _gdn_gates(tail, a_block, bt_block, a_log_pad, dt_pad):
    m = tail.shape[0]
    tr = _tile(m, 1024)
    blk = lambda off: pl.BlockSpec((tr, LANES), lambda i: (i, off))
    vec = pl.BlockSpec((1, LANES), lambda i: (0, 0))
    return pl.pallas_call(
        _gdn_gate_kernel,
        grid=(m // tr,),
        in_specs=[blk(a_block), blk(bt_block), vec, vec],
        out_specs=[blk(0), blk(0)],
        out_shape=[jax.ShapeDtypeStruct((m, LANES), F32)] * 2,
        compiler_params=_params(("parallel",), 8 * tr * LANES * 4),
        name="gdn_gates",
    )(tail, tail, a_log_pad, dt_pad)


def _unit_lower_inverse(a, ri, ci):
    n = a.shape[0]
    eye = (ri == ci).astype(F32)
    blk = lambda idx, size: jnp.right_shift(idx, size.bit_length() - 1)
    base = SUBLANES
    a0 = jnp.where(blk(ri, base) == blk(ci, base), a, 0.0)
    t = eye - a0
    pw = _dot3(a0, a0)
    t = t + _dot3(t, pw)
    pw = _dot3(pw, pw)
    t = t + _dot3(t, pw)
    s = base
    while s < n:
        pair = blk(ri, 2 * s) == blk(ci, 2 * s)
        off = jnp.where(pair, jnp.where(blk(ri, s) != blk(ci, s), a, 0.0), 0.0)
        t = t - _dot3(_dot3(t, off), t)
        s *= 2
    return t


def _gdn_intra_kernel(q_ref, k_ref, v_ref, gc_ref, beta_ref, u_ref, w_ref, qk_ref, qg_ref, kg_ref, *, group):
    hg = pl.program_id(1)
    n = q_ref.shape[0]
    ri = lax.broadcasted_iota(jnp.int32, (n, n), 0)
    ci = lax.broadcasted_iota(jnp.int32, (n, n), 1)
    for i in range(group):
        sl = slice(i * HEAD_DIM, (i + 1) * HEAD_DIM)
        q, k, v = q_ref[:, sl], k_ref[:, sl], v_ref[:, sl]
        gcol = _pick_lane(gc_ref[...], hg * group + i)
        bcol = _pick_lane(beta_ref[...], hg * group + i)
        gmat = jnp.broadcast_to(gcol, (n, n))
        diff = gmat - gmat.T
        decay = jnp.where(ri >= ci, jnp.exp(jnp.where(ri >= ci, diff, 0.0)), 0.0)
        kb = k * bcol
        a = jnp.where(ri > ci, _dot_nt(kb, k) * decay, 0.0)
        t_inv = _unit_lower_inverse(a, ri, ci)
        eg = jnp.exp(gcol)
        uw = _dot(t_inv, jnp.concatenate([v * bcol, kb * eg], axis=1))
        u_ref[:, sl] = uw[:, :HEAD_DIM]
        w_ref[:, sl] = uw[:, HEAD_DIM:]
        qk_ref[:, sl] = _dot_nt(q, k) * decay
        qg_ref[:, sl] = q * eg
        kg_ref[:, sl] = k * jnp.exp(gcol[n - 1:n, :] - gcol)


def _gdn_intra(qn, kn, vc, gc, beta, batch, t, heads, group):
    n = GDN_CHUNK
    nc = t // n
    wide = pl.BlockSpec((n, group * HEAD_DIM), lambda b, g, c: (b * nc + c, g))
    gate = pl.BlockSpec((n, LANES), lambda b, g, c: (b * nc + c, 0))
    shape = jax.ShapeDtypeStruct((batch * t, heads * HEAD_DIM), F32)
    return pl.pallas_call(
        functools.partial(_gdn_intra_kernel, group=group),
        grid=(batch, heads // group, nc),
        in_specs=[wide, wide, wide, gate, gate],
        out_specs=[wide] * 5,
        out_shape=[shape] * 5,
        compiler_params=_params(("parallel", "parallel", "parallel"), 2 * 8 * n * group * HEAD_DIM * 4 + (8 << 20)),
        name="gdn_intra",
    )(qn, kn, vc, gc, beta)


def _gdn_state_kernel(u_ref, w_ref, qk_ref, qg_ref, kg_ref, gc_ref, z_ref, gout_ref, s0_ref, o_ref, sout_ref,
                      s_ref, *, group, n_chunks):
    hg = pl.program_id(1)
    c = pl.program_id(2)
    n = u_ref.shape[0]

    @pl.when(c == 0)
    def _():
        s_ref[...] = s0_ref[...]

    for i in range(group):
        sl = slice(i * HEAD_DIM, (i + 1) * HEAD_DIM)
        s = s_ref[i]
        v_new = u_ref[:, sl] - _dot(w_ref[:, sl], s)
        o = _dot(qg_ref[:, sl], s) + _dot(qk_ref[:, sl], v_new)
        g_last = _pick_lane(gc_ref[n - 1:n, :], hg * group + i)
        s_ref[i] = s * jnp.exp(g_last) + _dot_tn(kg_ref[:, sl], v_new)
        on = o * lax.rsqrt(jnp.mean(o * o, axis=-1, keepdims=True) + RMS_EPS) * gout_ref[...]
        o_ref[:, sl] = (on * _silu(z_ref[:, sl])).astype(o_ref.dtype)

    @pl.when(c == n_chunks - 1)
    def _():
        sout_ref[...] = s_ref[...]


def _gdn_state(u, w, qk, qg, kg, gc, z_arr, z_block0, g_out, s0, batch, t, heads, group):
    n = GDN_CHUNK
    nc = t // n
    gw = group * HEAD_DIM
    wide = pl.BlockSpec((n, gw), lambda b, g, c: (b * nc + c, g))
    state = pl.BlockSpec((None, group, HEAD_DIM, HEAD_DIM), lambda b, g, c: (b, g, 0, 0))
    return pl.pallas_call(
        functools.partial(_gdn_state_kernel, group=group, n_chunks=nc),
        grid=(batch, heads // group, nc),
        in_specs=[wide] * 5 + [pl.BlockSpec((n, LANES), lambda b, g, c: (b * nc + c, 0)),
                               pl.BlockSpec((n, gw), lambda b, g, c: (b * nc + c, z_block0 + g)),
                               pl.BlockSpec((1, HEAD_DIM), lambda b, g, c: (0, 0)), state],
        out_specs=[wide, state],
        out_shape=[jax.ShapeDtypeStruct((batch * t, heads * HEAD_DIM), BF16),
                   jax.ShapeDtypeStruct((batch, heads, HEAD_DIM, HEAD_DIM), F32)],
        scratch_shapes=[pltpu.VMEM((group, HEAD_DIM, HEAD_DIM), F32)],
        compiler_params=_params(("parallel", "parallel", "arbitrary"), 2 * 8 * n * gw * 4 + 6 * group * HEAD_DIM * HEAD_DIM * 4 + (4 << 20)),
        name="gdn_state",
    )(u, w, qk, qg, kg, gc, z_arr, g_out.reshape(1, HEAD_DIM), s0)


def _pad_lanes(v, width=LANES):
    return jnp.pad(v, [(0, 0)] * (v.ndim - 1) + [(0, width - v.shape[-1])])


def _pad_rows8(a, rows_axis):
    pad = [(0, 0)] * a.ndim
    pad[rows_axis] = (SUBLANES - a.shape[rows_axis], 0)
    return jnp.pad(a, pad)


def _tail_weight(w_in, main, parts, mem_width):
    cols = [w_in[:, w_in.shape[1] - mem_width:]]
    off = main
    for width in parts:
        cols.append(_pad_lanes(w_in[:, off:off + width]))
        off += width
    return jnp.concatenate(cols, axis=1)


def _in_proj(x, w_stack, j, main, w_tail):
    tn_tail = 256 if w_tail.shape[1] % 256 == 0 else LANES
    return (_matmul(x, w_stack, layer=j, col0=0, n=main),
            _matmul(x, w_tail, tn=tn_tail))


def _fox_mixer(xs, w_in_fox, b_fox_f, j, caches, page_table, heads, mem_width):
    mix_width = heads * HEAD_DIM
    main = 3 * mix_width
    w_tail = _tail_weight(w_in_fox[j], main, [heads], mem_width)
    gate_block = mem_width // LANES
    bias = _pad_lanes(b_fox_f[j].reshape(1, heads))
    out = {}
    for name, (x, batch, t) in xs.items():
        proj, tail = _in_proj(x, w_in_fox, j, main, w_tail)
        logf = _fox_logf(tail, gate_block, bias)
        if name == "prompt":
            c_tok = _cumsum_rows(logf, batch, t, carry=True)
            c_row = jnp.transpose(c_tok.reshape(batch, t, LANES)[:, :, :heads], (0, 2, 1))[:, :, None, :]
            mix = _fox_prompt(proj, c_tok, c_row, batch, t, heads)
        else:
            k_pool, v_pool, lf_pool_t = caches
            lfn_t = jnp.transpose(logf.reshape(batch, t, LANES)[:, :, :heads], (0, 2, 1))
            lfn_t = _pad_lanes(lfn_t, k_pool.shape[2])
            mix = _fox_decode(proj, lfn_t, k_pool, v_pool, lf_pool_t, page_table, j, batch, t, heads)
        shp = (batch, t, heads, HEAD_DIM)
        out[name] = dict(
            mix=mix, tail=tail,
            k=proj[:, mix_width:2 * mix_width].reshape(shp),
            v=proj[:, 2 * mix_width:3 * mix_width].reshape(shp),
            logf=logf[:, :heads].reshape(batch, t, heads))
    return out


def _conv_mixer(xs, w_in_conv, w_conv, j, state_conv, heads, mem_width):
    mix_width = heads * HEAD_DIM
    main = 3 * mix_width
    width = w_conv.shape[1]
    w_tail = w_in_conv[j][:, main:]
    w8 = jnp.pad(w_conv[j], ((0, SUBLANES - width), (0, 0)))
    out = {}
    for name, (x, batch, t) in xs.items():
        proj, tail = _in_proj(x, w_in_conv, j, main, w_tail)
        if name == "prompt":
            buf8 = jnp.zeros((batch, SUBLANES, mix_width), F32)
        else:
            buf8 = _pad_rows8(state_conv[j], 1)
        mix, last8 = _shortconv(proj, buf8, w8, batch, t, mix_width, width)
        out[name] = dict(mix=mix, tail=tail, buf=last8[:, SUBLANES - (width - 1):, :])
    return out


def _gdn_mixer(xs, w_in_gdn, w_gdn_conv, gdn_a_log, gdn_dt_bias, g_gdn_out, j, state_gdn_conv, state_gdn_s,
               heads, mem_width):
    mix_width = heads * HEAD_DIM
    main = 4 * mix_width
    width = w_gdn_conv.shape[1]
    w_tail = _tail_weight(w_in_gdn[j], main, [heads, heads], mem_width)
    a_block = mem_width // LANES
    w8 = jnp.pad(w_gdn_conv[j], ((0, SUBLANES - width), (0, 0)))
    a_log = _pad_lanes(gdn_a_log[j].reshape(1, heads))
    dt_bias = _pad_lanes(gdn_dt_bias[j].reshape(1, heads))
    group = 2 if heads % 2 == 0 else 1
    out = {}
    for name, (x, batch, t) in xs.items():
        proj, tail = _in_proj(x, w_in_gdn, j, main, w_tail)
        if name == "prompt":
            buf = jnp.zeros((batch, width - 1, 3 * mix_width), F32)
            s0 = jnp.zeros((batch, heads, HEAD_DIM, HEAD_DIM), F32)
        else:
            buf, s0 = state_gdn_conv[j], state_gdn_s[j]
        buf8 = _pad_rows8(buf, 1)
        conv = functools.partial(_gdn_conv, proj, buf8, w8, batch=batch, t=t, mix_width=mix_width, width=width)
        qn = conv(part=0, l2norm=True, scale=HEAD_DIM ** -0.5)
        kn = conv(part=1, l2norm=True)
        vc = conv(part=2, l2norm=False)
        g, beta = _gdn_gates(tail, a_block, a_block + 1, a_log, dt_bias)
        z_arr, z_block0 = proj, 3 * mix_width // (group * HEAD_DIM)
        tp = -(-t // GDN_CHUNK) * GDN_CHUNK
        if tp != t:
            pad = lambda a_: jnp.pad(a_.reshape(batch, t, -1), ((0, 0), (0, tp - t), (0, 0))).reshape(batch * tp, -1)
            qn, kn, vc, g, beta = (pad(a_) for a_ in (qn, kn, vc, g, beta))
            z_arr, z_block0 = pad(proj[:, 3 * mix_width:main]), 0
        gc = _cumsum_rows(g, batch, tp, carry=False)
        u, w, qk, qg, kg = _gdn_intra(qn, kn, vc, gc, beta, batch, tp, heads, group)
        mix, s_new = _gdn_state(u, w, qk, qg, kg, gc, z_arr, z_block0, g_gdn_out[j], s0, batch, tp, heads, group)
        if tp != t:
            mix = mix.reshape(batch, tp, mix_width)[:, :t].reshape(batch * t, mix_width)
        keep = min(t, width - 1)
        newest = proj.reshape(batch, t, -1)[:, t - keep:, :3 * mix_width]
        out[name] = dict(mix=mix, tail=tail, conv=jnp.concatenate([buf, newest], axis=1)[:, keep:], s=s_new)
    return out


def kernel(x_prompt, x_sample, cache_fox_k, cache_fox_v, cache_fox_logf, cache_mem_k, cache_mem_v, state_conv, state_gdn_conv, state_gdn_s, page_table, mem_prompt, g_mix_pre, g_mix_post, g_mlp_pre, g_mlp_post, g_mem, w_mem_kv, w_out, w_mlp_up, w_mlp_down, w_in_fox, b_fox_f, w_in_conv, w_conv, w_in_gdn, w_gdn_conv, gdn_a_log, gdn_dt_bias, g_gdn_out):
    bp, tp_, d = x_prompt.shape
    bs, ts, _ = x_sample.shape
    depth = g_mix_pre.shape[0]
    mem_tokens, mem_heads, mem_hd = cache_mem_k.shape[2:]
    mem_width = mem_heads * mem_hd
    mix_width = d - mem_width
    heads = mix_width // HEAD_DIM
    n_fox, n_pool, page = cache_fox_k.shape[:3]

    y = {"prompt": x_prompt.reshape(bp * tp_, d), "sample": x_sample.reshape(bs * ts, d)}
    dims = {"prompt": (bp, tp_), "sample": (bs, ts)}
    x = {n: _norm_cast(y[n], g_mix_pre[0]) for n in y}
    mem2d = mem_prompt.reshape(bp * mem_tokens, d)
    caches = (cache_fox_k.reshape(n_fox, n_pool, page, mix_width), cache_fox_v.reshape(n_fox, n_pool, page, mix_width),
              jnp.transpose(cache_fox_logf, (0, 1, 3, 2)))
    mem_k_s = cache_mem_k.reshape(depth, bs, mem_tokens, mem_width)
    mem_v_s = cache_mem_v.reshape(depth, bs, mem_tokens, mem_width)

    res = {k_: [] for k_ in ("fox_k_p", "fox_v_p", "fox_lf_p", "fox_k_s", "fox_v_s", "fox_lf_s", "conv_p", "conv_s",
                             "gconv_p", "gconv_s", "gs_p", "gs_s", "mem_k", "mem_v")}
    for i in range(depth):
        kind, j = i % 3, i // 3
        xs = {n: (x[n], *dims[n]) for n in x}
        kv = _matmul(_norm_cast(mem2d, g_mem[i]), w_mem_kv, layer=i)
        res["mem_k"].append(kv[:, :mem_width].reshape(bp, mem_tokens, mem_heads, mem_hd))
        res["mem_v"].append(kv[:, mem_width:].reshape(bp, mem_tokens, mem_heads, mem_hd))
        if kind == 0:
            mixed = _fox_mixer(xs, w_in_fox, b_fox_f, j, caches, page_table, heads, mem_width)
            for n, s in (("prompt", "p"), ("sample", "s")):
                res["fox_k_" + s].append(mixed[n]["k"])
                res["fox_v_" + s].append(mixed[n]["v"])
                res["fox_lf_" + s].append(mixed[n]["logf"])
        elif kind == 1:
            mixed = _conv_mixer(xs, w_in_conv, w_conv, j, state_conv, heads, mem_width)
            res["conv_p"].append(mixed["prompt"]["buf"])
            res["conv_s"].append(mixed["sample"]["buf"])
        else:
            mixed = _gdn_mixer(xs, w_in_gdn, w_gdn_conv, gdn_a_log, gdn_dt_bias, g_gdn_out, j, state_gdn_conv,
                               state_gdn_s, heads, mem_width)
            for n, s in (("prompt", "p"), ("sample", "s")):
                res["gconv_" + s].append(mixed[n]["conv"])
                res["gs_" + s].append(mixed[n]["s"])
        g_next = g_mix_pre[i + 1] if i + 1 < depth else g_mlp_pre[i]
        for n in ("prompt", "sample"):
            b, t = dims[n]
            if n == "prompt":
                mem = _mem_attn(mixed[n]["tail"], kv, kv, lambda bb, part: (bb, part), b, t, mem_heads, mem_width,
                                mem_tokens)
            else:
                mem = _mem_attn(mixed[n]["tail"], mem_k_s, mem_v_s, lambda bb, part: (i, bb, 0, 0), b, t, mem_heads,
                                mem_width, mem_tokens)
            cat = jnp.concatenate([mixed[n]["mix"], mem], axis=1)
            delta = _matmul(cat, w_out, layer=i)
            y[n], x2 = _resid_norm(delta, y[n], g_mix_post[i], g_mlp_pre[i])
            hid = _matmul(x2, w_mlp_up, layer=i, out_dtype=BF16, act="relu2")
            delta = _matmul(hid, w_mlp_down, layer=i, tk=2048)
            y[n], x[n] = _resid_norm(delta, y[n], g_mlp_post[i], g_next)

    st = lambda k_: jnp.stack(res[k_])
    return (y["prompt"].reshape(bp, tp_, d), y["sample"].reshape(bs, ts, d),
            st("fox_k_p"), st("fox_v_p"), st("fox_lf_p"), st("conv_p"), st("gconv_p"), st("gs_p"),
            st("mem_k"), st("mem_v"),
            st("fox_k_s"), st("fox_v_s"), st("fox_lf_s"), st("conv_s"), st("gconv_s"), st("gs_s"))
```

```python
import functools

import jax
import jax.numpy as jnp
from jax import lax
from jax.experimental import pallas as pl
from jax.experimental.pallas import tpu as pltpu

F32 = jnp.float32
BF16 = jnp.bfloat16
RMS_EPS = 1e-6
L2_EPS = 1e-6
HEAD_DIM = 128
LANES = 128
SUBLANES = 8
GDN_CHUNK = 128
V7X_VMEM_BYTES = 64 * 1024 * 1024
VMEM_CAP = V7X_VMEM_BYTES - 6 * 1024 * 1024
HIGHEST = lax.Precision.HIGHEST
NT_DIMS = (((1,), (1,)), ((), ()))
TN_DIMS = (((0,), (0,)), ((), ()))


def _params(semantics, vmem_bytes):
    limit = int(min(VMEM_CAP, max(16 * 1024 * 1024, vmem_bytes * 5 // 4 + (2 << 20))))
    return pltpu.CompilerParams(dimension_semantics=semantics, vmem_limit_bytes=limit)


def _tile(dim, pref, align=LANES):
    if dim <= pref:
        return dim
    for cand in range(pref - pref % align, 0, -align):
        if dim % cand == 0:
            return cand
    raise ValueError((dim, pref, align))


def _dot(a, b):
    return jnp.dot(a.astype(BF16), b.astype(BF16), preferred_element_type=F32)


def _dot_nt(a, b):
    return lax.dot_general(a.astype(BF16), b.astype(BF16), NT_DIMS, preferred_element_type=F32)


def _dot_tn(a, b):
    return lax.dot_general(a.astype(BF16), b.astype(BF16), TN_DIMS, preferred_element_type=F32)


def _each(fn, *columns):
    return [fn(*args) for args in zip(*columns)]


def _log_sigmoid(x):
    return jnp.minimum(x, 0.0) - jnp.log1p(jnp.exp(-jnp.abs(x)))


def _softplus(x):
    return jnp.maximum(x, 0.0) + jnp.log1p(jnp.exp(-jnp.abs(x)))


def _silu(x):
    return x * jax.nn.sigmoid(x)


def _pick_lane(blk, idx):
    lane = lax.broadcasted_iota(jnp.int32, blk.shape, 1)
    return jnp.sum(jnp.where(lane == idx, blk, 0.0), axis=-1, keepdims=True)


def _norm_cast_kernel(y_ref, g_ref, o_ref):
    y = y_ref[...]
    inv = lax.rsqrt(jnp.mean(y * y, axis=-1, keepdims=True) + RMS_EPS)
    o_ref[...] = (y * inv * g_ref[...]).astype(o_ref.dtype)


def _norm_cast(y, g):
    m, d = y.shape
    tr = _tile(m, 256)
    return pl.pallas_call(
        _norm_cast_kernel,
        grid=(m // tr,),
        in_specs=[pl.BlockSpec((tr, d), lambda i: (i, 0)), pl.BlockSpec((1, d), lambda i: (0, 0))],
        out_specs=pl.BlockSpec((tr, d), lambda i: (i, 0)),
        out_shape=jax.ShapeDtypeStruct((m, d), BF16),
        compiler_params=_params(("parallel",), 2 * tr * d * 6),
        name="norm_cast",
    )(y, g.reshape(1, d))


def _resid_norm_kernel(d_ref, y_ref, gp_ref, gn_ref, yo_ref, xo_ref):
    d = d_ref[...]
    inv = lax.rsqrt(jnp.mean(d * d, axis=-1, keepdims=True) + RMS_EPS)
    yn = y_ref[...] + d * inv * gp_ref[...]
    yo_ref[...] = yn
    inv2 = lax.rsqrt(jnp.mean(yn * yn, axis=-1, keepdims=True) + RMS_EPS)
    xo_ref[...] = (yn * inv2 * gn_ref[...]).astype(xo_ref.dtype)


def _resid_norm(delta, y, g_post, g_next):
    m, d = y.shape
    tr = _tile(m, 256)
    row = pl.BlockSpec((tr, d), lambda i: (i, 0))
    vec = pl.BlockSpec((1, d), lambda i: (0, 0))
    return pl.pallas_call(
        _resid_norm_kernel,
        grid=(m // tr,),
        in_specs=[row, row, vec, vec],
        out_specs=[row, row],
        out_shape=[jax.ShapeDtypeStruct((m, d), F32), jax.ShapeDtypeStruct((m, d), BF16)],
        compiler_params=_params(("parallel",), 2 * tr * d * 14),
        name="resid_norm",
    )(delta, y, g_post.reshape(1, d), g_next.reshape(1, d))


def _mm_kernel(x_ref, w_ref, o_ref, *scratch, nk, act):
    part = jnp.dot(x_ref[...], w_ref[...].astype(BF16), preferred_element_type=F32)

    def finish(v):
        if act == "relu2":
            v = jnp.square(jnp.maximum(v, 0.0))
        o_ref[...] = v.astype(o_ref.dtype)

    if nk == 1:
        finish(part)
        return
    (acc_ref,) = scratch
    k = pl.program_id(2)

    @pl.when(k == 0)
    def _():
        acc_ref[...] = part

    @pl.when((k > 0) & (k < nk - 1))
    def _():
        acc_ref[...] += part

    @pl.when(k == nk - 1)
    def _():
        finish(acc_ref[...] + part)


def _matmul(x, w, *, layer=None, col0=0, n=None, out_dtype=F32, act=None, tm=1024, tn=512, tk=4096):
    m, kdim = x.shape
    n = w.shape[-1] - col0 if n is None else n
    tm, tn, tk = _tile(m, tm), _tile(n, tn), _tile(kdim, tk)
    assert col0 % tn == 0 and w.shape[-2] == kdim
    nk = kdim // tk
    cb = col0 // tn
    if layer is None:
        w_spec = pl.BlockSpec((tk, tn), lambda i, j, k: (k, cb + j))
    else:
        w_spec = pl.BlockSpec((None, tk, tn), lambda i, j, k: (layer, k, cb + j))
    osz = jnp.dtype(out_dtype).itemsize
    vmem = 2 * (tm * tk * 2 + tk * tn * 4 + tm * tn * osz) + tk * tn * 2 + 2 * tm * tn * 4
    return pl.pallas_call(
        functools.partial(_mm_kernel, nk=nk, act=act),
        grid=(m // tm, n // tn, nk),
        in_specs=[pl.BlockSpec((tm, tk), lambda i, j, k: (i, k)), w_spec],
        out_specs=pl.BlockSpec((tm, tn), lambda i, j, k: (i, j)),
        out_shape=jax.ShapeDtypeStruct((m, n), out_dtype),
        scratch_shapes=[pltpu.VMEM((tm, tn), F32)] if nk > 1 else [],
        compiler_params=_params(("parallel", "parallel", "arbitrary"), vmem),
        name="matmul",
    )(x, w)


def _mem_attn_kernel(q_ref, k_ref, v_ref, o_ref, *, heads):
    dh = q_ref.shape[-1] // heads
    scale = dh ** -0.5
    cols = [slice(hh * dh, (hh + 1) * dh) for hh in range(heads)]
    of = lambda ref: [ref[:, sl] for sl in cols]
    s = _each(lambda q, k: _dot_nt(q, k) * scale, of(q_ref), of(k_ref))
    p = _each(lambda s_: jnp.exp(s_ - jnp.max(s_, axis=-1, keepdims=True)), s)
    o = _each(lambda p_, v: _dot(p_, v) / jnp.sum(p_, axis=-1, keepdims=True), p, of(v_ref))
    for sl, o_ in zip(cols, o):
        o_ref[:, sl] = o_.astype(o_ref.dtype)


def _mem_attn(tail, k_arr, v_arr, kv_index, batch, t, heads, mem_width, mt):
    tt = _tile(t, 512)
    nt = t // tt
    lead = (None,) * (k_arr.ndim - 2)
    k_spec = pl.BlockSpec(lead + (mt, mem_width), lambda b, i: kv_index(b, 0))
    v_spec = pl.BlockSpec(lead + (mt, mem_width), lambda b, i: kv_index(b, 1))
    return pl.pallas_call(
        functools.partial(_mem_attn_kernel, heads=heads),
        grid=(batch, nt),
        in_specs=[pl.BlockSpec((tt, mem_width), lambda b, i: (b * nt + i, 0)), k_spec, v_spec],
        out_specs=pl.BlockSpec((tt, mem_width), lambda b, i: (b * nt + i, 0)),
        out_shape=jax.ShapeDtypeStruct((batch * t, mem_width), BF16),
        compiler_params=_params(("parallel", "parallel"), 2 * (tt * mem_width * 6 + 2 * mt * mem_width * 4) + 4 * tt * mt * 4),
        name="mem_attn",
    )(tail, k_arr, v_arr)


def _logf_kernel(f_ref, b_ref, o_ref):
    o_ref[...] = _log_sigmoid(f_ref[...] + b_ref[...])


def _fox_logf(tail, gate_block, bias_pad):
    m = tail.shape[0]
    tr = _tile(m, 1024)
    return pl.pallas_call(
        _logf_kernel,
        grid=(m // tr,),
        in_specs=[pl.BlockSpec((tr, LANES), lambda i: (i, gate_block)), pl.BlockSpec((1, LANES), lambda i: (0, 0))],
        out_specs=pl.BlockSpec((tr, LANES), lambda i: (i, 0)),
        out_shape=jax.ShapeDtypeStruct((m, LANES), F32),
        compiler_params=_params(("parallel",), 4 * tr * LANES * 4),
        name="fox_logf",
    )(tail, bias_pad)


def _cumsum_kernel(x_ref, o_ref, carry_ref, *, carry):
    tr = x_ref.shape[0]
    r = lax.broadcasted_iota(jnp.int32, (tr, tr), 0)
    c = lax.broadcasted_iota(jnp.int32, (tr, tr), 1)
    lower = (r >= c).astype(F32)
    cs = jnp.dot(lower, x_ref[...], precision=HIGHEST, preferred_element_type=F32)
    if carry:
        @pl.when(pl.program_id(1) == 0)
        def _():
            carry_ref[...] = jnp.zeros_like(carry_ref)

        cs = cs + carry_ref[...]
        carry_ref[...] = cs[tr - 1:tr, :]
    o_ref[...] = cs


def _cumsum_rows(x, batch, t, *, carry):
    tr = LANES
    nt = t // tr
    return pl.pallas_call(
        functools.partial(_cumsum_kernel, carry=carry),
        grid=(batch, nt),
        in_specs=[pl.BlockSpec((tr, LANES), lambda b, i: (b * nt + i, 0))],
        out_specs=pl.BlockSpec((tr, LANES), lambda b, i: (b * nt + i, 0)),
        out_shape=jax.ShapeDtypeStruct(x.shape, F32),
        scratch_shapes=[pltpu.VMEM((1, LANES), F32)],
        compiler_params=_params(("parallel", "arbitrary"), 8 * tr * LANES * 4),
        name="cumsum_rows",
    )(x)


def _fox_prompt_kernel(q_ref, k_ref, v_ref, c_ref, crow_ref, o_ref, kb_ref, vb_ref, m_ref, l_ref, acc_ref):
    h = pl.program_id(1)
    qi = pl.program_id(2)
    tq = q_ref.shape[0]
    scale = HEAD_DIM ** -0.5

    @pl.when(qi == 0)
    def _():
        kb_ref[...] = k_ref[...].astype(BF16)
        vb_ref[...] = v_ref[...].astype(BF16)

    q = q_ref[...].astype(BF16)
    cq = _pick_lane(c_ref[...], h)
    above_diag = lax.broadcasted_iota(jnp.int32, (tq, tq), 1) - lax.broadcasted_iota(jnp.int32, (tq, tq), 0)
    m_ref[...] = jnp.full_like(m_ref, -jnp.inf)
    l_ref[...] = jnp.zeros_like(l_ref)
    acc_ref[...] = jnp.zeros_like(acc_ref)

    def kv_tile(j, _):
        rows = pl.ds(pl.multiple_of(j * tq, tq), tq)
        s = lax.dot_general(q, kb_ref[rows, :], NT_DIMS, preferred_element_type=F32) * scale
        s = s + (cq - crow_ref[j])
        s = jnp.where(above_diag <= jnp.where(j < qi, tq, 0), s, -jnp.inf)
        m_old = m_ref[...]
        m_new = jnp.maximum(m_old, jnp.max(s, axis=-1, keepdims=True))
        alpha = jnp.exp(m_old - m_new)
        p = jnp.exp(s - m_new)
        l_ref[...] = alpha * l_ref[...] + jnp.sum(p, axis=-1, keepdims=True)
        acc_ref[...] = alpha * acc_ref[...] + jnp.dot(p.astype(BF16), vb_ref[rows, :], preferred_element_type=F32)
        m_ref[...] = m_new
        return 0

    lax.fori_loop(0, qi + 1, kv_tile, 0)
    o_ref[...] = (acc_ref[...] / l_ref[...]).astype(o_ref.dtype)


def _fox_prompt(q, k, v, c_tok, c_row, batch, t, heads):
    tq = c_row.shape[-1]
    nq = t // tq
    blk = pl.BlockSpec((tq, HEAD_DIM), lambda b, h, i: (b * nq + i, h))
    full = pl.BlockSpec((t, HEAD_DIM), lambda b, h, i: (b, h))
    return pl.pallas_call(
        _fox_prompt_kernel,
        grid=(batch, heads, nq),
        in_specs=[blk, full, full,
                  pl.BlockSpec((tq, LANES), lambda b, h, i: (b * nq + i, 0)),
                  pl.BlockSpec((None, None, nq, 1, tq), lambda b, h, i: (b, h, 0, 0, 0))],
        out_specs=blk,
        out_shape=jax.ShapeDtypeStruct((batch * t, heads * HEAD_DIM), BF16),
        scratch_shapes=[pltpu.VMEM((t, HEAD_DIM), BF16), pltpu.VMEM((t, HEAD_DIM), BF16),
                        pltpu.VMEM((tq, 1), F32), pltpu.VMEM((tq, 1), F32), pltpu.VMEM((tq, HEAD_DIM), F32)],
        compiler_params=_params(("parallel", "parallel", "arbitrary"), 5 * t * HEAD_DIM * 4 + 8 * tq * tq * 4),
        name="fox_prompt",
    )(q, k, v, c_tok, c_row)


def _fox_decode_kernel(pt_ref, q_ref, kn_ref, vn_ref, lfn_ref, kp_ref, vp_ref, lfp_ref, o_ref,
                       m_ref, l_ref, acc_ref, carry_ref, *, n_pages, heads, t_new, page):
    del pt_ref
    p = pl.program_id(1)
    scale = HEAD_DIM ** -0.5
    r = lax.broadcasted_iota(jnp.int32, (page, page), 0)
    c = lax.broadcasted_iota(jnp.int32, (page, page), 1)
    upper = (r <= c).astype(F32)

    @pl.when(p == 0)
    def _():
        m_ref[...] = jnp.full_like(m_ref, -jnp.inf)
        l_ref[...] = jnp.zeros_like(l_ref)
        acc_ref[...] = jnp.zeros_like(acc_ref)
        carry_ref[...] = jnp.zeros_like(carry_ref)

    head_cols = lambda h: slice(h * HEAD_DIM, (h + 1) * HEAD_DIM)

    def attend(k_heads, v_heads, lf_t, mask):
        c_t = jnp.dot(lf_t, upper, precision=HIGHEST, preferred_element_type=F32) + carry_ref[...]
        carry_ref[...] = c_t[:, page - 1:page]
        s = jnp.stack([_dot_nt(q_ref[:, head_cols(h)], k_heads[h]) for h in range(heads)])
        s = s * scale - c_t[:, None, :]
        if mask is not None:
            s = jnp.where(mask[None], s, -jnp.inf)
        m_old = m_ref[...]
        m_new = jnp.maximum(m_old, jnp.max(s, axis=-1, keepdims=True))
        alpha = jnp.exp(m_old - m_new)
        pr = jnp.exp(s - m_new)
        l_ref[...] = alpha * l_ref[...] + jnp.sum(pr, axis=-1, keepdims=True)
        pv = jnp.stack([_dot(pr[h], v_heads[h]) for h in range(heads)])
        acc_ref[...] = alpha * acc_ref[...] + pv
        m_ref[...] = m_new

    @pl.when(p < n_pages)
    def _():
        attend([kp_ref[pl.ds(h, page, stride=heads), :] for h in range(heads)],
               [vp_ref[pl.ds(h, page, stride=heads), :] for h in range(heads)], lfp_ref[...], None)

    @pl.when(p == n_pages)
    def _():
        pad = jnp.zeros((page - t_new, HEAD_DIM), F32)
        qrow = lax.broadcasted_iota(jnp.int32, (t_new, page), 0)
        kcol = lax.broadcasted_iota(jnp.int32, (t_new, page), 1)
        attend([jnp.concatenate([kn_ref[:, head_cols(h)], pad], axis=0) for h in range(heads)],
               [jnp.concatenate([vn_ref[:, head_cols(h)], pad], axis=0) for h in range(heads)],
               lfn_ref[...], kcol <= qrow)
        out = acc_ref[...] / l_ref[...]
        for h in range(heads):
            o_ref[:, head_cols(h)] = out[h].astype(o_ref.dtype)


def _fox_decode(q, k, v, lfn_t, k_pool, v_pool, lf_pool_t, page_table, layer, batch, t_new, heads):
    n_pages = page_table.shape[1]
    page = k_pool.shape[2] // heads
    width = heads * HEAD_DIM
    assert page == LANES and t_new <= page
    last = n_pages - 1
    new = pl.BlockSpec((t_new, width), lambda b, p, pt: (b, 0))
    pool =pl.BlockSpec((None, None, page * heads, HEAD_DIM),
                        lambda b, p, pt: (layer, pt[b, jnp.minimum(p, last)], 0, 0))
    lf_pool = pl.BlockSpec((None, None, heads, page), lambda b, p, pt: (layer, pt[b, jnp.minimum(p, last)], 0, 0))
    grid_spec = pltpu.PrefetchScalarGridSpec(
        num_scalar_prefetch=1,
        grid=(batch, n_pages + 1),
        in_specs=[new, new, new, pl.BlockSpec((None, heads, page), lambda b, p, pt: (b, 0, 0)),
                  pool, pool, lf_pool],
        out_specs=new,
        scratch_shapes=[pltpu.VMEM((heads, t_new, 1), F32), pltpu.VMEM((heads, t_new, 1), F32),
                        pltpu.VMEM((heads, t_new, HEAD_DIM), F32), pltpu.VMEM((heads, 1), F32)],
    )
    return pl.pallas_call(
        functools.partial(_fox_decode_kernel, n_pages=n_pages, heads=heads, t_new=t_new, page=page),
        grid_spec=grid_spec,
        out_shape=jax.ShapeDtypeStruct((batch * t_new, width), BF16),
        compiler_params=_params(("parallel", "arbitrary"), 4 * page * width * 4 + 8 * t_new * width * 4 + (4 << 20)),
        name="fox_decode",
    )(page_table, q, k, v, lfn_t, k_pool, v_pool, lf_pool_t)


def _shortconv_kernel(gb_ref, gc_ref, h_ref, buf_ref, w_ref, o_ref, tail_ref, us_ref, *, width):
    tt = gb_ref.shape[0]

    @pl.when(pl.program_id(2) == 0)
    def _():
        us_ref[0:SUBLANES, :] = buf_ref[...]

    us_ref[SUBLANES:SUBLANES + tt, :] = gc_ref[...] * h_ref[...]
    conv = None
    for k in range(width):
        off = SUBLANES - (width - 1) + k
        term = w_ref[k:k + 1, :] * us_ref[off:off + tt, :]
        conv = term if conv is None else conv + term
    o_ref[...] = (gb_ref[...] * conv).astype(o_ref.dtype)
    last = us_ref[tt:tt + SUBLANES, :]
    tail_ref[...] = last
    us_ref[0:SUBLANES, :] = last


def _shortconv(proj, buf8, w8, batch, t, mix_width, width):
    tt = _tile(t, 512)
    tc = _tile(mix_width, 512)
    nt, nc = t // tt, mix_width // tc
    col = lambda off: pl.BlockSpec((tt, tc), lambda b, c, i: (b * nt + i, off * nc + c))
    return pl.pallas_call(
        functools.partial(_shortconv_kernel, width=width),
        grid=(batch, nc, nt),
        in_specs=[col(0), col(1), col(2),
                  pl.BlockSpec((None, SUBLANES, tc), lambda b, c, i: (b, 0, c)),
                  pl.BlockSpec((SUBLANES, tc), lambda b, c, i: (0, c))],
        out_specs=[col(0), pl.BlockSpec((None, SUBLANES, tc), lambda b, c, i: (b, 0, c))],
        out_shape=[jax.ShapeDtypeStruct((batch * t, mix_width), BF16),
                   jax.ShapeDtypeStruct((batch, SUBLANES, mix_width), F32)],
        scratch_shapes=[pltpu.VMEM((tt + SUBLANES, tc), F32)],
        compiler_params=_params(("parallel", "parallel", "arbitrary"), 12 * tt * tc * 4),
        name="shortconv",
    )(proj, proj, proj, buf8, w8)


def _gdn_conv_kernel(x_ref, buf_ref, w_ref, o_ref, xs_ref, *, width, l2norm, scale):
    tt, tc = x_ref.shape

    @pl.when(pl.program_id(2) == 0)
    def _():
        xs_ref[0:SUBLANES, :] = buf_ref[...]

    xs_ref[SUBLANES:SUBLANES + tt, :] = x_ref[...]
    conv = None
    for k in range(width):
        off = SUBLANES - (width - 1) + k
        term = w_ref[k:k + 1, :] * xs_ref[off:off + tt, :]
        conv = term if conv is None else conv + term
    y = _silu(conv)
    if l2norm:
        for g in range(tc // HEAD_DIM):
            sl = slice(g * HEAD_DIM, (g + 1) * HEAD_DIM)
            seg = y[:, sl]
            seg = seg * lax.rsqrt(jnp.sum(seg * seg, axis=-1, keepdims=True) + L2_EPS)
            o_ref[:, sl] = seg * scale if scale != 1.0 else seg
    else:
        o_ref[...] = y
    xs_ref[0:SUBLANES, :] = xs_ref[tt:tt + SUBLANES, :]


def _gdn_conv(proj, buf8, w8, part, batch, t, mix_width, width, *, l2norm, scale=1.0):
    tt = _tile(t, 512)
    tc = _tile(mix_width, 512)
    nt, nc = t // tt, mix_width // tc
    return pl.pallas_call(
        functools.partial(_gdn_conv_kernel, width=width, l2norm=l2norm, scale=scale),
        grid=(batch, nc, nt),
        in_specs=[pl.BlockSpec((tt, tc), lambda b, c, i: (b * nt + i, part * nc + c)),
                  pl.BlockSpec((None, SUBLANES, tc), lambda b, c, i: (b, 0, part * nc + c)),
                  pl.BlockSpec((SUBLANES, tc), lambda b, c, i: (0, part * nc + c))],
        out_specs=pl.BlockSpec((tt, tc), lambda b, c, i: (b * nt + i, c)),
        out_shape=jax.ShapeDtypeStruct((batch * t, mix_width), F32),
        scratch_shapes=[pltpu.VMEM((tt + SUBLANES, tc), F32)],
        compiler_params=_params(("parallel", "parallel", "arbitrary"), 10 * tt * tc * 4),
        name="gdn_conv",
    )(proj, buf8, w8)


def _gdn_gate_kernel(a_ref, bt_ref, alog_ref, dt_ref, g_ref, beta_ref):
    g_ref[...] = -jnp.exp(alog_ref[...]) * _softplus(a_ref[...] + dt_ref[...])
    beta_ref[...] = jax.nn.sigmoid(bt_ref[...])


def _gdn_gates(tail, a_block, bt_block, a_log_pad, dt_pad):
    m = tail.shape[0]
    tr = _tile(m, 1024)
    blk = lambda off: pl.BlockSpec((tr, LANES), lambda i: (i, off))
    vec = pl.BlockSpec((1, LANES), lambda i: (0, 0))
    return pl.pallas_call(
        _gdn_gate_kernel,
        grid=(m // tr,),
        in_specs=[blk(a_block), blk(bt_block), vec, vec],
        out_specs=[blk(0), blk(0)],
        out_shape=[jax.ShapeDtypeStruct((m, LANES), F32)] * 2,
        compiler_params=_params(("parallel",), 8 * tr * LANES * 4),
        name="gdn_gates",
    )(tail, tail, a_log_pad, dt_pad)


def _unit_lower_inverse_offdiag(mats, ri, ci):
    n = mats[0].shape[0]
    blk = lambda idx, size: jnp.right_shift(idx, size.bit_length() - 1)
    base = SUBLANES
    in_base = blk(ri, base) == blk(ci, base)
    a0 = _each(lambda a: jnp.where(in_base, a, 0.0), mats)
    low = _each(lambda a: -a, a0)
    pw = _each(_dot, a0, a0)
    low = _each(lambda l, p, lp: l + p + lp, low, pw, _each(_dot, low, pw))
    pw = _each(_dot, pw, pw)
    low = _each(lambda l, p, lp: l + p + lp, low, pw, _each(_dot, low, pw))
    s = base
    while s < n:
        sub = (blk(ri, 2 * s) == blk(ci, 2 * s)) & (blk(ri, s) != blk(ci, s))
        off = _each(lambda a: jnp.where(sub, a, 0.0), mats)
        x = _each(lambda o, lo: o + lo, off, _each(_dot, low, off))
        low = _each(lambda l, x_, xl: l - (x_ + xl), low, x, _each(_dot, x, low))
        s *= 2
    return low


def _gdn_intra_kernel(q_ref, k_ref, v_ref, gc_ref, beta_ref, u_ref, w_ref, qk_ref, qg_ref, kg_ref, *, group):
    hg = pl.program_id(1)
    n = q_ref.shape[0]
    ri = lax.broadcasted_iota(jnp.int32, (n, n), 0)
    ci = lax.broadcasted_iota(jnp.int32, (n, n), 1)
    incl = ri >= ci
    cols = [slice(i * HEAD_DIM, (i + 1) * HEAD_DIM) for i in range(group)]
    q = [q_ref[:, sl] for sl in cols]
    k = [k_ref[:, sl] for sl in cols]
    v = [v_ref[:, sl] for sl in cols]
    gcol = [_pick_lane(gc_ref[...], hg * group + i) for i in range(group)]
    bcol = [_pick_lane(beta_ref[...], hg * group + i) for i in range(group)]

    def decay_of(g):
        gmat = jnp.broadcast_to(g, (n, n))
        return jnp.where(incl, jnp.exp(jnp.where(incl, gmat - gmat.T, 0.0)), 0.0)

    decay = _each(decay_of, gcol)
    kb = _each(lambda k_, b: k_ * b, k, bcol)
    a = _each(lambda kk, d: jnp.where(ri > ci, kk * d, 0.0), _each(_dot_nt, kb, k), decay)
    low = _unit_lower_inverse_offdiag(a, ri, ci)
    eg = _each(jnp.exp, gcol)
    rhs = _each(lambda v_, b, kb_, e: jnp.concatenate([v_ * b, kb_ * e], axis=1), v, bcol, kb, eg)
    uw = _each(lambda r, lr: r + lr, rhs, _each(_dot, low, rhs))
    qk = _each(lambda x, d: x * d, _each(_dot_nt, q, k), decay)
    for i, sl in enumerate(cols):
        u_ref[:, sl] = uw[i][:, :HEAD_DIM]
        w_ref[:, sl] = uw[i][:, HEAD_DIM:].astype(w_ref.dtype)
        qk_ref[:, sl] = qk[i].astype(qk_ref.dtype)
        qg_ref[:, sl] = (q[i] * eg[i]).astype(qg_ref.dtype)
        kg_ref[:, sl] = (k[i] * jnp.exp(gcol[i][n - 1:n, :] - gcol[i])).astype(kg_ref.dtype)


def _gdn_intra(qn, kn, vc, gc, beta, batch, t, heads, group):
    n = GDN_CHUNK
    nc = t // n
    wide = pl.BlockSpec((n, group * HEAD_DIM), lambda b, g, c: (b * nc + c, g))
    gate = pl.BlockSpec((n, LANES), lambda b, g, c: (b * nc + c, 0))
    shape = lambda dt: jax.ShapeDtypeStruct((batch * t, heads * HEAD_DIM), dt)
    return pl.pallas_call(
        functools.partial(_gdn_intra_kernel, group=group),
        grid=(batch, heads // group, nc),
        in_specs=[wide, wide, wide, gate, gate],
        out_specs=[wide] * 5,
        out_shape=[shape(F32)] + [shape(BF16)] * 4,
        compiler_params=_params(("parallel", "parallel", "parallel"), 2 * 8 * n * group * HEAD_DIM * 4 + (8 << 20)),
        name="gdn_intra",
    )(qn, kn, vc, gc, beta)


def _gdn_state_kernel(u_ref, w_ref, qk_ref, qg_ref, kg_ref, gc_ref, z_ref, gout_ref, s0_ref, o_ref, sout_ref,
                      s_ref, *, group, n_chunks):
    hg = pl.program_id(1)
    c = pl.program_id(2)
    n = u_ref.shape[0]

    @pl.when(c == 0)
    def _():
        s_ref[...] = s0_ref[...]

    cols = [slice(i * HEAD_DIM, (i + 1) * HEAD_DIM) for i in range(group)]
    of = lambda ref: [ref[:, sl] for sl in cols]
    s = [s_ref[i] for i in range(group)]
    v_new = _each(lambda u, ws: u - ws, of(u_ref), _each(_dot, of(w_ref), s))
    o = _each(lambda x, y: x + y, _each(_dot, of(qg_ref), s), _each(_dot, of(qk_ref), v_new))
    upd = _each(_dot_tn, of(kg_ref), v_new)
    for i, sl in enumerate(cols):
        g_last = _pick_lane(gc_ref[n - 1:n, :], hg * group + i)
        s_ref[i] = s[i] * jnp.exp(g_last) + upd[i]
        on = o[i] * lax.rsqrt(jnp.mean(o[i] * o[i], axis=-1, keepdims=True) + RMS_EPS) * gout_ref[...]
        o_ref[:, sl] = (on * _silu(z_ref[:, sl])).astype(o_ref.dtype)

    @pl.when(c == n_chunks - 1)
    def _():
        sout_ref[...] = s_ref[...]


def _gdn_state(u, w, qk, qg, kg, gc, z_arr, z_block0, g_out, s0, batch, t, heads, group):
    n = GDN_CHUNK
    nc = t // n
    gw = group * HEAD_DIM
    wide = pl.BlockSpec((n, gw), lambda b, g, c: (b * nc + c, g))
    state = pl.BlockSpec((None, group, HEAD_DIM, HEAD_DIM), lambda b, g, c: (b, g, 0, 0))
    return pl.pallas_call(
        functools.partial(_gdn_state_kernel, group=group, n_chunks=nc),
        grid=(batch, heads // group, nc),
        in_specs=[wide] * 5 + [pl.BlockSpec((n, LANES), lambda b, g, c: (b * nc + c, 0)),
                               pl.BlockSpec((n, gw), lambda b, g, c: (b * nc + c, z_block0 + g)),
                               pl.BlockSpec((1, HEAD_DIM), lambda b, g, c: (0, 0)), state],
        out_specs=[wide, state],
        out_shape=[jax.ShapeDtypeStruct((batch * t, heads * HEAD_DIM), BF16),
                   jax.ShapeDtypeStruct((batch, heads, HEAD_DIM, HEAD_DIM), F32)],
        scratch_shapes=[pltpu.VMEM((group, HEAD_DIM, HEAD_DIM), F32)],
        compiler_params=_params(("parallel", "parallel", "arbitrary"), 2 * 8 * n * gw * 4 + 6 * group * HEAD_DIM * HEAD_DIM * 4 + (4 << 20)),
        name="gdn_state",
    )(u, w, qk, qg, kg, gc, z_arr, g_out.reshape(1, HEAD_DIM), s0)


def _pad_lanes(v, width=LANES):
    return jnp.pad(v, [(0, 0)] * (v.ndim - 1) + [(0, width - v.shape[-1])])


def _pad_rows8(a, rows_axis):
    pad = [(0, 0)] * a.ndim
    pad[rows_axis] = (SUBLANES - a.shape[rows_axis], 0)
    return jnp.pad(a, pad)


def _tail_weight(w_in, main, parts, mem_width):
    cols = [w_in[:, w_in.shape[1] - mem_width:]]
    off = main
    for width in parts:
        cols.append(_pad_lanes(w_in[:, off:off + width]))
        off += width
    return jnp.concatenate(cols, axis=1)


def _in_proj(x, w_stack, j, main, w_tail):
    tn_tail = 256 if w_tail.shape[1] % 256 == 0 else LANES
    return (_matmul(x, w_stack, layer=j, col0=0, n=main),
            _matmul(x, w_tail, tn=tn_tail))


def _fox_mixer(xs, w_in_fox, b_fox_f, j, caches, page_table, heads, mem_width):
    mix_width = heads * HEAD_DIM
    main = 3 * mix_width
    w_tail = _tail_weight(w_in_fox[j], main, [heads], mem_width)
    gate_block = mem_width // LANES
    bias = _pad_lanes(b_fox_f[j].reshape(1, heads))
    out = {}
    tn_tail = 256 if w_tail.shape[1] % 256 == 0 else LANES
    for name, (x, batch, t) in xs.items():
        q, k, v = (_matmul(x, w_in_fox, layer=j, col0=part * mix_width, n=mix_width) for part in range(3))
        tail = _matmul(x, w_tail, tn=tn_tail)
        logf = _fox_logf(tail, gate_block, bias)
        if name == "prompt":
            c_tok = _cumsum_rows(logf, batch, t, carry=True)
            tq = _tile(t, 512)
            c_row = jnp.transpose(c_tok.reshape(batch, t, LANES)[:, :, :heads], (0, 2, 1))
            c_row = c_row.reshape(batch, heads, t // tq, 1, tq)
            mix = _fox_prompt(q, k, v, c_tok, c_row, batch, t, heads)
        else:
            k_pool, v_pool, lf_pool_t = caches
            lfn_t = jnp.transpose(logf.reshape(batch, t, LANES)[:, :, :heads], (0, 2, 1))
            lfn_t = _pad_lanes(lfn_t, lf_pool_t.shape[-1])
            mix = _fox_decode(q, k, v, lfn_t, k_pool, v_pool, lf_pool_t, page_table, j, batch, t, heads)
        shp = (batch, t, heads, HEAD_DIM)
        out[name] = dict(mix=mix, tail=tail, k=k.reshape(shp), v=v.reshape(shp),
                         logf=logf[:, :heads].reshape(batch, t, heads))
    return out


def _conv_mixer(xs, w_in_conv, w_conv, j, state_conv, heads, mem_width):
    mix_width = heads * HEAD_DIM
    main = 3 * mix_width
    width = w_conv.shape[1]
    w_tail = w_in_conv[j][:, main:]
    w8 = jnp.pad(w_conv[j], ((0, SUBLANES - width), (0, 0)))
    out = {}
    for name, (x, batch, t) in xs.items():
        proj, tail = _in_proj(x, w_in_conv, j, main, w_tail)
        if name == "prompt":
            buf8 = jnp.zeros((batch, SUBLANES, mix_width), F32)
        else:
            buf8 = _pad_rows8(state_conv[j], 1)
        mix, last8 = _shortconv(proj, buf8, w8, batch, t, mix_width, width)
        out[name] = dict(mix=mix, tail=tail, buf=last8[:, SUBLANES - (width - 1):, :])
    return out


def _gdn_mixer(xs, w_in_gdn, w_gdn_conv, gdn_a_log, gdn_dt_bias, g_gdn_out, j, state_gdn_conv, state_gdn_s,
               heads, mem_width):
    mix_width = heads * HEAD_DIM
    main = 4 * mix_width
    width = w_gdn_conv.shape[1]
    w_tail = _tail_weight(w_in_gdn[j], main, [heads, heads], mem_width)
    a_block = mem_width // LANES
    w8 = jnp.pad(w_gdn_conv[j], ((0, SUBLANES - width), (0, 0)))
    a_log = _pad_lanes(gdn_a_log[j].reshape(1, heads))
    dt_bias = _pad_lanes(gdn_dt_bias[j].reshape(1, heads))
    group = next(g for g in (8, 4, 2, 1) if heads % g == 0)
    out = {}
    for name, (x, batch, t) in xs.items():
        proj, tail = _in_proj(x, w_in_gdn, j, main, w_tail)
        if name == "prompt":
            buf = jnp.zeros((batch, width - 1, 3 * mix_width), F32)
            s0 = jnp.zeros((batch, heads, HEAD_DIM, HEAD_DIM), F32)
        else:
            buf, s0 = state_gdn_conv[j], state_gdn_s[j]
        buf8 = _pad_rows8(buf, 1)
        conv = functools.partial(_gdn_conv, proj, buf8, w8, batch=batch, t=t, mix_width=mix_width, width=width)
        qn = conv(part=0, l2norm=True, scale=HEAD_DIM ** -0.5)
        kn = conv(part=1, l2norm=True)
        vc = conv(part=2, l2norm=False)
        g, beta = _gdn_gates(tail, a_block, a_block + 1, a_log, dt_bias)
        z_arr, z_block0 = proj, 3 * mix_width // (group * HEAD_DIM)
        tp = -(-t // GDN_CHUNK) * GDN_CHUNK
        if tp != t:
            pad = lambda a_: jnp.pad(a_.reshape(batch, t, -1), ((0, 0), (0, tp - t), (0, 0))).reshape(batch * tp, -1)
            qn, kn, vc, g, beta = (pad(a_) for a_ in (qn, kn, vc, g, beta))
            z_arr, z_block0 = pad(proj[:, 3 * mix_width:main]), 0
        gc = _cumsum_rows(g, batch, tp, carry=False)
        u, w, qk, qg, kg = _gdn_intra(qn, kn, vc, gc, beta, batch, tp, heads, group)
        mix, s_new = _gdn_state(u, w, qk, qg, kg, gc, z_arr, z_block0, g_gdn_out[j], s0, batch, tp, heads, group)
        if tp != t:
            mix = mix.reshape(batch, tp, mix_width)[:, :t].reshape(batch * t, mix_width)
        keep = min(t, width - 1)
        newest = proj.reshape(batch, t, -1)[:, t - keep:, :3 * mix_width]
        out[name] = dict(mix=mix, tail=tail, conv=jnp.concatenate([buf, newest], axis=1)[:, keep:], s=s_new)
    return out


def kernel(x_prompt, x_sample, cache_fox_k, cache_fox_v, cache_fox_logf, cache_mem_k, cache_mem_v, state_conv, state_gdn_conv, state_gdn_s, page_table, mem_prompt, g_mix_pre, g_mix_post, g_mlp_pre, g_mlp_post, g_mem, w_mem_kv, w_out, w_mlp_up, w_mlp_down, w_in_fox, b_fox_f, w_in_conv, w_conv, w_in_gdn, w_gdn_conv, gdn_a_log, gdn_dt_bias, g_gdn_out):
    bp, tp_, d = x_prompt.shape
    bs, ts, _ = x_sample.shape
    depth = g_mix_pre.shape[0]
    mem_tokens, mem_heads, mem_hd = cache_mem_k.shape[2:]
    mem_width = mem_heads * mem_hd
    mix_width = d - mem_width
    heads = mix_width // HEAD_DIM
    n_fox, n_pool, page = cache_fox_k.shape[:3]

    y = {"prompt": x_prompt.reshape(bp * tp_, d), "sample": x_sample.reshape(bs * ts, d)}
    dims = {"prompt": (bp, tp_), "sample": (bs, ts)}
    x = {n: _norm_cast(y[n], g_mix_pre[0]) for n in y}
    mem2d = mem_prompt.reshape(bp * mem_tokens, d)
    caches = (cache_fox_k.reshape(n_fox, n_pool, page * heads, HEAD_DIM),
              cache_fox_v.reshape(n_fox, n_pool, page * heads, HEAD_DIM),
              jnp.transpose(cache_fox_logf, (0, 1, 3, 2)))
    mem_k_s = cache_mem_k.reshape(depth, bs, mem_tokens, mem_width)
    mem_v_s = cache_mem_v.reshape(depth, bs, mem_tokens, mem_width)

    res = {k_: [] for k_ in ("fox_k_p", "fox_v_p", "fox_lf_p", "fox_k_s", "fox_v_s", "fox_lf_s", "conv_p", "conv_s",
                             "gconv_p", "gconv_s", "gs_p", "gs_s", "mem_k", "mem_v")}
    for i in range(depth):
        kind, j = i % 3, i // 3
        xs = {n: (x[n], *dims[n]) for n in x}
        kv = _matmul(_norm_cast(mem2d, g_mem[i]), w_mem_kv, layer=i)
        res["mem_k"].append(kv[:, :mem_width].reshape(bp, mem_tokens, mem_heads, mem_hd))
        res["mem_v"].append(kv[:, mem_width:].reshape(bp, mem_tokens, mem_heads, mem_hd))
        if kind == 0:
            mixed = _fox_mixer(xs, w_in_fox, b_fox_f, j, caches, page_table, heads, mem_width)
            for n, s in (("prompt", "p"), ("sample", "s")):
                res["fox_k_" + s].append(mixed[n]["k"])
                res["fox_v_" + s].append(mixed[n]["v"])
                res["fox_lf_" + s].append(mixed[n]["logf"])
        elif kind == 1:
            mixed = _conv_mixer(xs, w_in_conv, w_conv, j, state_conv, heads, mem_width)
            res["conv_p"].append(mixed["prompt"]["buf"])
            res["conv_s"].append(mixed["sample"]["buf"])
        else:
            mixed = _gdn_mixer(xs, w_in_gdn, w_gdn_conv, gdn_a_log, gdn_dt_bias, g_gdn_out, j, state_gdn_conv,
                               state_gdn_s, heads, mem_width)
            for n, s in (("prompt", "p"), ("sample", "s")):
                res["gconv_" + s].append(mixed[n]["conv"])
                res["gs_" + s].append(mixed[n]["s"])
        g_next = g_mix_pre[i + 1] if i + 1 < depth else g_mlp_pre[i]
        for n in ("prompt", "sample"):
            b, t = dims[n]
            if n == "prompt":
                mem = _mem_attn(mixed[n]["tail"], kv, kv, lambda bb, part: (bb, part), b, t, mem_heads, mem_width,
                                mem_tokens)
            else:
                mem = _mem_attn(mixed[n]["tail"], mem_k_s, mem_v_s, lambda bb, part: (i, bb, 0, 0), b, t, mem_heads,
                                mem_width, mem_tokens)
            cat = jnp.concatenate([mixed[n]["mix"], mem], axis=1)
            delta = _matmul(cat, w_out, layer=i)
            y[n], x2 = _resid_norm(delta, y[n], g_mix_post[i], g_mlp_pre[i])
            hid = _matmul(x2, w_mlp_up, layer=i, out_dtype=BF16, act="relu2")
            delta = _matmul(hid, w_mlp_down, layer=i)
            y[n], x[n] = _resid_norm(delta, y[n], g_mlp_post[i], g_next)

    st = lambda k_: jnp.stack(res[k_])
    return (y["prompt"].reshape(bp, tp_, d), y["sample"].reshape(bs, ts, d),
            st("fox_k_p"), st("fox_v_p"), st("fox_lf_p"), st("conv_p"), st("gconv_p"), st("gs_p"),
            st("mem_k"), st("mem_v"),
            st("fox_k_s"), st("fox_v_s"), st("fox_lf_s"), st("conv_s"), st("gconv_s"), st("gs_s"))
```

```python
import functools

import jax
import jax.numpy as jnp
from jax import lax
from jax.experimental import pallas as pl
from jax.experimental.pallas import tpu as pltpu

F32 = jnp.float32
BF16 = jnp.bfloat16
RMS_EPS = 1e-6
L2_EPS = 1e-6
HEAD_DIM = 128
LANES = 128
SUBLANES = 8
GDN_CHUNK = 128
V7X_VMEM_BYTES = 64 * 1024 * 1024
VMEM_CAP = V7X_VMEM_BYTES - 6 * 1024 * 1024
HIGHEST = lax.Precision.HIGHEST
NT_DIMS = (((1,), (1,)), ((), ()))
TN_DIMS = (((0,), (0,)), ((), ()))


def _params(semantics, vmem_bytes):
    limit = int(min(VMEM_CAP, max(16 * 1024 * 1024, vmem_bytes * 5 // 4 + (2 << 20))))
    return pltpu.CompilerParams(dimension_semantics=semantics, vmem_limit_bytes=limit)


def _tile(dim, pref, align=LANES):
    if dim <= pref:
        return dim
    for cand in range(pref - pref % align, 0, -align):
        if dim % cand == 0:
            return cand
    raise ValueError((dim, pref, align))


def _dot(a, b):
    return jnp.dot(a.astype(BF16), b.astype(BF16), preferred_element_type=F32)


def _dot_nt(a, b):
    return lax.dot_general(a.astype(BF16), b.astype(BF16), NT_DIMS, preferred_element_type=F32)


def _dot_tn(a, b):
    return lax.dot_general(a.astype(BF16), b.astype(BF16), TN_DIMS, preferred_element_type=F32)


def _each(fn, *columns):
    return [fn(*args) for args in zip(*columns)]


def _log_sigmoid(x):
    return jnp.minimum(x, 0.0) - jnp.log1p(jnp.exp(-jnp.abs(x)))


def _softplus(x):
    return jnp.maximum(x, 0.0) + jnp.log1p(jnp.exp(-jnp.abs(x)))


def _silu(x):
    return x * jax.nn.sigmoid(x)


def _pick_lane(blk, idx):
    lane = lax.broadcasted_iota(jnp.int32, blk.shape, 1)
    return jnp.sum(jnp.where(lane == idx, blk, 0.0), axis=-1, keepdims=True)


def _norm_cast_kernel(y_ref, g_ref, o_ref):
    y = y_ref[...]
    inv = lax.rsqrt(jnp.mean(y * y, axis=-1, keepdims=True) + RMS_EPS)
    o_ref[...] = (y * inv * g_ref[...]).astype(o_ref.dtype)


def _norm_cast(y, g):
    m, d = y.shape
    tr = _tile(m, 256)
    return pl.pallas_call(
        _norm_cast_kernel,
        grid=(m // tr,),
        in_specs=[pl.BlockSpec((tr, d), lambda i: (i, 0)), pl.BlockSpec((1, d), lambda i: (0, 0))],
        out_specs=pl.BlockSpec((tr, d), lambda i: (i, 0)),
        out_shape=jax.ShapeDtypeStruct((m, d), BF16),
        compiler_params=_params(("parallel",), 2 * tr * d * 6),
        name="norm_cast",
    )(y, g.reshape(1, d))


def _resid_norm_kernel(d_ref, y_ref, gp_ref, gn_ref, yo_ref, xo_ref):
    d = d_ref[...]
    inv = lax.rsqrt(jnp.mean(d * d, axis=-1, keepdims=True) + RMS_EPS)
    yn = y_ref[...] + d * inv * gp_ref[...]
    yo_ref[...] = yn
    inv2 = lax.rsqrt(jnp.mean(yn * yn, axis=-1, keepdims=True) + RMS_EPS)
    xo_ref[...] = (yn * inv2 * gn_ref[...]).astype(xo_ref.dtype)


def _resid_norm(delta, y, g_post, g_next):
    m, d = y.shape
    tr = _tile(m, 256)
    row = pl.BlockSpec((tr, d), lambda i: (i, 0))
    vec = pl.BlockSpec((1, d), lambda i: (0, 0))
    return pl.pallas_call(
        _resid_norm_kernel,
        grid=(m // tr,),
        in_specs=[row, row, vec, vec],
        out_specs=[row, row],
        out_shape=[jax.ShapeDtypeStruct((m, d), F32), jax.ShapeDtypeStruct((m, d), BF16)],
        compiler_params=_params(("parallel",), 2 * tr * d * 14),
        name="resid_norm",
    )(delta, y, g_post.reshape(1, d), g_next.reshape(1, d))


def _mm_kernel(*refs, nk, act, rider):
    if rider:
        x_ref, xr_ref, w_ref, o_ref, or_ref = refs
    else:
        x_ref, w_ref, o_ref = refs
    wb = w_ref[...].astype(BF16)
    k = pl.program_id(2)

    def product(lhs_ref, out_ref):
        part = jnp.dot(lhs_ref[...], wb, preferred_element_type=F32)
        if nk == 1:
            if act == "relu2":
                part = jnp.square(jnp.maximum(part, 0.0))
            out_ref[...] = part.astype(out_ref.dtype)
        else:
            @pl.when(k == 0)
            def _():
                out_ref[...] = jnp.zeros_like(out_ref)

            out_ref[...] += part

    product(x_ref, o_ref)
    if rider:
        i = pl.program_id(0)

        @pl.when(i == 0)
        def _():
            product(xr_ref, or_ref)

        @pl.when(i > 0)
        def _():
            or_ref[...] = jnp.zeros_like(or_ref)


def _matmul(x, w, *, rider=None, layer=None, col0=0, n=None, out_dtype=F32, act=None, tm=1024, tn=512, tk=4096):
    m, kdim = x.shape
    n = w.shape[-1] - col0 if n is None else n
    tm, tn, tk = _tile(m, tm), _tile(n, tn), _tile(kdim, tk)
    assert col0 % tn == 0 and w.shape[-2] == kdim
    nk = kdim // tk
    assert nk == 1 or (act is None and out_dtype == F32)
    cb = col0 // tn
    if layer is None:
        w_spec = pl.BlockSpec((tk, tn), lambda i, j, k: (k, cb + j))
    else:
        w_spec = pl.BlockSpec((None, tk, tn), lambda i, j, k: (layer, k, cb + j))
    osz = jnp.dtype(out_dtype).itemsize
    in_specs = [pl.BlockSpec((tm, tk), lambda i, j, k: (i, k))]
    out_specs = [pl.BlockSpec((tm, tn), lambda i, j, k: (i, j))]
    out_shape = [jax.ShapeDtypeStruct((m, n), out_dtype)]
    operands = [x]
    mr = 0
    if rider is not None:
        mr = rider.shape[0]
        spare = min(1, m // tm - 1)
        in_specs.append(pl.BlockSpec((mr, tk), lambda i, j, k: (0, k)))
        out_specs.append(pl.BlockSpec((None, mr, tn), lambda i, j, k: (jnp.minimum(i, spare), 0, j)))
        out_shape.append(jax.ShapeDtypeStruct((1 + spare, mr, n), out_dtype))
        operands.append(rider)
    vmem = 2 * ((tm + mr) * tk * 2 + tk * tn * 4 + (tm + mr) * tn * osz) + tk * tn * 2 + 2 * tm * tn * 4
    outs = pl.pallas_call(
        functools.partial(_mm_kernel, nk=nk, act=act, rider=rider is not None),
        grid=(m // tm, n // tn, nk),
        in_specs=in_specs + [w_spec],
        out_specs=out_specs,
        out_shape=out_shape,
        compiler_params=_params(("parallel", "parallel", "arbitrary"), vmem),
        name="matmul",
    )(*operands, w)
    return outs[0] if rider is None else (outs[0], outs[1][0])


def _matmul_groups(x, w, **kw):
    out, out_r = _matmul(x["prompt"], w, rider=x["sample"], **kw)
    return {"prompt": out, "sample": out_r}


def _mem_attn_kernel(q_ref, k_ref, v_ref, o_ref, *, heads):
    dh = q_ref.shape[-1] // heads
    scale = dh ** -0.5
    cols = [slice(hh * dh, (hh + 1) * dh) for hh in range(heads)]
    of = lambda ref: [ref[:, sl] for sl in cols]
    s = _each(lambda q, k: _dot_nt(q, k) * scale, of(q_ref), of(k_ref))
    p = _each(lambda s_: jnp.exp(s_ - jnp.max(s_, axis=-1, keepdims=True)), s)
    o = _each(lambda p_, v: _dot(p_, v) / jnp.sum(p_, axis=-1, keepdims=True), p, of(v_ref))
    for sl, o_ in zip(cols, o):
        o_ref[:, sl] = o_.astype(o_ref.dtype)


def _mem_attn(tail, k_arr, v_arr, kv_index, batch, t, heads, mem_width, mt):
    tt = _tile(t, 512)
    nt = t // tt
    lead = (None,) * (k_arr.ndim - 2)
    k_spec = pl.BlockSpec(lead + (mt, mem_width), lambda b, i: kv_index(b, 0))
    v_spec = pl.BlockSpec(lead + (mt, mem_width), lambda b, i: kv_index(b, 1))
    return pl.pallas_call(
        functools.partial(_mem_attn_kernel, heads=heads),
        grid=(batch, nt),
        in_specs=[pl.BlockSpec((tt, mem_width), lambda b, i: (b * nt + i, 0)), k_spec, v_spec],
        out_specs=pl.BlockSpec((tt, mem_width), lambda b, i: (b * nt + i, 0)),
        out_shape=jax.ShapeDtypeStruct((batch * t, mem_width), BF16),
        compiler_params=_params(("parallel", "parallel"), 2 * (tt * mem_width * 6 + 2 * mt * mem_width * 4) + 4 * tt * mt * 4),
        name="mem_attn",
    )(tail, k_arr, v_arr)


def _logf_kernel(f_ref, b_ref, o_ref):
    o_ref[...] = _log_sigmoid(f_ref[...] + b_ref[...])


def _fox_logf(tail, gate_block, bias_pad):
    m = tail.shape[0]
    tr = _tile(m, 1024)
    return pl.pallas_call(
        _logf_kernel,
        grid=(m // tr,),
        in_specs=[pl.BlockSpec((tr, LANES), lambda i: (i, gate_block)), pl.BlockSpec((1, LANES), lambda i: (0, 0))],
        out_specs=pl.BlockSpec((tr, LANES), lambda i: (i, 0)),
        out_shape=jax.ShapeDtypeStruct((m, LANES), F32),
        compiler_params=_params(("parallel",), 4 * tr * LANES * 4),
        name="fox_logf",
    )(tail, bias_pad)


def _cumsum_kernel(x_ref, o_ref, carry_ref, *, carry):
    tr = x_ref.shape[0]
    r = lax.broadcasted_iota(jnp.int32, (tr, tr), 0)
    c = lax.broadcasted_iota(jnp.int32, (tr, tr), 1)
    lower = (r >= c).astype(F32)
    cs = jnp.dot(lower, x_ref[...], precision=HIGHEST, preferred_element_type=F32)
    if carry:
        @pl.when(pl.program_id(1) == 0)
        def _():
            carry_ref[...] = jnp.zeros_like(carry_ref)

        cs = cs + carry_ref[...]
        carry_ref[...] = cs[tr - 1:tr, :]
    o_ref[...] = cs


def _cumsum_rows(x, batch, t, *, carry):
    tr = LANES
    nt = t // tr
    return pl.pallas_call(
        functools.partial(_cumsum_kernel, carry=carry),
        grid=(batch, nt),
        in_specs=[pl.BlockSpec((tr, LANES), lambda b, i: (b * nt + i, 0))],
        out_specs=pl.BlockSpec((tr, LANES), lambda b, i: (b * nt + i, 0)),
        out_shape=jax.ShapeDtypeStruct(x.shape, F32),
        scratch_shapes=[pltpu.VMEM((1, LANES), F32)],
        compiler_params=_params(("parallel", "arbitrary"), 8 * tr * LANES * 4),
        name="cumsum_rows",
    )(x)


def _fox_prompt_kernel(q_ref, k_ref, v_ref, crow_ref, o_ref, kb_ref, vb_ref, *, n_q):
    qi = pl.program_id(2)
    tq = q_ref.shape[0]
    scale = HEAD_DIM ** -0.5

    @pl.when(qi == 0)
    def _():
        kb_ref[...] = k_ref[...].astype(BF16)
        vb_ref[...] = v_ref[...].astype(BF16)

    def attend(extent):
        s = _dot_nt(q_ref[...], kb_ref[0:extent, :]) * scale - crow_ref[:, 0:extent]
        row = lax.broadcasted_iota(jnp.int32, (tq, extent), 0) + (extent - tq)
        col = lax.broadcasted_iota(jnp.int32, (tq, extent), 1)
        s = jnp.where(col <= row, s, -jnp.inf)
        p = jnp.exp(s - jnp.max(s, axis=-1, keepdims=True))
        o = _dot(p, vb_ref[0:extent, :]) / jnp.sum(p, axis=-1, keepdims=True)
        o_ref[...] = o.astype(o_ref.dtype)

    for tile in range(n_q):
        pl.when(qi == tile)(functools.partial(attend, (tile + 1) * tq))


def _fox_prompt(q, k, v, c_row, batch, t, heads):
    tq = _tile(t, 256)
    nq = t // tq
    blk = pl.BlockSpec((tq, HEAD_DIM), lambda b, h, i: (b * nq + i, h))
    full = pl.BlockSpec((t, HEAD_DIM), lambda b, h, i: (b, h))
    return pl.pallas_call(
        functools.partial(_fox_prompt_kernel, n_q=nq),
        grid=(batch, heads, nq),
        in_specs=[blk, full, full, pl.BlockSpec((None, None, 1, t), lambda b, h, i: (b, h, 0, 0))],
        out_specs=blk,
        out_shape=jax.ShapeDtypeStruct((batch * t, heads * HEAD_DIM), BF16),
        scratch_shapes=[pltpu.VMEM((t, HEAD_DIM), BF16), pltpu.VMEM((t, HEAD_DIM), BF16)],
        compiler_params=_params(("parallel", "parallel", "arbitrary"), 5 * t * HEAD_DIM * 4 + 6 * tq * t * 4),
        name="fox_prompt",
    )(q, k, v, c_row)


def _fox_decode_kernel(pt_ref, q_ref, kn_ref, vn_ref, lfn_ref, kp_ref, vp_ref, lfp_ref, o_ref,
                       m_ref, l_ref, acc_ref, carry_ref, *, n_pages, heads, t_new, page):
    del pt_ref
    p = pl.program_id(1)
    scale = HEAD_DIM ** -0.5
    r = lax.broadcasted_iota(jnp.int32, (page, page), 0)
    c = lax.broadcasted_iota(jnp.int32, (page, page), 1)
    upper = (r <= c).astype(F32)

    @pl.when(p == 0)
    def _():
        m_ref[...] = jnp.full_like(m_ref, -jnp.inf)
        l_ref[...] = jnp.zeros_like(l_ref)
        acc_ref[...] = jnp.zeros_like(acc_ref)
        carry_ref[...] = jnp.zeros_like(carry_ref)

    head_cols = lambda h: slice(h * HEAD_DIM, (h + 1) * HEAD_DIM)

    def attend(k_heads, v_heads, lf_t, mask):
        c_t = jnp.dot(lf_t, upper, precision=HIGHEST, preferred_element_type=F32) + carry_ref[...]
        carry_ref[...] = c_t[:, page - 1:page]
        s = jnp.stack([_dot_nt(q_ref[:, head_cols(h)], k_heads[h]) for h in range(heads)])
        s = s * scale - c_t[:, None, :]
        if mask is not None:
            s = jnp.where(mask[None], s, -jnp.inf)
        m_old = m_ref[...]
        m_new = jnp.maximum(m_old, jnp.max(s, axis=-1, keepdims=True))
        alpha = jnp.exp(m_old - m_new)
        pr = jnp.exp(s - m_new)
        l_ref[...] = alpha * l_ref[...] + jnp.sum(pr, axis=-1, keepdims=True)
        pv = jnp.stack([_dot(pr[h], v_heads[h]) for h in range(heads)])
        acc_ref[...] = alpha * acc_ref[...] + pv
        m_ref[...] = m_new

    @pl.when(p < n_pages)
    def _():
        attend([kp_ref[pl.ds(h, page, stride=heads), :] for h in range(heads)],
               [vp_ref[pl.ds(h, page, stride=heads), :] for h in range(heads)], lfp_ref[...], None)

    @pl.when(p == n_pages)
    def _():
        pad = jnp.zeros((page - t_new, HEAD_DIM), F32)
        qrow = lax.broadcasted_iota(jnp.int32, (t_new, page), 0)
        kcol = lax.broadcasted_iota(jnp.int32, (t_new, page), 1)
        attend([jnp.concatenate([kn_ref[:, head_cols(h)], pad], axis=0) for h in range(heads)],
               [jnp.concatenate([vn_ref[:, head_cols(h)], pad], axis=0) for h in range(heads)],
               lfn_ref[...], kcol <= qrow)
        out = acc_ref[...] / l_ref[...]
        for h in range(heads):
            o_ref[:, head_cols(h)] = out[h].astype(o_ref.dtype)


def _fox_decode(q, k, v, lfn_t, k_pool, v_pool, lf_pool_t, page_table, layer, batch, t_new, heads):
    n_pages = page_table.shape[1]
    page = k_pool.shape[2] // heads
    width = heads * HEAD_DIM
    assert page == LANES and t_new <= page
    last = n_pages - 1
    new = pl.BlockSpec((t_new, width), lambda b, p, pt: (b, 0))
    pool =pl.BlockSpec((None, None, page * heads, HEAD_DIM),
                        lambda b, p, pt: (layer, pt[b, jnp.minimum(p, last)], 0, 0))
    lf_pool = pl.BlockSpec((None, None, heads, page), lambda b, p, pt: (layer, pt[b, jnp.minimum(p, last)], 0, 0))
    grid_spec = pltpu.PrefetchScalarGridSpec(
        num_scalar_prefetch=1,
        grid=(batch, n_pages + 1),
        in_specs=[new, new, new, pl.BlockSpec((None, heads, page), lambda b, p, pt: (b, 0, 0)),
                  pool, pool, lf_pool],
        out_specs=new,
        scratch_shapes=[pltpu.VMEM((heads, t_new, 1), F32), pltpu.VMEM((heads, t_new, 1), F32),
                        pltpu.VMEM((heads, t_new, HEAD_DIM), F32), pltpu.VMEM((heads, 1), F32)],
    )
    return pl.pallas_call(
        functools.partial(_fox_decode_kernel, n_pages=n_pages, heads=heads, t_new=t_new, page=page),
        grid_spec=grid_spec,
        out_shape=jax.ShapeDtypeStruct((batch * t_new, width), BF16),
        compiler_params=_params(("parallel", "arbitrary"), 4 * page * width * 4 + 8 * t_new * width * 4 + (4 << 20)),
        name="fox_decode",
    )(page_table, q, k, v, lfn_t, k_pool, v_pool, lf_pool_t)


def _shortconv_kernel(gb_ref, gc_ref, h_ref, buf_ref, w_ref, o_ref, tail_ref, us_ref, *, width):
    tt = gb_ref.shape[0]

    @pl.when(pl.program_id(2) == 0)
    def _():
        us_ref[0:SUBLANES, :] = buf_ref[...]

    us_ref[SUBLANES:SUBLANES + tt, :] = gc_ref[...] * h_ref[...]
    conv = None
    for k in range(width):
        off = SUBLANES - (width - 1) + k
        term = w_ref[k:k + 1, :] * us_ref[off:off + tt, :]
        conv = term if conv is None else conv + term
    o_ref[...] = (gb_ref[...] * conv).astype(o_ref.dtype)
    last = us_ref[tt:tt + SUBLANES, :]
    tail_ref[...] = last
    us_ref[0:SUBLANES, :] = last


def _shortconv(proj, buf8, w8, batch, t, mix_width, width):
    tt = _tile(t, 512)
    tc = _tile(mix_width, 512)
    nt, nc = t // tt, mix_width // tc
    col = lambda off: pl.BlockSpec((tt, tc), lambda b, c, i: (b * nt + i, off * nc + c))
    return pl.pallas_call(
        functools.partial(_shortconv_kernel, width=width),
        grid=(batch, nc, nt),
        in_specs=[col(0), col(1), col(2),
                  pl.BlockSpec((None, SUBLANES, tc), lambda b, c, i: (b, 0, c)),
                  pl.BlockSpec((SUBLANES, tc), lambda b, c, i: (0, c))],
        out_specs=[col(0), pl.BlockSpec((None, SUBLANES, tc), lambda b, c, i: (b, 0, c))],
        out_shape=[jax.ShapeDtypeStruct((batch * t, mix_width), BF16),
                   jax.ShapeDtypeStruct((batch, SUBLANES, mix_width), F32)],
        scratch_shapes=[pltpu.VMEM((tt + SUBLANES, tc), F32)],
        compiler_params=_params(("parallel", "parallel", "arbitrary"), 12 * tt * tc * 4),
        name="shortconv",
    )(proj, proj, proj, buf8, w8)


def _gdn_conv_kernel(x_ref, buf_ref, w_ref, o_ref, xs_ref, *, width, l2norm, scale):
    tt, tc = x_ref.shape

    @pl.when(pl.program_id(2) == 0)
    def _():
        xs_ref[0:SUBLANES, :] = buf_ref[...]

    xs_ref[SUBLANES:SUBLANES + tt, :] = x_ref[...]
    conv = None
    for k in range(width):
        off = SUBLANES - (width - 1) + k
        term = w_ref[k:k + 1, :] * xs_ref[off:off + tt, :]
        conv = term if conv is None else conv + term
    y = _silu(conv)
    if l2norm:
        for g in range(tc // HEAD_DIM):
            sl = slice(g * HEAD_DIM, (g + 1) * HEAD_DIM)
            seg = y[:, sl]
            seg = seg * lax.rsqrt(jnp.sum(seg * seg, axis=-1, keepdims=True) + L2_EPS)
            o_ref[:, sl] = seg * scale if scale != 1.0 else seg
    else:
        o_ref[...] = y
    xs_ref[0:SUBLANES, :] = xs_ref[tt:tt + SUBLANES, :]


def _gdn_conv(proj, buf8, w8, part, batch, t, mix_width, width, *, l2norm, scale=1.0):
    tt = _tile(t, 512)
    tc = _tile(mix_width, 512)
    nt, nc = t // tt, mix_width // tc
    return pl.pallas_call(
        functools.partial(_gdn_conv_kernel, width=width, l2norm=l2norm, scale=scale),
        grid=(batch, nc, nt),
        in_specs=[pl.BlockSpec((tt, tc), lambda b, c, i: (b * nt + i, part * nc + c)),
                  pl.BlockSpec((None, SUBLANES, tc), lambda b, c, i: (b, 0, part * nc + c)),
                  pl.BlockSpec((SUBLANES, tc), lambda b, c, i: (0, part * nc + c))],
        out_specs=pl.BlockSpec((tt, tc), lambda b, c, i: (b * nt + i, c)),
        out_shape=jax.ShapeDtypeStruct((batch * t, mix_width), F32),
        scratch_shapes=[pltpu.VMEM((tt + SUBLANES, tc), F32)],
        compiler_params=_params(("parallel", "parallel", "arbitrary"), 10 * tt * tc * 4),
        name="gdn_conv",
    )(proj, buf8, w8)


def _gdn_gate_kernel(a_ref, bt_ref, alog_ref, dt_ref, g_ref, beta_ref):
    g_ref[...] = -jnp.exp(alog_ref[...]) * _softplus(a_ref[...] + dt_ref[...])
    beta_ref[...] = jax.nn.sigmoid(bt_ref[...])


def _gdn_gates(tail, a_block, bt_block, a_log_pad, dt_pad):
    m = tail.shape[0]
    tr = _tile(m, 1024)
    blk = lambda off: pl.BlockSpec((tr, LANES), lambda i: (i, off))
    vec = pl.BlockSpec((1, LANES), lambda i: (0, 0))
    return pl.pallas_call(
        _gdn_gate_kernel,
        grid=(m // tr,),
        in_specs=[blk(a_block), blk(bt_block), vec, vec],
        out_specs=[blk(0), blk(0)],
        out_shape=[jax.ShapeDtypeStruct((m, LANES), F32)] * 2,
        compiler_params=_params(("parallel",), 8 * tr * LANES * 4),
        name="gdn_gates",
    )(tail, tail, a_log_pad, dt_pad)


def _unit_lower_inverse_offdiag(mats, ri, ci):
    n = mats[0].shape[0]
    blk = lambda idx, size: jnp.right_shift(idx, size.bit_length() - 1)
    base = SUBLANES
    in_base = blk(ri, base) == blk(ci, base)
    a0 = _each(lambda a: jnp.where(in_base, a, 0.0), mats)
    low = _each(lambda a: -a, a0)
    pw = _each(_dot, a0, a0)
    low = _each(lambda l, p, lp: l + p + lp, low, pw, _each(_dot, low, pw))
    pw = _each(_dot, pw, pw)
    low = _each(lambda l, p, lp: l + p + lp, low, pw, _each(_dot, low, pw))
    s = base
    while s < n:
        sub = (blk(ri, 2 * s) == blk(ci, 2 * s)) & (blk(ri, s) != blk(ci, s))
        off = _each(lambda a: jnp.where(sub, a, 0.0), mats)
        x = _each(lambda o, lo: o + lo, off, _each(_dot, low, off))
        low = _each(lambda l, x_, xl: l - (x_ + xl), low, x, _each(_dot, x, low))
        s *= 2
    return low


def _gdn_intra_kernel(q_ref, k_ref, v_ref, gc_ref, beta_ref, u_ref, w_ref, qk_ref, qg_ref, kg_ref, *, group):
    hg = pl.program_id(1)
    n = q_ref.shape[0]
    ri = lax.broadcasted_iota(jnp.int32, (n, n), 0)
    ci = lax.broadcasted_iota(jnp.int32, (n, n), 1)
    incl = ri >= ci
    cols = [slice(i * HEAD_DIM, (i + 1) * HEAD_DIM) for i in range(group)]
    q = [q_ref[:, sl] for sl in cols]
    k = [k_ref[:, sl] for sl in cols]
    v = [v_ref[:, sl] for sl in cols]
    gcol = [_pick_lane(gc_ref[...], hg * group + i) for i in range(group)]
    bcol = [_pick_lane(beta_ref[...], hg * group + i) for i in range(group)]

    def decay_of(g):
        gmat = jnp.broadcast_to(g, (n, n))
        return jnp.where(incl, jnp.exp(jnp.where(incl, gmat - gmat.T, 0.0)), 0.0)

    decay = _each(decay_of, gcol)
    kb = _each(lambda k_, b: k_ * b, k, bcol)
    a = _each(lambda kk, d: jnp.where(ri > ci, kk * d, 0.0), _each(_dot_nt, kb, k), decay)
    low = _unit_lower_inverse_offdiag(a, ri, ci)
    eg = _each(jnp.exp, gcol)
    rhs = _each(lambda v_, b, kb_, e: jnp.concatenate([v_ * b, kb_ * e], axis=1), v, bcol, kb, eg)
    uw = _each(lambda r, lr: r + lr, rhs, _each(_dot, low, rhs))
    qk = _each(lambda x, d: x * d, _each(_dot_nt, q, k), decay)
    for i, sl in enumerate(cols):
        u_ref[:, sl] = uw[i][:, :HEAD_DIM]
        w_ref[:, sl] = uw[i][:, HEAD_DIM:].astype(w_ref.dtype)
        qk_ref[:, sl] = qk[i].astype(qk_ref.dtype)
        qg_ref[:, sl] = (q[i] * eg[i]).astype(qg_ref.dtype)
        kg_ref[:, sl] = (k[i] * jnp.exp(gcol[i][n - 1:n, :] - gcol[i])).astype(kg_ref.dtype)


def _gdn_intra(qn, kn, vc, gc, beta, batch, t, heads, group):
    n = GDN_CHUNK
    nc = t // n
    wide = pl.BlockSpec((n, group * HEAD_DIM), lambda b, g, c: (b * nc + c, g))
    gate = pl.BlockSpec((n, LANES), lambda b, g, c: (b * nc + c, 0))
    shape = lambda dt: jax.ShapeDtypeStruct((batch * t, heads * HEAD_DIM), dt)
    return pl.pallas_call(
        functools.partial(_gdn_intra_kernel, group=group),
        grid=(batch, heads // group, nc),
        in_specs=[wide, wide, wide, gate, gate],
        out_specs=[wide] * 5,
        out_shape=[shape(F32)] + [shape(BF16)] * 4,
        compiler_params=_params(("parallel", "parallel", "parallel"), 2 * 8 * n * group * HEAD_DIM * 4 + (8 << 20)),
        name="gdn_intra",
    )(qn, kn, vc, gc, beta)


def _gdn_state_kernel(u_ref, w_ref, qk_ref, qg_ref, kg_ref, gc_ref, z_ref, gout_ref, s0_ref, o_ref, sout_ref,
                      s_ref, *, group, n_chunks):
    hg = pl.program_id(1)
    c = pl.program_id(2)
    n = u_ref.shape[0]

    @pl.when(c == 0)
    def _():
        s_ref[...] = s0_ref[...]

    cols = [slice(i * HEAD_DIM, (i + 1) * HEAD_DIM) for i in range(group)]
    of = lambda ref: [ref[:, sl] for sl in cols]
    s = [s_ref[i] for i in range(group)]
    v_new = _each(lambda u, ws: u - ws, of(u_ref), _each(_dot, of(w_ref), s))
    o = _each(lambda x, y: x + y, _each(_dot, of(qg_ref), s), _each(_dot, of(qk_ref), v_new))
    upd = _each(_dot_tn, of(kg_ref), v_new)
    for i, sl in enumerate(cols):
        g_last = _pick_lane(gc_ref[n - 1:n, :], hg * group + i)
        s_ref[i] = s[i] * jnp.exp(g_last) + upd[i]
        on = o[i] * lax.rsqrt(jnp.mean(o[i] * o[i], axis=-1, keepdims=True) + RMS_EPS) * gout_ref[...]
        o_ref[:, sl] = (on * _silu(z_ref[:, sl])).astype(o_ref.dtype)

    @pl.when(c == n_chunks - 1)
    def _():
        sout_ref[...] = s_ref[...]


def _gdn_state(u, w, qk, qg, kg, gc, z_arr, z_block0, g_out, s0, batch, t, heads, group):
    n = GDN_CHUNK
    nc = t // n
    gw = group * HEAD_DIM
    wide = pl.BlockSpec((n, gw), lambda b, g, c: (b * nc + c, g))
    state = pl.BlockSpec((None, group, HEAD_DIM, HEAD_DIM), lambda b, g, c: (b, g, 0, 0))
    return pl.pallas_call(
        functools.partial(_gdn_state_kernel, group=group, n_chunks=nc),
        grid=(batch, heads // group, nc),
        in_specs=[wide] * 5 + [pl.BlockSpec((n, LANES), lambda b, g, c: (b * nc + c, 0)),
                               pl.BlockSpec((n, gw), lambda b, g, c: (b * nc + c, z_block0 + g)),
                               pl.BlockSpec((1, HEAD_DIM), lambda b, g, c: (0, 0)), state],
        out_specs=[wide, state],
        out_shape=[jax.ShapeDtypeStruct((batch * t, heads * HEAD_DIM), BF16),
                   jax.ShapeDtypeStruct((batch, heads, HEAD_DIM, HEAD_DIM), F32)],
        scratch_shapes=[pltpu.VMEM((group, HEAD_DIM, HEAD_DIM), F32)],
        compiler_params=_params(("parallel", "parallel", "arbitrary"), 2 * 8 * n * gw * 4 + 6 * group * HEAD_DIM * HEAD_DIM * 4 + (4 << 20)),
        name="gdn_state",
    )(u, w, qk, qg, kg, gc, z_arr, g_out.reshape(1, HEAD_DIM), s0)


def _pad_lanes(v, width=LANES):
    return jnp.pad(v, [(0, 0)] * (v.ndim - 1) + [(0, width - v.shape[-1])])


def _pad_rows8(a, rows_axis):
    pad = [(0, 0)] * a.ndim
    pad[rows_axis] = (SUBLANES - a.shape[rows_axis], 0)
    return jnp.pad(a, pad)


def _tail_weight(w_stack, j, main, parts, mem_width):
    total = w_stack.shape[2]
    cols = [w_stack[j, :, total - mem_width:]]
    off = main
    for width in parts:
        cols.append(_pad_lanes(w_stack[j, :, off:off + width]))
        off += width
    return jnp.concatenate(cols, axis=1)


def _tail_proj(x, w_tail):
    tn_tail = 256 if w_tail.shape[1] % 256 == 0 else LANES
    return _matmul_groups(x, w_tail, tn=tn_tail)


def _fox_mixer(x, dims, w_in_fox, b_fox_f, j, caches, page_table, heads, mem_width):
    mix_width = heads * HEAD_DIM
    main = 3 * mix_width
    w_tail = _tail_weight(w_in_fox, j, main, [heads], mem_width)
    gate_block = mem_width // LANES
    bias = _pad_lanes(b_fox_f[j].reshape(1, heads))
    out = {}
    qkv = [_matmul_groups(x, w_in_fox, layer=j, col0=part * mix_width, n=mix_width) for part in range(3)]
    tails = _tail_proj(x, w_tail)
    for name, (batch, t) in dims.items():
        q, k, v = (part[name] for part in qkv)
        tail = tails[name]
        logf = _fox_logf(tail, gate_block, bias)
        if name == "prompt":
            c_tok = _cumsum_rows(logf, batch, t, carry=True)
            c_row = jnp.transpose(c_tok.reshape(batch, t, LANES)[:, :, :heads], (0, 2, 1))[:, :, None, :]
            mix = _fox_prompt(q, k, v, c_row, batch, t, heads)
        else:
            k_pool, v_pool, lf_pool_t = caches
            lfn_t = jnp.transpose(logf.reshape(batch, t, LANES)[:, :, :heads], (0, 2, 1))
            lfn_t = _pad_lanes(lfn_t, lf_pool_t.shape[-1])
            mix = _fox_decode(q, k, v, lfn_t, k_pool, v_pool, lf_pool_t, page_table, j, batch, t, heads)
        shp = (batch, t, heads, HEAD_DIM)
        out[name] = dict(mix=mix, tail=tail, k=k.reshape(shp), v=v.reshape(shp),
                         logf=logf[:, :heads].reshape(batch, t, heads))
    return out


def _conv_mixer(x, dims, w_in_conv, w_conv, j, state_conv, heads, mem_width):
    mix_width = heads * HEAD_DIM
    main = 3 * mix_width
    width = w_conv.shape[1]
    w8 = jnp.pad(w_conv[j], ((0, SUBLANES - width), (0, 0)))
    out = {}
    projs = _matmul_groups(x, w_in_conv, layer=j, col0=0, n=main)
    tails = _tail_proj(x, w_in_conv[j, :, main:])
    for name, (batch, t) in dims.items():
        proj, tail = projs[name], tails[name]
        if name == "prompt":
            buf8 = jnp.zeros((batch, SUBLANES, mix_width), F32)
        else:
            buf8 = _pad_rows8(state_conv[j], 1)
        mix, last8 = _shortconv(proj, buf8, w8, batch, t, mix_width, width)
        out[name] = dict(mix=mix, tail=tail, buf=last8[:, SUBLANES - (width - 1):, :])
    return out


def _gdn_mixer(x, dims, w_in_gdn, w_gdn_conv, gdn_a_log, gdn_dt_bias, g_gdn_out, j, state_gdn_conv, state_gdn_s,
               heads, mem_width):
    mix_width = heads * HEAD_DIM
    main = 4 * mix_width
    width = w_gdn_conv.shape[1]
    w_tail = _tail_weight(w_in_gdn, j, main, [heads, heads], mem_width)
    a_block = mem_width // LANES
    w8 = jnp.pad(w_gdn_conv[j], ((0, SUBLANES - width), (0, 0)))
    a_log = _pad_lanes(gdn_a_log[j].reshape(1, heads))
    dt_bias = _pad_lanes(gdn_dt_bias[j].reshape(1, heads))
    group = next(g for g in (8, 4, 2, 1) if heads % g == 0)
    out = {}
    projs = _matmul_groups(x, w_in_gdn, layer=j, col0=0, n=main)
    tails = _tail_proj(x, w_tail)
    for name, (batch, t) in dims.items():
        proj, tail = projs[name], tails[name]
        if name == "prompt":
            buf = jnp.zeros((batch, width - 1, 3 * mix_width), F32)
            s0 = jnp.zeros((batch, heads, HEAD_DIM, HEAD_DIM), F32)
        else:
            buf, s0 = state_gdn_conv[j], state_gdn_s[j]
        buf8 = _pad_rows8(buf, 1)
        conv = functools.partial(_gdn_conv, proj, buf8, w8, batch=batch, t=t, mix_width=mix_width, width=width)
        qn = conv(part=0, l2norm=True, scale=HEAD_DIM ** -0.5)
        kn = conv(part=1, l2norm=True)
        vc = conv(part=2, l2norm=False)
        g, beta = _gdn_gates(tail, a_block, a_block + 1, a_log, dt_bias)
        z_arr, z_block0 = proj, 3 * mix_width // (group * HEAD_DIM)
        tp = -(-t // GDN_CHUNK) * GDN_CHUNK
        if tp != t:
            pad = lambda a_: jnp.pad(a_.reshape(batch, t, -1), ((0, 0), (0, tp - t), (0, 0))).reshape(batch * tp, -1)
            qn, kn, vc, g, beta = (pad(a_) for a_ in (qn, kn, vc, g, beta))
            z_arr, z_block0 = pad(proj[:, 3 * mix_width:main]), 0
        gc = _cumsum_rows(g, batch, tp, carry=False)
        u, w, qk, qg, kg = _gdn_intra(qn, kn, vc, gc, beta, batch, tp, heads, group)
        mix, s_new = _gdn_state(u, w, qk, qg, kg, gc, z_arr, z_block0, g_gdn_out[j], s0, batch, tp, heads, group)
        if tp != t:
            mix = mix.reshape(batch, tp, mix_width)[:, :t].reshape(batch * t, mix_width)
        keep = min(t, width - 1)
        newest = proj.reshape(batch, t, -1)[:, t - keep:, :3 * mix_width]
        out[name] = dict(mix=mix, tail=tail, conv=jnp.concatenate([buf, newest], axis=1)[:, keep:], s=s_new)
    return out


def kernel(x_prompt, x_sample, cache_fox_k, cache_fox_v, cache_fox_logf, cache_mem_k, cache_mem_v, state_conv, state_gdn_conv, state_gdn_s, page_table, mem_prompt, g_mix_pre, g_mix_post, g_mlp_pre, g_mlp_post, g_mem, w_mem_kv, w_out, w_mlp_up, w_mlp_down, w_in_fox, b_fox_f, w_in_conv, w_conv, w_in_gdn, w_gdn_conv, gdn_a_log, gdn_dt_bias, g_gdn_out):
    bp, tp_, d = x_prompt.shape
    bs, ts, _ = x_sample.shape
    depth = g_mix_pre.shape[0]
    mem_tokens, mem_heads, mem_hd = cache_mem_k.shape[2:]
    mem_width = mem_heads * mem_hd
    mix_width = d - mem_width
    heads = mix_width // HEAD_DIM
    n_fox, n_pool, page = cache_fox_k.shape[:3]

    y = {"prompt": x_prompt.reshape(bp * tp_, d), "sample": x_sample.reshape(bs * ts, d)}
    dims = {"prompt": (bp, tp_), "sample": (bs, ts)}
    x = {n: _norm_cast(y[n], g_mix_pre[0]) for n in y}
    mem2d = mem_prompt.reshape(bp * mem_tokens, d)
    caches = (cache_fox_k.reshape(n_fox, n_pool, page * heads, HEAD_DIM),
              cache_fox_v.reshape(n_fox, n_pool, page * heads, HEAD_DIM),
              jnp.transpose(cache_fox_logf, (0, 1, 3, 2)))
    mem_k_s = cache_mem_k.reshape(depth, bs, mem_tokens, mem_width)
    mem_v_s = cache_mem_v.reshape(depth, bs, mem_tokens, mem_width)

    res = {k_: [] for k_ in ("fox_k_p", "fox_v_p", "fox_lf_p", "fox_k_s", "fox_v_s", "fox_lf_s", "conv_p", "conv_s",
                             "gconv_p", "gconv_s", "gs_p", "gs_s", "mem_k", "mem_v")}
    for i in range(depth):
        kind, j = i % 3, i // 3
        kv = _matmul(_norm_cast(mem2d, g_mem[i]), w_mem_kv, layer=i)
        res["mem_k"].append(kv[:, :mem_width].reshape(bp, mem_tokens, mem_heads, mem_hd))
        res["mem_v"].append(kv[:, mem_width:].reshape(bp, mem_tokens, mem_heads, mem_hd))
        if kind == 0:
            mixed = _fox_mixer(x, dims, w_in_fox, b_fox_f, j, caches, page_table, heads, mem_width)
            for n, s in (("prompt", "p"), ("sample", "s")):
                res["fox_k_" + s].append(mixed[n]["k"])
                res["fox_v_" + s].append(mixed[n]["v"])
                res["fox_lf_" + s].append(mixed[n]["logf"])
        elif kind == 1:
            mixed = _conv_mixer(x, dims, w_in_conv, w_conv, j, state_conv, heads, mem_width)
            res["conv_p"].append(mixed["prompt"]["buf"])
            res["conv_s"].append(mixed["sample"]["buf"])
        else:
            mixed = _gdn_mixer(x, dims, w_in_gdn, w_gdn_conv, gdn_a_log, gdn_dt_bias, g_gdn_out, j, state_gdn_conv,
                               state_gdn_s, heads, mem_width)
            for n, s in (("prompt", "p"), ("sample", "s")):
                res["gconv_" + s].append(mixed[n]["conv"])
                res["gs_" + s].append(mixed[n]["s"])
        g_next = g_mix_pre[i + 1] if i + 1 < depth else g_mlp_pre[i]
        cat = {}
        for n, (b, t) in dims.items():
            if n == "prompt":
                mem = _mem_attn(mixed[n]["tail"], kv, kv, lambda bb, part: (bb, part), b, t, mem_heads, mem_width,
                                mem_tokens)
            else:
                mem = _mem_attn(mixed[n]["tail"], mem_k_s, mem_v_s, lambda bb, part: (i, bb, 0, 0), b, t, mem_heads,
                                mem_width, mem_tokens)
            cat[n] = jnp.concatenate([mixed[n]["mix"], mem], axis=1)
        delta = _matmul_groups(cat, w_out, layer=i)
        x2 = {}
        for n in dims:
            y[n], x2[n] = _resid_norm(delta[n], y[n], g_mix_post[i], g_mlp_pre[i])
        hid = _matmul_groups(x2, w_mlp_up, layer=i, out_dtype=BF16, act="relu2")
        delta = _matmul_groups(hid, w_mlp_down, layer=i)
        for n in dims:
            y[n], x[n] = _resid_norm(delta[n], y[n], g_mlp_post[i], g_next)

    st = lambda k_: jnp.stack(res[k_])
    return (y["prompt"].reshape(bp, tp_, d), y["sample"].reshape(bs, ts, d),
            st("fox_k_p"), st("fox_v_p"), st("fox_lf_p"), st("conv_p"), st("gconv_p"), st("gs_p"),
            st("mem_k"), st("mem_v"),
            st("fox_k_s"), st("fox_v_s"), st("fox_lf_s"), st("conv_s"), st("gconv_s"), st("gs_s"))
```

```python
import functools

import jax
import jax.numpy as jnp
from jax import lax
from jax.experimental import pallas as pl
from jax.experimental.pallas import tpu as pltpu

F32 = jnp.float32
BF16 = jnp.bfloat16
RMS_EPS = 1e-6
L2_EPS = 1e-6
HEAD_DIM = 128
LANES = 128
SUBLANES = 8
GDN_CHUNK = 128
V7X_VMEM_BYTES = 64 * 1024 * 1024
VMEM_CAP = V7X_VMEM_BYTES - 6 * 1024 * 1024
HIGHEST = lax.Precision.HIGHEST
NT_DIMS = (((1,), (1,)), ((), ()))
TN_DIMS = (((0,), (0,)), ((), ()))


def _params(semantics, vmem_bytes):
    limit = int(min(VMEM_CAP, max(16 * 1024 * 1024, vmem_bytes * 5 // 4 + (2 << 20))))
    return pltpu.CompilerParams(dimension_semantics=semantics, vmem_limit_bytes=limit)


def _tile(dim, pref, align=LANES):
    if dim <= pref:
        return dim
    for cand in range(pref - pref % align, 0, -align):
        if dim % cand == 0:
            return cand
    raise ValueError((dim, pref, align))


def _dot(a, b):
    return jnp.dot(a.astype(BF16), b.astype(BF16), preferred_element_type=F32)


def _dot_nt(a, b):
    return lax.dot_general(a.astype(BF16), b.astype(BF16), NT_DIMS, preferred_element_type=F32)


def _dot_tn(a, b):
    return lax.dot_general(a.astype(BF16), b.astype(BF16), TN_DIMS, preferred_element_type=F32)


def _each(fn, *columns):
    return [fn(*args) for args in zip(*columns)]


def _log_sigmoid(x):
    return jnp.minimum(x, 0.0) - jnp.log1p(jnp.exp(-jnp.abs(x)))


def _softplus(x):
    return jnp.maximum(x, 0.0) + jnp.log1p(jnp.exp(-jnp.abs(x)))


def _silu(x):
    return x * jax.nn.sigmoid(x)


def _pick_lane(blk, idx):
    lane = lax.broadcasted_iota(jnp.int32, blk.shape, 1)
    return jnp.sum(jnp.where(lane == idx, blk, 0.0), axis=-1, keepdims=True)


def _norm_cast_kernel(y_ref, g_ref, o_ref):
    y = y_ref[...]
    inv = lax.rsqrt(jnp.mean(y * y, axis=-1, keepdims=True) + RMS_EPS)
    o_ref[...] = (y * inv * g_ref[...]).astype(o_ref.dtype)


def _norm_cast(y, g):
    m, d = y.shape
    tr = _tile(m, 256)
    return pl.pallas_call(
        _norm_cast_kernel,
        grid=(m // tr,),
        in_specs=[pl.BlockSpec((tr, d), lambda i: (i, 0)), pl.BlockSpec((1, d), lambda i: (0, 0))],
        out_specs=pl.BlockSpec((tr, d), lambda i: (i, 0)),
        out_shape=jax.ShapeDtypeStruct((m, d), BF16),
        compiler_params=_params(("parallel",), 2 * tr * d * 6),
        name="norm_cast",
    )(y, g.reshape(1, d))


def _resid_norm_kernel(d_ref, y_ref, gp_ref, gn_ref, yo_ref, xo_ref):
    d = d_ref[...]
    inv = lax.rsqrt(jnp.mean(d * d, axis=-1, keepdims=True) + RMS_EPS)
    yn = y_ref[...] + d * inv * gp_ref[...]
    yo_ref[...] = yn
    inv2 = lax.rsqrt(jnp.mean(yn * yn, axis=-1, keepdims=True) + RMS_EPS)
    xo_ref[...] = (yn * inv2 * gn_ref[...]).astype(xo_ref.dtype)


def _resid_norm(delta, y, g_post, g_next):
    m, d = y.shape
    tr = _tile(m, 256)
    row = pl.BlockSpec((tr, d), lambda i: (i, 0))
    vec = pl.BlockSpec((1, d), lambda i: (0, 0))
    return pl.pallas_call(
        _resid_norm_kernel,
        grid=(m // tr,),
        in_specs=[row, row, vec, vec],
        out_specs=[row, row],
        out_shape=[jax.ShapeDtypeStruct((m, d), F32), jax.ShapeDtypeStruct((m, d), BF16)],
        compiler_params=_params(("parallel",), 2 * tr * d * 14),
        name="resid_norm",
    )(delta, y, g_post.reshape(1, d), g_next.reshape(1, d))


def _mm_kernel(*refs, nk, act, rider, transposed):
    if rider:
        x_ref, xr_ref, w_ref, o_ref, or_ref = refs
    else:
        x_ref, w_ref, o_ref = refs
    wb = w_ref[...].astype(BF16)
    k = pl.program_id(2)
    dims = NT_DIMS if transposed else (((1,), (0,)), ((), ()))

    def product(lhs_ref, out_ref):
        part = lax.dot_general(lhs_ref[...], wb, dims, preferred_element_type=F32)
        if nk == 1:
            if act == "relu2":
                part = jnp.square(jnp.maximum(part, 0.0))
            out_ref[...] = part.astype(out_ref.dtype)
        else:
            @pl.when(k == 0)
            def _():
                out_ref[...] = jnp.zeros_like(out_ref)

            out_ref[...] += part

    product(x_ref, o_ref)
    if rider:
        i = pl.program_id(0)

        @pl.when(i == 0)
        def _():
            product(xr_ref, or_ref)

        @pl.when(i > 0)
        def _():
            or_ref[...] = jnp.zeros_like(or_ref)


def _matmul(x, w, *, rider=None, layer=None, col0=0, n=None, out_dtype=F32, act=None, transposed=False,
            tm=1024, tn=512, tk=4096):
    m, kdim = x.shape
    n_axis, k_axis = (-2, -1) if transposed else (-1, -2)
    n = w.shape[n_axis] - col0 if n is None else n
    tm, tn, tk = _tile(m, tm), _tile(n, tn), _tile(kdim, tk)
    assert col0 % tn == 0 and w.shape[k_axis] == kdim
    nk = kdim // tk
    assert nk == 1 or (act is None and out_dtype == F32)
    cb = col0 // tn
    lead = () if layer is None else (None,)
    at = (lambda *idx: idx) if layer is None else (lambda *idx: (layer,) + idx)
    if transposed:
        w_spec = pl.BlockSpec(lead + (tn, tk), lambda i, j, k: at(cb + j, k))
    else:
        w_spec = pl.BlockSpec(lead + (tk, tn), lambda i, j, k: at(k, cb + j))
    osz = jnp.dtype(out_dtype).itemsize
    in_specs = [pl.BlockSpec((tm, tk), lambda i, j, k: (i, k))]
    out_specs = [pl.BlockSpec((tm, tn), lambda i, j, k: (i, j))]
    out_shape = [jax.ShapeDtypeStruct((m, n), out_dtype)]
    operands = [x]
    mr = 0
    if rider is not None:
        mr = rider.shape[0]
        spare = min(1, m // tm - 1)
        in_specs.append(pl.BlockSpec((mr, tk), lambda i, j, k: (0, k)))
        out_specs.append(pl.BlockSpec((None, mr, tn), lambda i, j, k: (jnp.minimum(i, spare), 0, j)))
        out_shape.append(jax.ShapeDtypeStruct((1 + spare, mr, n), out_dtype))
        operands.append(rider)
    vmem = 2 * ((tm + mr) * tk * 2 + tk * tn * 4 + (tm + mr) * tn * osz) + tk * tn * 2 + 2 * tm * tn * 4
    outs = pl.pallas_call(
        functools.partial(_mm_kernel, nk=nk, act=act, rider=rider is not None, transposed=transposed),
        grid=(m // tm, n // tn, nk),
        in_specs=in_specs + [w_spec],
        out_specs=out_specs,
        out_shape=out_shape,
        compiler_params=_params(("parallel", "parallel", "arbitrary"), vmem),
        name="matmul",
    )(*operands, w)
    return outs[0] if rider is None else (outs[0], outs[1][0])


def _matmul_groups(x, w, **kw):
    out, out_r = _matmul(x["prompt"], w, rider=x["sample"], **kw)
    return {"prompt": out, "sample": out_r}


def _mem_attn_kernel(q_ref, k_ref, v_ref, o_ref, *, heads):
    dh = q_ref.shape[-1] // heads
    scale = dh ** -0.5
    cols = [slice(hh * dh, (hh + 1) * dh) for hh in range(heads)]
    of = lambda ref: [ref[:, sl] for sl in cols]
    s = _each(lambda q, k: _dot_nt(q, k) * scale, of(q_ref), of(k_ref))
    p = _each(lambda s_: jnp.exp(s_ - jnp.max(s_, axis=-1, keepdims=True)), s)
    o = _each(lambda p_, v: _dot(p_, v) / jnp.sum(p_, axis=-1, keepdims=True), p, of(v_ref))
    for sl, o_ in zip(cols, o):
        o_ref[:, sl] = o_.astype(o_ref.dtype)


def _mem_attn(tail, k_arr, v_arr, kv_index, batch, t, heads, mem_width, mt):
    tt = _tile(t, 512)
    nt = t // tt
    lead = (None,) * (k_arr.ndim - 2)
    k_spec = pl.BlockSpec(lead + (mt, mem_width), lambda b, i: kv_index(b, 0))
    v_spec = pl.BlockSpec(lead + (mt, mem_width), lambda b, i: kv_index(b, 1))
    return pl.pallas_call(
        functools.partial(_mem_attn_kernel, heads=heads),
        grid=(batch, nt),
        in_specs=[pl.BlockSpec((tt, mem_width), lambda b, i: (b * nt + i, 0)), k_spec, v_spec],
        out_specs=pl.BlockSpec((tt, mem_width), lambda b, i: (b * nt + i, 0)),
        out_shape=jax.ShapeDtypeStruct((batch * t, mem_width), BF16),
        compiler_params=_params(("parallel", "parallel"), 2 * (tt * mem_width * 6 + 2 * mt * mem_width * 4) + 4 * tt * mt * 4),
        name="mem_attn",
    )(tail, k_arr, v_arr)


def _logf_kernel(f_ref, b_ref, o_ref):
    o_ref[...] = _log_sigmoid(f_ref[...] + b_ref[...])


def _fox_logf(tail, gate_block, bias_pad):
    m = tail.shape[0]
    tr = _tile(m, 1024)
    return pl.pallas_call(
        _logf_kernel,
        grid=(m // tr,),
        in_specs=[pl.BlockSpec((tr, LANES), lambda i: (i, gate_block)), pl.BlockSpec((1, LANES), lambda i: (0, 0))],
        out_specs=pl.BlockSpec((tr, LANES), lambda i: (i, 0)),
        out_shape=jax.ShapeDtypeStruct((m, LANES), F32),
        compiler_params=_params(("parallel",), 4 * tr * LANES * 4),
        name="fox_logf",
    )(tail, bias_pad)


def _cumsum_kernel(x_ref, o_ref, carry_ref, *, carry):
    tr = x_ref.shape[0]
    r = lax.broadcasted_iota(jnp.int32, (tr, tr), 0)
    c = lax.broadcasted_iota(jnp.int32, (tr, tr), 1)
    lower = (r >= c).astype(F32)
    cs = jnp.dot(lower, x_ref[...], precision=HIGHEST, preferred_element_type=F32)
    if carry:
        @pl.when(pl.program_id(1) == 0)
        def _():
            carry_ref[...] = jnp.zeros_like(carry_ref)

        cs = cs + carry_ref[...]
        carry_ref[...] = cs[tr - 1:tr, :]
    o_ref[...] = cs


def _cumsum_rows(x, batch, t, *, carry):
    tr = LANES
    nt = t // tr
    return pl.pallas_call(
        functools.partial(_cumsum_kernel, carry=carry),
        grid=(batch, nt),
        in_specs=[pl.BlockSpec((tr, LANES), lambda b, i: (b * nt + i, 0))],
        out_specs=pl.BlockSpec((tr, LANES), lambda b, i: (b * nt + i, 0)),
        out_shape=jax.ShapeDtypeStruct(x.shape, F32),
        scratch_shapes=[pltpu.VMEM((1, LANES), F32)],
        compiler_params=_params(("parallel", "arbitrary"), 8 * tr * LANES * 4),
        name="cumsum_rows",
    )(x)


def _fox_prompt_kernel(q_ref, k_ref, v_ref, crow_ref, o_ref, kb_ref, vb_ref, *, n_q):
    qi = pl.program_id(2)
    tq = q_ref.shape[0]
    scale = HEAD_DIM ** -0.5

    @pl.when(qi == 0)
    def _():
        kb_ref[...] = k_ref[...].astype(BF16)
        vb_ref[...] = v_ref[...].astype(BF16)

    def attend(extent):
        s = _dot_nt(q_ref[...], kb_ref[0:extent, :]) * scale - crow_ref[:, 0:extent]
        row = lax.broadcasted_iota(jnp.int32, (tq, extent), 0) + (extent - tq)
        col = lax.broadcasted_iota(jnp.int32, (tq, extent), 1)
        s = jnp.where(col <= row, s, -jnp.inf)
        p = jnp.exp(s - jnp.max(s, axis=-1, keepdims=True))
        o = _dot(p, vb_ref[0:extent, :]) / jnp.sum(p, axis=-1, keepdims=True)
        o_ref[...] = o.astype(o_ref.dtype)

    for tile in range(n_q):
        pl.when(qi == tile)(functools.partial(attend, (tile + 1) * tq))


def _fox_prompt(q, k, v, c_row, batch, t, heads):
    tq = _tile(t, 256)
    nq = t // tq
    blk = pl.BlockSpec((tq, HEAD_DIM), lambda b, h, i: (b * nq + i, h))
    full = pl.BlockSpec((t, HEAD_DIM), lambda b, h, i: (b, h))
    return pl.pallas_call(
        functools.partial(_fox_prompt_kernel, n_q=nq),
        grid=(batch, heads, nq),
        in_specs=[blk, full, full, pl.BlockSpec((None, None, 1, t), lambda b, h, i: (b, h, 0, 0))],
        out_specs=blk,
        out_shape=jax.ShapeDtypeStruct((batch * t, heads * HEAD_DIM), BF16),
        scratch_shapes=[pltpu.VMEM((t, HEAD_DIM), BF16), pltpu.VMEM((t, HEAD_DIM), BF16)],
        compiler_params=_params(("parallel", "parallel", "arbitrary"), 5 * t * HEAD_DIM * 4 + 6 * tq * t * 4),
        name="fox_prompt",
    )(q, k, v, c_row)


def _fox_decode_kernel(pt_ref, q_ref, kn_ref, vn_ref, lfn_ref, *rest, n_steps, per_step, heads, t_new, page):
    del pt_ref
    pools = [rest[3 * i:3 * i + 3] for i in range(per_step)]
    o_ref, m_ref, l_ref, acc_ref, carry_ref = rest[3 * per_step:]
    p = pl.program_id(1)
    scale = HEAD_DIM ** -0.5
    r = lax.broadcasted_iota(jnp.int32, (page, page), 0)
    c = lax.broadcasted_iota(jnp.int32, (page, page), 1)
    upper = (r <= c).astype(F32)

    @pl.when(p == 0)
    def _():
        m_ref[...] = jnp.full_like(m_ref, -jnp.inf)
        l_ref[...] = jnp.zeros_like(l_ref)
        acc_ref[...] = jnp.zeros_like(acc_ref)
        carry_ref[...] = jnp.zeros_like(carry_ref)

    head_cols = lambda h: slice(h * HEAD_DIM, (h + 1) * HEAD_DIM)

    def attend(k_heads, v_heads, lf_t, mask):
        c_t = jnp.dot(lf_t, upper, precision=HIGHEST, preferred_element_type=F32) + carry_ref[...]
        carry_ref[...] = c_t[:, page - 1:page]
        s = jnp.stack([_dot_nt(q_ref[:, head_cols(h)], k_heads[h]) for h in range(heads)])
        s = s * scale - c_t[:, None, :]
        if mask is not None:
            s = jnp.where(mask[None], s, -jnp.inf)
        m_old = m_ref[...]
        m_new = jnp.maximum(m_old, jnp.max(s, axis=-1, keepdims=True))
        alpha = jnp.exp(m_old - m_new)
        pr = jnp.exp(s - m_new)
        l_ref[...] = alpha * l_ref[...] + jnp.sum(pr, axis=-1, keepdims=True)
        pv = jnp.stack([_dot(pr[h], v_heads[h]) for h in range(heads)])
        acc_ref[...] = alpha * acc_ref[...] + pv
        m_ref[...] = m_new

    @pl.when(p < n_steps)
    def _():
        for kp_ref, vp_ref, lfp_ref in pools:
            attend([kp_ref[pl.ds(h, page, stride=heads), :] for h in range(heads)],
                   [vp_ref[pl.ds(h, page, stride=heads), :] for h in range(heads)], lfp_ref[...], None)

    @pl.when(p == n_steps)
    def _():
        pad = jnp.zeros((page - t_new, HEAD_DIM), F32)
        qrow = lax.broadcasted_iota(jnp.int32, (t_new, page), 0)
        kcol = lax.broadcasted_iota(jnp.int32, (t_new, page), 1)
        attend([jnp.concatenate([kn_ref[:, head_cols(h)], pad], axis=0) for h in range(heads)],
               [jnp.concatenate([vn_ref[:, head_cols(h)], pad], axis=0) for h in range(heads)],
               lfn_ref[...], kcol <= qrow)
        out = acc_ref[...] / l_ref[...]
        for h in range(heads):
            o_ref[:, head_cols(h)] = out[h].astype(o_ref.dtype)


def _fox_decode(q, k, v, lfn_t, k_pool, v_pool, lf_pool_t, page_table, layer, batch, t_new, heads):
    n_pages = page_table.shape[1]
    page = k_pool.shape[2] // heads
    width = heads * HEAD_DIM
    assert page == LANES and t_new <= page
    per_step = 2 if n_pages % 2 == 0 else 1
    n_steps = n_pages // per_step
    new = pl.BlockSpec((t_new, width), lambda b, p, pt: (b, 0))

    def page_specs(slot):
        at = lambda b, p, pt: (layer, pt[b, jnp.minimum(p, n_steps - 1) * per_step + slot], 0, 0)
        return [pl.BlockSpec((None, None, page * heads, HEAD_DIM), at),
                pl.BlockSpec((None, None, page * heads, HEAD_DIM), at),
                pl.BlockSpec((None, None, heads, page), at)]

    grid_spec = pltpu.PrefetchScalarGridSpec(
        num_scalar_prefetch=1,
        grid=(batch, n_steps + 1),
        in_specs=[new, new, new, pl.BlockSpec((None, heads, page), lambda b, p, pt: (b, 0, 0))]
        + [spec for slot in range(per_step) for spec in page_specs(slot)],
        out_specs=new,
        scratch_shapes=[pltpu.VMEM((heads, t_new, 1), F32), pltpu.VMEM((heads, t_new, 1), F32),
                        pltpu.VMEM((heads, t_new, HEAD_DIM), F32), pltpu.VMEM((heads, 1), F32)],
    )
    return pl.pallas_call(
        functools.partial(_fox_decode_kernel, n_steps=n_steps, per_step=per_step, heads=heads, t_new=t_new,
                          page=page),
        grid_spec=grid_spec,
        out_shape=jax.ShapeDtypeStruct((batch * t_new, width), BF16),
        compiler_params=_params(("parallel", "arbitrary"),
                                4 * per_step * page * width * 4 + 8 * t_new * width * 4 + (4 << 20)),
        name="fox_decode",
    )(page_table, q, k, v, lfn_t, *([k_pool, v_pool, lf_pool_t] * per_step))


def _shortconv_kernel(gb_ref, gc_ref, h_ref, buf_ref, w_ref, o_ref, tail_ref, us_ref, *, width):
    tt = gb_ref.shape[0]

    @pl.when(pl.program_id(2) == 0)
    def _():
        us_ref[0:SUBLANES, :] = buf_ref[...]

    us_ref[SUBLANES:SUBLANES + tt, :] = gc_ref[...] * h_ref[...]
    conv = None
    for k in range(width):
        off = SUBLANES - (width - 1) + k
        term = w_ref[k:k + 1, :] * us_ref[off:off + tt, :]
        conv = term if conv is None else conv + term
    o_ref[...] = (gb_ref[...] * conv).astype(o_ref.dtype)
    last = us_ref[tt:tt + SUBLANES, :]
    tail_ref[...] = last
    us_ref[0:SUBLANES, :] = last


def _shortconv(proj, buf8, w8, batch, t, mix_width, width):
    tt = _tile(t, 512)
    tc = _tile(mix_width, 512)
    nt, nc = t // tt, mix_width // tc
    col = lambda off: pl.BlockSpec((tt, tc), lambda b, c, i: (b * nt + i, off * nc + c))
    return pl.pallas_call(
        functools.partial(_shortconv_kernel, width=width),
        grid=(batch, nc, nt),
        in_specs=[col(0), col(1), col(2),
                  pl.BlockSpec((None, SUBLANES, tc), lambda b, c, i: (b, 0, c)),
                  pl.BlockSpec((SUBLANES, tc), lambda b, c, i: (0, c))],
        out_specs=[col(0), pl.BlockSpec((None, SUBLANES, tc), lambda b, c, i: (b, 0, c))],
        out_shape=[jax.ShapeDtypeStruct((batch * t, mix_width), BF16),
                   jax.ShapeDtypeStruct((batch, SUBLANES, mix_width), F32)],
        scratch_shapes=[pltpu.VMEM((tt + SUBLANES, tc), F32)],
        compiler_params=_params(("parallel", "parallel", "arbitrary"), 12 * tt * tc * 4),
        name="shortconv",
    )(proj, proj, proj, buf8, w8)


def _gdn_conv_kernel(x_ref, buf_ref, w_ref, o_ref, xs_ref, *, width, l2norm, scale):
    tt, tc = x_ref.shape

    @pl.when(pl.program_id(2) == 0)
    def _():
        xs_ref[0:SUBLANES, :] = buf_ref[...]

    xs_ref[SUBLANES:SUBLANES + tt, :] = x_ref[...]
    conv = None
    for k in range(width):
        off = SUBLANES - (width - 1) + k
        term = w_ref[k:k + 1, :] * xs_ref[off:off + tt, :]
        conv = term if conv is None else conv + term
    y = _silu(conv)
    if l2norm:
        for g in range(tc // HEAD_DIM):
            sl = slice(g * HEAD_DIM, (g + 1) * HEAD_DIM)
            seg = y[:, sl]
            seg = seg * lax.rsqrt(jnp.sum(seg * seg, axis=-1, keepdims=True) + L2_EPS)
            o_ref[:, sl] = seg * scale if scale != 1.0 else seg
    else:
        o_ref[...] = y
    xs_ref[0:SUBLANES, :] = xs_ref[tt:tt + SUBLANES, :]


def _gdn_conv(proj, buf8, w8, part, batch, t, mix_width, width, *, l2norm, scale=1.0):
    tt = _tile(t, 512)
    tc = _tile(mix_width, 512)
    nt, nc = t // tt, mix_width // tc
    return pl.pallas_call(
        functools.partial(_gdn_conv_kernel, width=width, l2norm=l2norm, scale=scale),
        grid=(batch, nc, nt),
        in_specs=[pl.BlockSpec((tt, tc), lambda b, c, i: (b * nt + i, part * nc + c)),
                  pl.BlockSpec((None, SUBLANES, tc), lambda b, c, i: (b, 0, part * nc + c)),
                  pl.BlockSpec((SUBLANES, tc), lambda b, c, i: (0, part * nc + c))],
        out_specs=pl.BlockSpec((tt, tc), lambda b, c, i: (b * nt + i, c)),
        out_shape=jax.ShapeDtypeStruct((batch * t, mix_width), F32),
        scratch_shapes=[pltpu.VMEM((tt + SUBLANES, tc), F32)],
        compiler_params=_params(("parallel", "parallel", "arbitrary"), 10 * tt * tc * 4),
        name="gdn_conv",
    )(proj, buf8, w8)


def _gdn_gate_kernel(a_ref, bt_ref, alog_ref, dt_ref, g_ref, beta_ref):
    g_ref[...] = -jnp.exp(alog_ref[...]) * _softplus(a_ref[...] + dt_ref[...])
    beta_ref[...] = jax.nn.sigmoid(bt_ref[...])


def _gdn_gates(tail, a_block, bt_block, a_log_pad, dt_pad):
    m = tail.shape[0]
    tr = _tile(m, 1024)
    blk = lambda off: pl.BlockSpec((tr, LANES), lambda i: (i, off))
    vec = pl.BlockSpec((1, LANES), lambda i: (0, 0))
    return pl.pallas_call(
        _gdn_gate_kernel,
        grid=(m // tr,),
        in_specs=[blk(a_block), blk(bt_block), vec, vec],
        out_specs=[blk(0), blk(0)],
        out_shape=[jax.ShapeDtypeStruct((m, LANES), F32)] * 2,
        compiler_params=_params(("parallel",), 8 * tr * LANES * 4),
        name="gdn_gates",
    )(tail, tail, a_log_pad, dt_pad)


def _unit_lower_inverse_offdiag(mats, ri, ci):
    n = mats[0].shape[0]
    blk = lambda idx, size: jnp.right_shift(idx, size.bit_length() - 1)
    base = SUBLANES
    in_base = blk(ri, base) == blk(ci, base)
    a0 = _each(lambda a: jnp.where(in_base, a, 0.0), mats)
    low = _each(lambda a: -a, a0)
    pw = _each(_dot, a0, a0)
    low = _each(lambda l, p, lp: l + p + lp, low, pw, _each(_dot, low, pw))
    pw = _each(_dot, pw, pw)
    low = _each(lambda l, p, lp: l + p + lp, low, pw, _each(_dot, low, pw))
    s = base
    while s < n:
        sub = (blk(ri, 2 * s) == blk(ci, 2 * s)) & (blk(ri, s) != blk(ci, s))
        off = _each(lambda a: jnp.where(sub, a, 0.0), mats)
        x = _each(lambda o, lo: o + lo, off, _each(_dot, low, off))
        low = _each(lambda l, x_, xl: l - (x_ + xl), low, x, _each(_dot, x, low))
        s *= 2
    return low


def _gdn_intra_kernel(q_ref, k_ref, v_ref, gc_ref, beta_ref, u_ref, w_ref, qk_ref, qg_ref, kg_ref, *, group):
    hg = pl.program_id(1)
    n = q_ref.shape[0]
    ri = lax.broadcasted_iota(jnp.int32, (n, n), 0)
    ci = lax.broadcasted_iota(jnp.int32, (n, n), 1)
    incl = ri >= ci
    cols = [slice(i * HEAD_DIM, (i + 1) * HEAD_DIM) for i in range(group)]
    q = [q_ref[:, sl] for sl in cols]
    k = [k_ref[:, sl] for sl in cols]
    v = [v_ref[:, sl] for sl in cols]
    gcol = [_pick_lane(gc_ref[...], hg * group + i) for i in range(group)]
    bcol = [_pick_lane(beta_ref[...], hg * group + i) for i in range(group)]

    def decay_of(g):
        gmat = jnp.broadcast_to(g, (n, n))
        return jnp.where(incl, jnp.exp(jnp.where(incl, gmat - gmat.T, 0.0)), 0.0)

    decay = _each(decay_of, gcol)
    kb = _each(lambda k_, b: k_ * b, k, bcol)
    a = _each(lambda kk, d: jnp.where(ri > ci, kk * d, 0.0), _each(_dot_nt, kb, k), decay)
    low = _unit_lower_inverse_offdiag(a, ri, ci)
    eg = _each(jnp.exp, gcol)
    rhs = _each(lambda v_, b, kb_, e: jnp.concatenate([v_ * b, kb_ * e], axis=1), v, bcol, kb, eg)
    uw = _each(lambda r, lr: r + lr, rhs, _each(_dot, low, rhs))
    qk = _each(lambda x, d: x * d, _each(_dot_nt, q, k), decay)
    for i, sl in enumerate(cols):
        u_ref[:, sl] = uw[i][:, :HEAD_DIM]
        w_ref[:, sl] = uw[i][:, HEAD_DIM:].astype(w_ref.dtype)
        qk_ref[:, sl] = qk[i].astype(qk_ref.dtype)
        qg_ref[:, sl] = (q[i] * eg[i]).astype(qg_ref.dtype)
        kg_ref[:, sl] = (k[i] * jnp.exp(gcol[i][n - 1:n, :] - gcol[i])).astype(kg_ref.dtype)


def _gdn_intra(qn, kn, vc, gc, beta, batch, t, heads, group):
    n = GDN_CHUNK
    nc = t // n
    wide = pl.BlockSpec((n, group * HEAD_DIM), lambda b, g, c: (b * nc + c, g))
    gate = pl.BlockSpec((n, LANES), lambda b, g, c: (b * nc + c, 0))
    shape = lambda dt: jax.ShapeDtypeStruct((batch * t, heads * HEAD_DIM), dt)
    return pl.pallas_call(
        functools.partial(_gdn_intra_kernel, group=group),
        grid=(batch, heads // group, nc),
        in_specs=[wide, wide, wide, gate, gate],
        out_specs=[wide] * 5,
        out_shape=[shape(F32)] + [shape(BF16)] * 4,
        compiler_params=_params(("parallel", "parallel", "parallel"), 2 * 8 * n * group * HEAD_DIM * 4 + (8 << 20)),
        name="gdn_intra",
    )(qn, kn, vc, gc, beta)


def _gdn_state_kernel(u_ref, w_ref, qk_ref, qg_ref, kg_ref, gc_ref, z_ref, gout_ref, s0_ref, o_ref, sout_ref,
                      s_ref, *, group, n_chunks):
    hg = pl.program_id(1)
    c = pl.program_id(2)
    n = u_ref.shape[0]

    @pl.when(c == 0)
    def _():
        s_ref[...] = s0_ref[...]

    cols = [slice(i * HEAD_DIM, (i + 1) * HEAD_DIM) for i in range(group)]
    of = lambda ref: [ref[:, sl] for sl in cols]
    s = [s_ref[i] for i in range(group)]
    v_new = _each(lambda u, ws: u - ws, of(u_ref), _each(_dot, of(w_ref), s))
    o = _each(lambda x, y: x + y, _each(_dot, of(qg_ref), s), _each(_dot, of(qk_ref), v_new))
    upd = _each(_dot_tn, of(kg_ref), v_new)
    for i, sl in enumerate(cols):
        g_last = _pick_lane(gc_ref[n - 1:n, :], hg * group + i)
        s_ref[i] = s[i] * jnp.exp(g_last) + upd[i]
        on = o[i] * lax.rsqrt(jnp.mean(o[i] * o[i], axis=-1, keepdims=True) + RMS_EPS) * gout_ref[...]
        o_ref[:, sl] = (on * _silu(z_ref[:, sl])).astype(o_ref.dtype)

    @pl.when(c == n_chunks - 1)
    def _():
        sout_ref[...] = s_ref[...]


def _gdn_state(u, w, qk, qg, kg, gc, z_arr, z_block0, g_out, s0, batch, t, heads, group):
    n = GDN_CHUNK
    nc = t // n
    gw = group * HEAD_DIM
    wide = pl.BlockSpec((n, gw), lambda b, g, c: (b * nc + c, g))
    state = pl.BlockSpec((None, group, HEAD_DIM, HEAD_DIM), lambda b, g, c: (b, g, 0, 0))
    return pl.pallas_call(
        functools.partial(_gdn_state_kernel, group=group, n_chunks=nc),
        grid=(batch, heads // group, nc),
        in_specs=[wide] * 5 + [pl.BlockSpec((n, LANES), lambda b, g, c: (b * nc + c, 0)),
                               pl.BlockSpec((n, gw), lambda b, g, c: (b * nc + c, z_block0 + g)),
                               pl.BlockSpec((1, HEAD_DIM), lambda b, g, c: (0, 0)), state],
        out_specs=[wide, state],
        out_shape=[jax.ShapeDtypeStruct((batch * t, heads * HEAD_DIM), BF16),
                   jax.ShapeDtypeStruct((batch, heads, HEAD_DIM, HEAD_DIM), F32)],
        scratch_shapes=[pltpu.VMEM((group, HEAD_DIM, HEAD_DIM), F32)],
        compiler_params=_params(("parallel", "parallel", "arbitrary"), 2 * 8 * n * gw * 4 + 6 * group * HEAD_DIM * HEAD_DIM * 4 + (4 << 20)),
        name="gdn_state",
    )(u, w, qk, qg, kg, gc, z_arr, g_out.reshape(1, HEAD_DIM), s0)


def _pad_lanes(v, width=LANES):
    return jnp.pad(v, [(0, 0)] * (v.ndim - 1) + [(0, width - v.shape[-1])])


def _pad_rows8(a, rows_axis):
    pad = [(0, 0)] * a.ndim
    pad[rows_axis] = (SUBLANES - a.shape[rows_axis], 0)
    return jnp.pad(a, pad)


def _in_weight(w_stack):
    if w_stack.shape[2] % LANES:
        return jnp.swapaxes(w_stack, 1, 2), True
    return w_stack, False


def _tail_weight(w, transposed, j, main, parts, mem_width):
    n_axis = 0 if transposed else 1
    total = w.shape[1 + n_axis]
    take = (lambda a, b: w[j, a:b, :]) if transposed else (lambda a, b: w[j, :, a:b])
    pad = [(0, 0), (0, 0)]
    cols = [take(total - mem_width, total)]
    off = main
    for width in parts:
        pad[n_axis] = (0, LANES - width)
        cols.append(jnp.pad(take(off, off + width), pad))
        off += width
    return jnp.concatenate(cols, axis=n_axis)


def _tail_proj(x, w_tail, transposed):
    n = w_tail.shape[0 if transposed else 1]
    return _matmul_groups(x, w_tail, transposed=transposed, tn=256 if n % 256 == 0 else LANES)


def _fox_mixer(x, dims, w_in_fox, b_fox_f, j, caches, page_table, heads, mem_width):
    mix_width = heads * HEAD_DIM
    main = 3 * mix_width
    w_in, w_t = _in_weight(w_in_fox)
    w_tail = _tail_weight(w_in, w_t, j, main, [heads], mem_width)
    gate_block = mem_width // LANES
    bias = _pad_lanes(b_fox_f[j].reshape(1, heads))
    out = {}
    qkv = [_matmul_groups(x, w_in, layer=j, col0=part * mix_width, n=mix_width, transposed=w_t) for part in range(3)]
    tails = _tail_proj(x, w_tail, w_t)
    for name, (batch, t) in dims.items():
        q, k, v = (part[name] for part in qkv)
        tail = tails[name]
        logf = _fox_logf(tail, gate_block, bias)
        if name == "prompt":
            c_tok = _cumsum_rows(logf, batch, t, carry=True)
            c_row = jnp.transpose(c_tok.reshape(batch, t, LANES)[:, :, :heads], (0, 2, 1))[:, :, None, :]
            mix = _fox_prompt(q, k, v, c_row, batch, t, heads)
        else:
            k_pool, v_pool, lf_pool_t = caches
            lfn_t = jnp.transpose(logf.reshape(batch, t, LANES)[:, :, :heads], (0, 2, 1))
            lfn_t = _pad_lanes(lfn_t, lf_pool_t.shape[-1])
            mix = _fox_decode(q, k, v, lfn_t, k_pool, v_pool, lf_pool_t, page_table, j, batch, t, heads)
        shp = (batch, t, heads, HEAD_DIM)
        out[name] = dict(mix=mix, tail=tail, k=k.reshape(shp), v=v.reshape(shp),
                         logf=logf[:, :heads].reshape(batch, t, heads))
    return out


def _conv_mixer(x, dims, w_in_conv, w_conv, j, state_conv, heads, mem_width):
    mix_width = heads * HEAD_DIM
    main = 3 * mix_width
    width = w_conv.shape[1]
    w8 = jnp.pad(w_conv[j], ((0, SUBLANES - width), (0, 0)))
    out = {}
    w_in, w_t = _in_weight(w_in_conv)
    projs = _matmul_groups(x, w_in, layer=j, col0=0, n=main, transposed=w_t)
    tails = _tail_proj(x, _tail_weight(w_in, w_t, j, main, [], mem_width), w_t)
    for name, (batch, t) in dims.items():
        proj, tail = projs[name], tails[name]
        if name == "prompt":
            buf8 = jnp.zeros((batch, SUBLANES, mix_width), F32)
        else:
            buf8 = _pad_rows8(state_conv[j], 1)
        mix, last8 = _shortconv(proj, buf8, w8, batch, t, mix_width, width)
        out[name] = dict(mix=mix, tail=tail, buf=last8[:, SUBLANES - (width - 1):, :])
    return out


def _gdn_mixer(x, dims, w_in_gdn, w_gdn_conv, gdn_a_log, gdn_dt_bias, g_gdn_out, j, state_gdn_conv, state_gdn_s,
               heads, mem_width):
    mix_width = heads * HEAD_DIM
    main = 4 * mix_width
    width = w_gdn_conv.shape[1]
    w_in, w_t = _in_weight(w_in_gdn)
    w_tail = _tail_weight(w_in, w_t, j, main, [heads, heads], mem_width)
    a_block = mem_width // LANES
    w8 = jnp.pad(w_gdn_conv[j], ((0, SUBLANES - width), (0, 0)))
    a_log = _pad_lanes(gdn_a_log[j].reshape(1, heads))
    dt_bias = _pad_lanes(gdn_dt_bias[j].reshape(1, heads))
    group = next(g for g in (8, 4, 2, 1) if heads % g == 0)
    out = {}
    projs = _matmul_groups(x, w_in, layer=j, col0=0, n=main, transposed=w_t)
    tails = _tail_proj(x, w_tail, w_t)
    for name, (batch, t) in dims.items():
        proj, tail = projs[name], tails[name]
        if name == "prompt":
            buf = jnp.zeros((batch, width - 1, 3 * mix_width), F32)
            s0 = jnp.zeros((batch, heads, HEAD_DIM, HEAD_DIM), F32)
        else:
            buf, s0 = state_gdn_conv[j], state_gdn_s[j]
        buf8 = _pad_rows8(buf, 1)
        conv = functools.partial(_gdn_conv, proj, buf8, w8, batch=batch, t=t, mix_width=mix_width, width=width)
        qn = conv(part=0, l2norm=True, scale=HEAD_DIM ** -0.5)
        kn = conv(part=1, l2norm=True)
        vc = conv(part=2, l2norm=False)
        g, beta = _gdn_gates(tail, a_block, a_block + 1, a_log, dt_bias)
        z_arr, z_block0 = proj, 3 * mix_width // (group * HEAD_DIM)
        tp = -(-t // GDN_CHUNK) * GDN_CHUNK
        if tp != t:
            pad = lambda a_: jnp.pad(a_.reshape(batch, t, -1), ((0, 0), (0, tp - t), (0, 0))).reshape(batch * tp, -1)
            qn, kn, vc, g, beta = (pad(a_) for a_ in (qn, kn, vc, g, beta))
            z_arr, z_block0 = pad(proj[:, 3 * mix_width:main]), 0
        gc = _cumsum_rows(g, batch, tp, carry=False)
        u, w, qk, qg, kg = _gdn_intra(qn, kn, vc, gc, beta, batch, tp, heads, group)
        mix, s_new = _gdn_state(u, w, qk, qg, kg, gc, z_arr, z_block0, g_gdn_out[j], s0, batch, tp, heads, group)
        if tp != t:
            mix = mix.reshape(batch, tp, mix_width)[:, :t].reshape(batch * t, mix_width)
        keep = min(t, width - 1)
        newest = proj.reshape(batch, t, -1)[:, t - keep:, :3 * mix_width]
        out[name] = dict(mix=mix, tail=tail, conv=jnp.concatenate([buf, newest], axis=1)[:, keep:], s=s_new)
    return out


def kernel(x_prompt, x_sample, cache_fox_k, cache_fox_v, cache_fox_logf, cache_mem_k, cache_mem_v, state_conv, state_gdn_conv, state_gdn_s, page_table, mem_prompt, g_mix_pre, g_mix_post, g_mlp_pre, g_mlp_post, g_mem, w_mem_kv, w_out, w_mlp_up, w_mlp_down, w_in_fox, b_fox_f, w_in_conv, w_conv, w_in_gdn, w_gdn_conv, gdn_a_log, gdn_dt_bias, g_gdn_out):
    bp, tp_, d = x_prompt.shape
    bs, ts, _ = x_sample.shape
    depth = g_mix_pre.shape[0]
    mem_tokens, mem_heads, mem_hd = cache_mem_k.shape[2:]
    mem_width = mem_heads * mem_hd
    mix_width = d - mem_width
    heads = mix_width // HEAD_DIM
    n_fox, n_pool, page = cache_fox_k.shape[:3]

    y = {"prompt": x_prompt.reshape(bp * tp_, d), "sample": x_sample.reshape(bs * ts, d)}
    dims = {"prompt": (bp, tp_), "sample": (bs, ts)}
    x = {n: _norm_cast(y[n], g_mix_pre[0]) for n in y}
    mem2d = mem_prompt.reshape(bp * mem_tokens, d)
    caches = (cache_fox_k.reshape(n_fox, n_pool, page * heads, HEAD_DIM),
              cache_fox_v.reshape(n_fox, n_pool, page * heads, HEAD_DIM),
              jnp.transpose(cache_fox_logf, (0, 1, 3, 2)))
    mem_k_s = cache_mem_k.reshape(depth, bs, mem_tokens, mem_width)
    mem_v_s = cache_mem_v.reshape(depth, bs, mem_tokens, mem_width)

    res = {k_: [] for k_ in ("fox_k_p", "fox_v_p", "fox_lf_p", "fox_k_s", "fox_v_s", "fox_lf_s", "conv_p", "conv_s",
                             "gconv_p", "gconv_s", "gs_p", "gs_s", "mem_k", "mem_v")}
    for i in range(depth):
        kind, j = i % 3, i // 3
        kv = _matmul(_norm_cast(mem2d, g_mem[i]), w_mem_kv, layer=i)
        res["mem_k"].append(kv[:, :mem_width].reshape(bp, mem_tokens, mem_heads, mem_hd))
        res["mem_v"].append(kv[:, mem_width:].reshape(bp, mem_tokens, mem_heads, mem_hd))
        if kind == 0:
            mixed = _fox_mixer(x, dims, w_in_fox, b_fox_f, j, caches, page_table, heads, mem_width)
            for n, s in (("prompt", "p"), ("sample", "s")):
                res["fox_k_" + s].append(mixed[n]["k"])
                res["fox_v_" + s].append(mixed[n]["v"])
                res["fox_lf_" + s].append(mixed[n]["logf"])
        elif kind == 1:
            mixed = _conv_mixer(x, dims, w_in_conv, w_conv, j, state_conv, heads, mem_width)
            res["conv_p"].append(mixed["prompt"]["buf"])
            res["conv_s"].append(mixed["sample"]["buf"])
        else:
            mixed = _gdn_mixer(x, dims, w_in_gdn, w_gdn_conv, gdn_a_log, gdn_dt_bias, g_gdn_out, j, state_gdn_conv,
                               state_gdn_s, heads, mem_width)
            for n, s in (("prompt", "p"), ("sample", "s")):
                res["gconv_" + s].append(mixed[n]["conv"])
                res["gs_" + s].append(mixed[n]["s"])
        g_next = g_mix_pre[i + 1] if i + 1 < depth else g_mlp_pre[i]
        cat = {}
        for n, (b, t) in dims.items():
            if n == "prompt":
                mem = _mem_attn(mixed[n]["tail"], kv, kv, lambda bb, part: (bb, part), b, t, mem_heads, mem_width,
                                mem_tokens)
            else:
                mem = _mem_attn(mixed[n]["tail"], mem_k_s, mem_v_s, lambda bb, part: (i, bb, 0, 0), b, t, mem_heads,
                                mem_width, mem_tokens)
            cat[n] = jnp.concatenate([mixed[n]["mix"], mem], axis=1)
        delta = _matmul_groups(cat, w_out, layer=i)
        x2 = {}
        for n in dims:
            y[n], x2[n] = _resid_norm(delta[n], y[n], g_mix_post[i], g_mlp_pre[i])
        hid = _matmul_groups(x2, w_mlp_up, layer=i, out_dtype=BF16, act="relu2")
        delta = _matmul_groups(hid, w_mlp_down, layer=i, tn=1024, tk=2048)
        for n in dims:
            y[n], x[n] = _resid_norm(delta[n], y[n], g_mlp_post[i], g_next)

    st = lambda k_: jnp.stack(res[k_])
    return (y["prompt"].reshape(bp, tp_, d), y["sample"].reshape(bs, ts, d),
            st("fox_k_p"), st("fox_v_p"), st("fox_lf_p"), st("conv_p"), st("gconv_p"), st("gs_p"),
            st("mem_k"), st("mem_v"),
            st("fox_k_s"), st("fox_v_s"), st("fox_lf_s"), st("conv_s"), st("gconv_s"), st("gs_s"))
```

```python
import functools

import jax
import jax.numpy as jnp
from jax import lax
from jax.experimental import pallas as pl
from jax.experimental.pallas import tpu as pltpu

F32 = jnp.float32
BF16 = jnp.bfloat16
RMS_EPS = 1e-6
L2_EPS = 1e-6
HEAD_DIM = 128
LANES = 128
SUBLANES = 8
GDN_CHUNK = 128
V7X_VMEM_BYTES = 64 * 1024 * 1024
VMEM_CAP = V7X_VMEM_BYTES - 6 * 1024 * 1024
HIGHEST = lax.Precision.HIGHEST
NT_DIMS = (((1,), (1,)), ((), ()))
TN_DIMS = (((0,), (0,)), ((), ()))


def _params(semantics, vmem_bytes):
    limit = int(min(VMEM_CAP, max(16 * 1024 * 1024, vmem_bytes * 5 // 4 + (2 << 20))))
    return pltpu.CompilerParams(dimension_semantics=semantics, vmem_limit_bytes=limit)


def _tile(dim, pref, align=LANES):
    if dim <= pref:
        return dim
    for cand in range(pref - pref % align, 0, -align):
        if dim % cand == 0:
            return cand
    raise ValueError((dim, pref, align))


def _dot(a, b):
    return jnp.dot(a.astype(BF16), b.astype(BF16), preferred_element_type=F32)


def _dot_nt(a, b):
    return lax.dot_general(a.astype(BF16), b.astype(BF16), NT_DIMS, preferred_element_type=F32)


def _dot_tn(a, b):
    return lax.dot_general(a.astype(BF16), b.astype(BF16), TN_DIMS, preferred_element_type=F32)


def _each(fn, *columns):
    return [fn(*args) for args in zip(*columns)]


def _log_sigmoid(x):
    return jnp.minimum(x, 0.0) - jnp.log1p(jnp.exp(-jnp.abs(x)))


def _softplus(x):
    return jnp.maximum(x, 0.0) + jnp.log1p(jnp.exp(-jnp.abs(x)))


def _silu(x):
    return x * jax.nn.sigmoid(x)


def _pick_lane(blk, idx):
    lane = lax.broadcasted_iota(jnp.int32, blk.shape, 1)
    return jnp.sum(jnp.where(lane == idx, blk, 0.0), axis=-1, keepdims=True)


def _norm_cast_kernel(y_ref, g_ref, o_ref):
    y = y_ref[...]
    inv = lax.rsqrt(jnp.mean(y * y, axis=-1, keepdims=True) + RMS_EPS)
    o_ref[...] = (y * inv * g_ref[...]).astype(o_ref.dtype)


def _norm_cast(y, g):
    m, d = y.shape
    tr = _tile(m, 256)
    return pl.pallas_call(
        _norm_cast_kernel,
        grid=(m // tr,),
        in_specs=[pl.BlockSpec((tr, d), lambda i: (i, 0)), pl.BlockSpec((1, d), lambda i: (0, 0))],
        out_specs=pl.BlockSpec((tr, d), lambda i: (i, 0)),
        out_shape=jax.ShapeDtypeStruct((m, d), BF16),
        compiler_params=_params(("parallel",), 2 * tr * d * 6),
        name="norm_cast",
    )(y, g.reshape(1, d))


def _resid_norm_kernel(d_ref, y_ref, gp_ref, gn_ref, yo_ref, xo_ref):
    d = d_ref[...]
    inv = lax.rsqrt(jnp.mean(d * d, axis=-1, keepdims=True) + RMS_EPS)
    yn = y_ref[...] + d * inv * gp_ref[...]
    yo_ref[...] = yn
    inv2 = lax.rsqrt(jnp.mean(yn * yn, axis=-1, keepdims=True) + RMS_EPS)
    xo_ref[...] = (yn * inv2 * gn_ref[...]).astype(xo_ref.dtype)


def _resid_norm(delta, y, g_post, g_next):
    m, d = y.shape
    tr = _tile(m, 256)
    row = pl.BlockSpec((tr, d), lambda i: (i, 0))
    vec = pl.BlockSpec((1, d), lambda i: (0, 0))
    return pl.pallas_call(
        _resid_norm_kernel,
        grid=(m // tr,),
        in_specs=[row, row, vec, vec],
        out_specs=[row, row],
        out_shape=[jax.ShapeDtypeStruct((m, d), F32), jax.ShapeDtypeStruct((m, d), BF16)],
        compiler_params=_params(("parallel",), 2 * tr * d * 14),
        name="resid_norm",
    )(delta, y, g_post.reshape(1, d), g_next.reshape(1, d))


def _mm_kernel(*refs, nk, act, rider, transposed, parts):
    x_refs, refs = refs[:parts], refs[parts:]
    if rider:
        xr_refs, (w_ref, o_ref, or_ref) = refs[:parts], refs[parts:]
    else:
        w_ref, o_ref = refs
    k = pl.program_id(2)
    dims = NT_DIMS if transposed else (((1,), (0,)), ((), ()))

    def product(lhs_refs, out_ref):
        lhs = lhs_refs[0][...] if parts == 1 else jnp.concatenate([r[...] for r in lhs_refs], axis=1)
        part = lax.dot_general(lhs, w_ref[...].astype(BF16), dims, preferred_element_type=F32)
        if nk == 1:
            if act == "relu2":
                part = jnp.square(jnp.maximum(part, 0.0))
            out_ref[...] = part.astype(out_ref.dtype)
        else:
            @pl.when(k == 0)
            def _():
                out_ref[...] = jnp.zeros_like(out_ref)

            out_ref[...] += part

    product(x_refs, o_ref)
    if rider:
        i = pl.program_id(0)

        @pl.when(i == 0)
        def _():
            product(xr_refs, or_ref)

        @pl.when(i > 0)
        def _():
            or_ref[...] = jnp.zeros_like(or_ref)


def _matmul(x, w, *, rider=None, layer=None, col0=0, n=None, out_dtype=F32, act=None, transposed=False,
            tm=1024, tn=512, tk=4096):
    xs = x if isinstance(x, (tuple, list)) else (x,)
    riders = () if rider is None else (rider if isinstance(rider, (tuple, list)) else (rider,))
    m, kdim = xs[0].shape[0], sum(part.shape[1] for part in xs)
    n_axis, k_axis = (-2, -1) if transposed else (-1, -2)
    n = w.shape[n_axis] - col0 if n is None else n
    tm, tn, tk = _tile(m, tm), _tile(n, tn), _tile(kdim, tk)
    assert col0 % tn == 0 and w.shape[k_axis] == kdim
    nk = kdim // tk
    assert nk == 1 or (act is None and out_dtype == F32 and len(xs) == 1)
    cb = col0 // tn
    widths = [tk] if len(xs) == 1 else [part.shape[1] for part in xs]
    lead = () if layer is None else (None,)
    at = (lambda *idx: idx) if layer is None else (lambda *idx: (layer,) + idx)
    if transposed:
        w_spec = pl.BlockSpec(lead + (tn, tk), lambda i, j, k: at(cb + j, k))
    else:
        w_spec = pl.BlockSpec(lead + (tk, tn), lambda i, j, k: at(k, cb + j))
    osz = jnp.dtype(out_dtype).itemsize
    in_specs = [pl.BlockSpec((tm, width), lambda i, j, k: (i, k)) for width in widths]
    out_specs = [pl.BlockSpec((tm, tn), lambda i, j, k: (i, j))]
    out_shape = [jax.ShapeDtypeStruct((m, n), out_dtype)]
    operands = list(xs)
    mr = 0
    if riders:
        mr = riders[0].shape[0]
        spare = min(1, m // tm - 1)
        in_specs += [pl.BlockSpec((mr, width), lambda i, j, k: (0, k)) for width in widths]
        out_specs.append(pl.BlockSpec((None, mr, tn), lambda i, j, k: (jnp.minimum(i, spare), 0, j)))
        out_shape.append(jax.ShapeDtypeStruct((1 + spare, mr, n), out_dtype))
        operands += list(riders)
    vmem = 2 * ((tm + mr) * tk * 2 + tk * tn * 4 + (tm + mr) * tn * osz) + tk * tn * 2 + 2 * tm * tn * 4
    outs = pl.pallas_call(
        functools.partial(_mm_kernel, nk=nk, act=act, rider=bool(riders), transposed=transposed, parts=len(xs)),
        grid=(m // tm, n // tn, nk),
        in_specs=in_specs + [w_spec],
        out_specs=out_specs,
        out_shape=out_shape,
        compiler_params=_params(("parallel", "parallel", "arbitrary"), vmem),
        name="matmul",
    )(*operands, w)
    return outs[0] if not riders else (outs[0], outs[1][0])


def _matmul_groups(x, w, **kw):
    out, out_r = _matmul(x["prompt"], w, rider=x["sample"], **kw)
    return {"prompt": out, "sample": out_r}


def _mem_attn_kernel(q_ref, k_ref, v_ref, o_ref, *, heads):
    dh = q_ref.shape[-1] // heads
    scale = dh ** -0.5
    cols = [slice(hh * dh, (hh + 1) * dh) for hh in range(heads)]
    of = lambda ref: [ref[:, sl] for sl in cols]
    s = _each(lambda q, k: _dot_nt(q, k) * scale, of(q_ref), of(k_ref))
    p = _each(lambda s_: jnp.exp(s_ - jnp.max(s_, axis=-1, keepdims=True)), s)
    o = _each(lambda p_, v: _dot(p_, v) / jnp.sum(p_, axis=-1, keepdims=True), p, of(v_ref))
    for sl, o_ in zip(cols, o):
        o_ref[:, sl] = o_.astype(o_ref.dtype)


def _mem_attn(tail, k_arr, v_arr, kv_index, batch, t, heads, mem_width, mt):
    tt = _tile(t, 512)
    nt = t // tt
    lead = (None,) * (k_arr.ndim - 2)
    k_spec = pl.BlockSpec(lead + (mt, mem_width), lambda b, i: kv_index(b, 0))
    v_spec = pl.BlockSpec(lead + (mt, mem_width), lambda b, i: kv_index(b, 1))
    return pl.pallas_call(
        functools.partial(_mem_attn_kernel, heads=heads),
        grid=(batch, nt),
        in_specs=[pl.BlockSpec((tt, mem_width), lambda b, i: (b * nt + i, 0)), k_spec, v_spec],
        out_specs=pl.BlockSpec((tt, mem_width), lambda b, i: (b * nt + i, 0)),
        out_shape=jax.ShapeDtypeStruct((batch * t, mem_width), BF16),
        compiler_params=_params(("parallel", "parallel"), 2 * (tt * mem_width * 6 + 2 * mt * mem_width * 4) + 4 * tt * mt * 4),
        name="mem_attn",
    )(tail, k_arr, v_arr)


def _logf_kernel(f_ref, b_ref, o_ref):
    o_ref[...] = _log_sigmoid(f_ref[...] + b_ref[...])


def _fox_logf(tail, gate_block, bias_pad):
    m = tail.shape[0]
    tr = _tile(m, 1024)
    return pl.pallas_call(
        _logf_kernel,
        grid=(m // tr,),
        in_specs=[pl.BlockSpec((tr, LANES), lambda i: (i, gate_block)), pl.BlockSpec((1, LANES), lambda i: (0, 0))],
        out_specs=pl.BlockSpec((tr, LANES), lambda i: (i, 0)),
        out_shape=jax.ShapeDtypeStruct((m, LANES), F32),
        compiler_params=_params(("parallel",), 4 * tr * LANES * 4),
        name="fox_logf",
    )(tail, bias_pad)


def _cumsum_kernel(x_ref, o_ref, carry_ref, *, carry):
    tr = x_ref.shape[0]
    r = lax.broadcasted_iota(jnp.int32, (tr, tr), 0)
    c = lax.broadcasted_iota(jnp.int32, (tr, tr), 1)
    lower = (r >= c).astype(F32)
    cs = jnp.dot(lower, x_ref[...], precision=HIGHEST, preferred_element_type=F32)
    if carry:
        @pl.when(pl.program_id(1) == 0)
        def _():
            carry_ref[...] = jnp.zeros_like(carry_ref)

        cs = cs + carry_ref[...]
        carry_ref[...] = cs[tr - 1:tr, :]
    o_ref[...] = cs


def _cumsum_rows(x, batch, t, *, carry):
    tr = LANES
    nt = t // tr
    return pl.pallas_call(
        functools.partial(_cumsum_kernel, carry=carry),
        grid=(batch, nt),
        in_specs=[pl.BlockSpec((tr, LANES), lambda b, i: (b * nt + i, 0))],
        out_specs=pl.BlockSpec((tr, LANES), lambda b, i: (b * nt + i, 0)),
        out_shape=jax.ShapeDtypeStruct(x.shape, F32),
        scratch_shapes=[pltpu.VMEM((1, LANES), F32)],
        compiler_params=_params(("parallel", "arbitrary"), 8 * tr * LANES * 4),
        name="cumsum_rows",
    )(x)


def _fox_prompt_kernel(q_ref, k_ref, v_ref, crow_ref, o_ref, kb_ref, vb_ref, *, n_q, group):
    qi = pl.program_id(2)
    tq = q_ref.shape[0]
    scale = HEAD_DIM ** -0.5
    cols = [slice(i * HEAD_DIM, (i + 1) * HEAD_DIM) for i in range(group)]

    @pl.when(qi == 0)
    def _():
        kb_ref[...] = k_ref[...].astype(BF16)
        vb_ref[...] = v_ref[...].astype(BF16)

    def attend(extent):
        row = lax.broadcasted_iota(jnp.int32, (tq, extent), 0) + (extent - tq)
        col = lax.broadcasted_iota(jnp.int32, (tq, extent), 1)
        s = [_dot_nt(q_ref[:, sl], kb_ref[0:extent, sl]) for sl in cols]
        s = [jnp.where(col <= row, s[i] * scale - crow_ref[i, :, 0:extent], -jnp.inf) for i in range(group)]
        p = _each(lambda s_: jnp.exp(s_ - jnp.max(s_, axis=-1, keepdims=True)), s)
        pv = [_dot(p[i], vb_ref[0:extent, cols[i]]) for i in range(group)]
        for i in range(group):
            o_ref[:, cols[i]] = (pv[i] / jnp.sum(p[i], axis=-1, keepdims=True)).astype(o_ref.dtype)

    for tile in range(n_q):
        pl.when(qi == tile)(functools.partial(attend, (tile + 1) * tq))


def _fox_prompt(q, k, v, c_row, batch, t, heads):
    tq = _tile(t, 256)
    nq = t // tq
    group = next(g for g in (4, 2, 1) if heads % g == 0)
    gw = group * HEAD_DIM
    blk = pl.BlockSpec((tq, gw), lambda b, g, i: (b * nq + i, g))
    full = pl.BlockSpec((t, gw), lambda b, g, i: (b, g))
    return pl.pallas_call(
        functools.partial(_fox_prompt_kernel, n_q=nq, group=group),
        grid=(batch, heads // group, nq),
        in_specs=[blk, full, full, pl.BlockSpec((None, group, 1, t), lambda b, g, i: (b, g, 0, 0))],
        out_specs=blk,
        out_shape=jax.ShapeDtypeStruct((batch * t, heads * HEAD_DIM), BF16),
        scratch_shapes=[pltpu.VMEM((t, gw), BF16), pltpu.VMEM((t, gw), BF16)],
        compiler_params=_params(("parallel", "parallel", "arbitrary"), 5 * t * gw * 4 + 6 * group * tq * t * 4),
        name="fox_prompt",
    )(q, k, v, c_row)


def _fox_decode_kernel(pt_ref, q_ref, kn_ref, vn_ref, lfn_ref, *rest, n_steps, per_step, heads, t_new, page):
    del pt_ref
    pools = [rest[3 * i:3 * i + 3] for i in range(per_step)]
    o_ref, m_ref, l_ref, acc_ref, carry_ref = rest[3 * per_step:]
    p = pl.program_id(1)
    scale = HEAD_DIM ** -0.5
    r = lax.broadcasted_iota(jnp.int32, (page, page), 0)
    c = lax.broadcasted_iota(jnp.int32, (page, page), 1)
    upper = (r <= c).astype(F32)

    @pl.when(p == 0)
    def _():
        m_ref[...] = jnp.full_like(m_ref, -jnp.inf)
        l_ref[...] = jnp.zeros_like(l_ref)
        acc_ref[...] = jnp.zeros_like(acc_ref)
        carry_ref[...] = jnp.zeros_like(carry_ref)

    head_cols = lambda h: slice(h * HEAD_DIM, (h + 1) * HEAD_DIM)

    def attend(k_heads, v_heads, lf_t, mask):
        c_t = jnp.dot(lf_t, upper, precision=HIGHEST, preferred_element_type=F32) + carry_ref[...]
        carry_ref[...] = c_t[:, page - 1:page]
        s = jnp.stack([_dot_nt(q_ref[:, head_cols(h)], k_heads[h]) for h in range(heads)])
        s = s * scale - c_t[:, None, :]
        if mask is not None:
            s = jnp.where(mask[None], s, -jnp.inf)
        m_old = m_ref[...]
        m_new = jnp.maximum(m_old, jnp.max(s, axis=-1, keepdims=True))
        alpha = jnp.exp(m_old - m_new)
        pr = jnp.exp(s - m_new)
        l_ref[...] = alpha * l_ref[...] + jnp.sum(pr, axis=-1, keepdims=True)
        pv = jnp.stack([_dot(pr[h], v_heads[h]) for h in range(heads)])
        acc_ref[...] = alpha * acc_ref[...] + pv
        m_ref[...] = m_new

    @pl.when(p < n_steps)
    def _():
        for kp_ref, vp_ref, lfp_ref in pools:
            attend([kp_ref[pl.ds(h, page, stride=heads), :] for h in range(heads)],
                   [vp_ref[pl.ds(h, page, stride=heads), :] for h in range(heads)], lfp_ref[...], None)

    @pl.when(p == n_steps)
    def _():
        pad = jnp.zeros((page - t_new, HEAD_DIM), F32)
        qrow = lax.broadcasted_iota(jnp.int32, (t_new, page), 0)
        kcol = lax.broadcasted_iota(jnp.int32, (t_new, page), 1)
        attend([jnp.concatenate([kn_ref[:, head_cols(h)], pad], axis=0) for h in range(heads)],
               [jnp.concatenate([vn_ref[:, head_cols(h)], pad], axis=0) for h in range(heads)],
               lfn_ref[...], kcol <= qrow)
        out = acc_ref[...] / l_ref[...]
        for h in range(heads):
            o_ref[:, head_cols(h)] = out[h].astype(o_ref.dtype)


def _fox_decode(q, k, v, lfn_t, k_pool, v_pool, lf_pool_t, page_table, layer, batch, t_new, heads):
    n_pages = page_table.shape[1]
    page = k_pool.shape[2] // heads
    width = heads * HEAD_DIM
    assert page == LANES and t_new <= page
    per_step = 2 if n_pages % 2 == 0 else 1
    n_steps = n_pages // per_step
    new = pl.BlockSpec((t_new, width), lambda b, p, pt: (b, 0))

    def page_specs(slot):
        at = lambda b, p, pt: (layer, pt[b, jnp.minimum(p, n_steps - 1) * per_step + slot], 0, 0)
        return [pl.BlockSpec((None, None, page * heads, HEAD_DIM), at),
                pl.BlockSpec((None, None, page * heads, HEAD_DIM), at),
                pl.BlockSpec((None, None, heads, page), at)]

    grid_spec = pltpu.PrefetchScalarGridSpec(
        num_scalar_prefetch=1,
        grid=(batch, n_steps + 1),
        in_specs=[new, new, new, pl.BlockSpec((None, heads, page), lambda b, p, pt: (b, 0, 0))]
        + [spec for slot in range(per_step) for spec in page_specs(slot)],
        out_specs=new,
        scratch_shapes=[pltpu.VMEM((heads, t_new, 1), F32), pltpu.VMEM((heads, t_new, 1), F32),
                        pltpu.VMEM((heads, t_new, HEAD_DIM), F32), pltpu.VMEM((heads, 1), F32)],
    )
    return pl.pallas_call(
        functools.partial(_fox_decode_kernel, n_steps=n_steps, per_step=per_step, heads=heads, t_new=t_new,
                          page=page),
        grid_spec=grid_spec,
        out_shape=jax.ShapeDtypeStruct((batch * t_new, width), BF16),
        compiler_params=_params(("parallel", "arbitrary"),
                                4 * per_step * page * width * 4 + 8 * t_new * width * 4 + (4 << 20)),
        name="fox_decode",
    )(page_table, q, k, v, lfn_t, *([k_pool, v_pool, lf_pool_t] * per_step))


def _shortconv_kernel(gb_ref, gc_ref, h_ref, buf_ref, w_ref, o_ref, tail_ref, us_ref, *, width):
    tt = gb_ref.shape[0]

    @pl.when(pl.program_id(2) == 0)
    def _():
        us_ref[0:SUBLANES, :] = buf_ref[...]

    us_ref[SUBLANES:SUBLANES + tt, :] = gc_ref[...] * h_ref[...]
    conv = None
    for k in range(width):
        off = SUBLANES - (width - 1) + k
        term = w_ref[k:k + 1, :] * us_ref[off:off + tt, :]
        conv = term if conv is None else conv + term
    o_ref[...] = (gb_ref[...] * conv).astype(o_ref.dtype)
    last = us_ref[tt:tt + SUBLANES, :]
    tail_ref[...] = last
    us_ref[0:SUBLANES, :] = last


def _shortconv(proj, buf8, w8, batch, t, mix_width, width):
    tt = _tile(t, 512)
    tc = _tile(mix_width, 512)
    nt, nc = t // tt, mix_width // tc
    col = lambda off: pl.BlockSpec((tt, tc), lambda b, c, i: (b * nt + i, off * nc + c))
    return pl.pallas_call(
        functools.partial(_shortconv_kernel, width=width),
        grid=(batch, nc, nt),
        in_specs=[col(0), col(1), col(2),
                  pl.BlockSpec((None, SUBLANES, tc), lambda b, c, i: (b, 0, c)),
                  pl.BlockSpec((SUBLANES, tc), lambda b, c, i: (0, c))],
        out_specs=[col(0), pl.BlockSpec((None, SUBLANES, tc), lambda b, c, i: (b, 0, c))],
        out_shape=[jax.ShapeDtypeStruct((batch * t, mix_width), BF16),
                   jax.ShapeDtypeStruct((batch, SUBLANES, mix_width), F32)],
        scratch_shapes=[pltpu.VMEM((tt + SUBLANES, tc), F32)],
        compiler_params=_params(("parallel", "parallel", "arbitrary"), 12 * tt * tc * 4),
        name="shortconv",
    )(proj, proj, proj, buf8, w8)


def _gdn_conv_kernel(x_ref, buf_ref, w_ref, o_ref, xs_ref, *, width, l2norm, scale):
    tt, tc = x_ref.shape

    @pl.when(pl.program_id(2) == 0)
    def _():
        xs_ref[0:SUBLANES, :] = buf_ref[...]

    xs_ref[SUBLANES:SUBLANES + tt, :] = x_ref[...]
    conv = None
    for k in range(width):
        off = SUBLANES - (width - 1) + k
        term = w_ref[k:k + 1, :] * xs_ref[off:off + tt, :]
        conv = term if conv is None else conv + term
    y = _silu(conv)
    if l2norm:
        for g in range(tc // HEAD_DIM):
            sl = slice(g * HEAD_DIM, (g + 1) * HEAD_DIM)
            seg = y[:, sl]
            seg = seg * lax.rsqrt(jnp.sum(seg * seg, axis=-1, keepdims=True) + L2_EPS)
            o_ref[:, sl] = seg * scale if scale != 1.0 else seg
    else:
        o_ref[...] = y
    xs_ref[0:SUBLANES, :] = xs_ref[tt:tt + SUBLANES, :]


def _gdn_conv(proj, buf8, w8, part, batch, t, mix_width, width, *, l2norm, scale=1.0):
    tt = _tile(t, 512)
    tc = _tile(mix_width, 512)
    nt, nc = t // tt, mix_width // tc
    return pl.pallas_call(
        functools.partial(_gdn_conv_kernel, width=width, l2norm=l2norm, scale=scale),
        grid=(batch, nc, nt),
        in_specs=[pl.BlockSpec((tt, tc), lambda b, c, i: (b * nt + i, part * nc + c)),
                  pl.BlockSpec((None, SUBLANES, tc), lambda b, c, i: (b, 0, part * nc + c)),
                  pl.BlockSpec((SUBLANES, tc), lambda b, c, i: (0, part * nc + c))],
        out_specs=pl.BlockSpec((tt, tc), lambda b, c, i: (b * nt + i, c)),
        out_shape=jax.ShapeDtypeStruct((batch * t, mix_width), F32),
        scratch_shapes=[pltpu.VMEM((tt + SUBLANES, tc), F32)],
        compiler_params=_params(("parallel", "parallel", "arbitrary"), 10 * tt * tc * 4),
        name="gdn_conv",
    )(proj, buf8, w8)


def _gdn_gate_kernel(a_ref, bt_ref, alog_ref, dt_ref, g_ref, beta_ref):
    g_ref[...] = -jnp.exp(alog_ref[...]) * _softplus(a_ref[...] + dt_ref[...])
    beta_ref[...] = jax.nn.sigmoid(bt_ref[...])


def _gdn_gates(tail, a_block, bt_block, a_log_pad, dt_pad):
    m = tail.shape[0]
    tr = _tile(m, 1024)
    blk = lambda off: pl.BlockSpec((tr, LANES), lambda i: (i, off))
    vec = pl.BlockSpec((1, LANES), lambda i: (0, 0))
    return pl.pallas_call(
        _gdn_gate_kernel,
        grid=(m // tr,),
        in_specs=[blk(a_block), blk(bt_block), vec, vec],
        out_specs=[blk(0), blk(0)],
        out_shape=[jax.ShapeDtypeStruct((m, LANES), F32)] * 2,
        compiler_params=_params(("parallel",), 8 * tr * LANES * 4),
        name="gdn_gates",
    )(tail, tail, a_log_pad, dt_pad)


def _unit_lower_inverse_offdiag(mats, ri, ci):
    n = mats[0].shape[0]
    blk = lambda idx, size: jnp.right_shift(idx, size.bit_length() - 1)
    base = SUBLANES
    in_base = blk(ri, base) == blk(ci, base)
    a0 = _each(lambda a: jnp.where(in_base, a, 0.0), mats)
    low = _each(lambda a: -a, a0)
    pw = _each(_dot, a0, a0)
    low = _each(lambda l, p, lp: l + p + lp, low, pw, _each(_dot, low, pw))
    pw = _each(_dot, pw, pw)
    low = _each(lambda l, p, lp: l + p + lp, low, pw, _each(_dot, low, pw))
    s = base
    while s < n:
        sub = (blk(ri, 2 * s) == blk(ci, 2 * s)) & (blk(ri, s) != blk(ci, s))
        off = _each(lambda a: jnp.where(sub, a, 0.0), mats)
        x = _each(lambda o, lo: o + lo, off, _each(_dot, low, off))
        low = _each(lambda l, x_, xl: l - (x_ + xl), low, x, _each(_dot, x, low))
        s *= 2
    return low


def _gdn_intra_kernel(q_ref, k_ref, v_ref, gc_ref, beta_ref, u_ref, w_ref, qk_ref, qg_ref, kg_ref, *, group):
    hg = pl.program_id(1)
    n = q_ref.shape[0]
    ri = lax.broadcasted_iota(jnp.int32, (n, n), 0)
    ci = lax.broadcasted_iota(jnp.int32, (n, n), 1)
    incl = ri >= ci
    cols = [slice(i * HEAD_DIM, (i + 1) * HEAD_DIM) for i in range(group)]
    q = [q_ref[:, sl] for sl in cols]
    k = [k_ref[:, sl] for sl in cols]
    v = [v_ref[:, sl] for sl in cols]
    gcol = [_pick_lane(gc_ref[...], hg * group + i) for i in range(group)]
    bcol = [_pick_lane(beta_ref[...], hg * group + i) for i in range(group)]

    def decay_of(g):
        gmat = jnp.broadcast_to(g, (n, n))
        return jnp.where(incl, jnp.exp(jnp.where(incl, gmat - gmat.T, 0.0)), 0.0)

    decay = _each(decay_of, gcol)
    kb = _each(lambda k_, b: k_ * b, k, bcol)
    a = _each(lambda kk, d: jnp.where(ri > ci, kk * d, 0.0), _each(_dot_nt, kb, k), decay)
    low = _unit_lower_inverse_offdiag(a, ri, ci)
    eg = _each(jnp.exp, gcol)
    rhs = _each(lambda v_, b, kb_, e: jnp.concatenate([v_ * b, kb_ * e], axis=1), v, bcol, kb, eg)
    uw = _each(lambda r, lr: r + lr, rhs, _each(_dot, low, rhs))
    qk = _each(lambda x, d: x * d, _each(_dot_nt, q, k), decay)
    for i, sl in enumerate(cols):
        u_ref[:, sl] = uw[i][:, :HEAD_DIM]
        w_ref[:, sl] = uw[i][:, HEAD_DIM:].astype(w_ref.dtype)
        qk_ref[:, sl] = qk[i].astype(qk_ref.dtype)
        qg_ref[:, sl] = (q[i] * eg[i]).astype(qg_ref.dtype)
        kg_ref[:, sl] = (k[i] * jnp.exp(gcol[i][n - 1:n, :] - gcol[i])).astype(kg_ref.dtype)


def _gdn_intra(qn, kn, vc, gc, beta, batch, t, heads, group):
    n = GDN_CHUNK
    nc = t // n
    wide = pl.BlockSpec((n, group * HEAD_DIM), lambda b, g, c: (b * nc + c, g))
    gate = pl.BlockSpec((n, LANES), lambda b, g, c: (b * nc + c, 0))
    shape = lambda dt: jax.ShapeDtypeStruct((batch * t, heads * HEAD_DIM), dt)
    return pl.pallas_call(
        functools.partial(_gdn_intra_kernel, group=group),
        grid=(batch, heads // group, nc),
        in_specs=[wide, wide, wide, gate, gate],
        out_specs=[wide] * 5,
        out_shape=[shape(F32)] + [shape(BF16)] * 4,
        compiler_params=_params(("parallel", "parallel", "parallel"), 2 * 8 * n * group * HEAD_DIM * 4 + (8 << 20)),
        name="gdn_intra",
    )(qn, kn, vc, gc, beta)


def _gdn_state_kernel(u_ref, w_ref, qk_ref, qg_ref, kg_ref, gc_ref, z_ref, gout_ref, s0_ref, o_ref, sout_ref,
                      s_ref, *, group, n_chunks):
    hg = pl.program_id(1)
    c = pl.program_id(2)
    n = u_ref.shape[0]

    @pl.when(c == 0)
    def _():
        s_ref[...] = s0_ref[...]

    cols = [slice(i * HEAD_DIM, (i + 1) * HEAD_DIM) for i in range(group)]
    of = lambda ref: [ref[:, sl] for sl in cols]
    s = [s_ref[i] for i in range(group)]
    v_new = _each(lambda u, ws: u - ws, of(u_ref), _each(_dot, of(w_ref), s))
    o = _each(lambda x, y: x + y, _each(_dot, of(qg_ref), s), _each(_dot, of(qk_ref), v_new))
    upd = _each(_dot_tn, of(kg_ref), v_new)
    for i, sl in enumerate(cols):
        g_last = _pick_lane(gc_ref[n - 1:n, :], hg * group + i)
        s_ref[i] = s[i] * jnp.exp(g_last) + upd[i]
        on = o[i] * lax.rsqrt(jnp.mean(o[i] * o[i], axis=-1, keepdims=True) + RMS_EPS) * gout_ref[...]
        o_ref[:, sl] = (on * _silu(z_ref[:, sl])).astype(o_ref.dtype)

    @pl.when(c == n_chunks - 1)
    def _():
        sout_ref[...] = s_ref[...]


def _gdn_state(u, w, qk, qg, kg, gc, z_arr, z_block0, g_out, s0, batch, t, heads, group):
    n = GDN_CHUNK
    nc = t // n
    gw = group * HEAD_DIM
    wide = pl.BlockSpec((n, gw), lambda b, g, c: (b * nc + c, g))
    state = pl.BlockSpec((None, group, HEAD_DIM, HEAD_DIM), lambda b, g, c: (b, g, 0, 0))
    return pl.pallas_call(
        functools.partial(_gdn_state_kernel, group=group, n_chunks=nc),
        grid=(batch, heads // group, nc),
        in_specs=[wide] * 5 + [pl.BlockSpec((n, LANES), lambda b, g, c: (b * nc + c, 0)),
                               pl.BlockSpec((n, gw), lambda b, g, c: (b * nc + c, z_block0 + g)),
                               pl.BlockSpec((1, HEAD_DIM), lambda b, g, c: (0, 0)), state],
        out_specs=[wide, state],
        out_shape=[jax.ShapeDtypeStruct((batch * t, heads * HEAD_DIM), BF16),
                   jax.ShapeDtypeStruct((batch, heads, HEAD_DIM, HEAD_DIM), F32)],
        scratch_shapes=[pltpu.VMEM((group, HEAD_DIM, HEAD_DIM), F32)],
        compiler_params=_params(("parallel", "parallel", "arbitrary"), 2 * 8 * n * gw * 4 + 6 * group * HEAD_DIM * HEAD_DIM * 4 + (4 << 20)),
        name="gdn_state",
    )(u, w, qk, qg, kg, gc, z_arr, g_out.reshape(1, HEAD_DIM), s0)


def _pad_lanes(v, width=LANES):
    return jnp.pad(v, [(0, 0)] * (v.ndim - 1) + [(0, width - v.shape[-1])])


def _pad_rows8(a, rows_axis):
    pad = [(0, 0)] * a.ndim
    pad[rows_axis] = (SUBLANES - a.shape[rows_axis], 0)
    return jnp.pad(a, pad)


def _in_weight(w_stack):
    if w_stack.shape[2] % LANES:
        return jnp.swapaxes(w_stack, 1, 2), True
    return w_stack, False


def _tail_weight(w, transposed, j, main, parts, mem_width):
    n_axis = 0 if transposed else 1
    total = w.shape[1 + n_axis]
    take = (lambda a, b: w[j, a:b, :]) if transposed else (lambda a, b: w[j, :, a:b])
    pad = [(0, 0), (0, 0)]
    cols = [take(total - mem_width, total)]
    off = main
    for width in parts:
        pad[n_axis] = (0, LANES - width)
        cols.append(jnp.pad(take(off, off + width), pad))
        off += width
    return jnp.concatenate(cols, axis=n_axis)


def _tail_proj(x, w_tail, transposed):
    n = w_tail.shape[0 if transposed else 1]
    return _matmul_groups(x, w_tail, transposed=transposed, tn=256 if n % 256 == 0 else LANES)


def _fox_mixer(x, dims, w_in_fox, b_fox_f, j, caches, page_table, heads, mem_width):
    mix_width = heads * HEAD_DIM
    main = 3 * mix_width
    w_in, w_t = _in_weight(w_in_fox)
    w_tail = _tail_weight(w_in, w_t, j, main, [heads], mem_width)
    gate_block = mem_width // LANES
    bias = _pad_lanes(b_fox_f[j].reshape(1, heads))
    out = {}
    qkv = [_matmul_groups(x, w_in, layer=j, col0=part * mix_width, n=mix_width, transposed=w_t) for part in range(3)]
    tails = _tail_proj(x, w_tail, w_t)
    for name, (batch, t) in dims.items():
        q, k, v = (part[name] for part in qkv)
        tail = tails[name]
        logf = _fox_logf(tail, gate_block, bias)
        if name == "prompt":
            c_tok = _cumsum_rows(logf, batch, t, carry=True)
            c_row = jnp.transpose(c_tok.reshape(batch, t, LANES)[:, :, :heads], (0, 2, 1))[:, :, None, :]
            mix = _fox_prompt(q, k, v, c_row, batch, t, heads)
        else:
            k_pool, v_pool, lf_pool_t = caches
            lfn_t = jnp.transpose(logf.reshape(batch, t, LANES)[:, :, :heads], (0, 2, 1))
            lfn_t = _pad_lanes(lfn_t, lf_pool_t.shape[-1])
            mix = _fox_decode(q, k, v, lfn_t, k_pool, v_pool, lf_pool_t, page_table, j, batch, t, heads)
        shp = (batch, t, heads, HEAD_DIM)
        out[name] = dict(mix=mix, tail=tail, k=k.reshape(shp), v=v.reshape(shp),
                         logf=logf[:, :heads].reshape(batch, t, heads))
    return out


def _conv_mixer(x, dims, w_in_conv, w_conv, j, state_conv, heads, mem_width):
    mix_width = heads * HEAD_DIM
    main = 3 * mix_width
    width = w_conv.shape[1]
    w8 = jnp.pad(w_conv[j], ((0, SUBLANES - width), (0, 0)))
    out = {}
    w_in, w_t = _in_weight(w_in_conv)
    projs = _matmul_groups(x, w_in, layer=j, col0=0, n=main, transposed=w_t)
    tails = _tail_proj(x, _tail_weight(w_in, w_t, j, main, [], mem_width), w_t)
    for name, (batch, t) in dims.items():
        proj, tail = projs[name], tails[name]
        if name == "prompt":
            buf8 = jnp.zeros((batch, SUBLANES, mix_width), F32)
        else:
            buf8 = _pad_rows8(state_conv[j], 1)
        mix, last8 = _shortconv(proj, buf8, w8, batch, t, mix_width, width)
        out[name] = dict(mix=mix, tail=tail, buf=last8[:, SUBLANES - (width - 1):, :])
    return out


def _gdn_mixer(x, dims, w_in_gdn, w_gdn_conv, gdn_a_log, gdn_dt_bias, g_gdn_out, j, state_gdn_conv, state_gdn_s,
               heads, mem_width):
    mix_width = heads * HEAD_DIM
    main = 4 * mix_width
    width = w_gdn_conv.shape[1]
    w_in, w_t = _in_weight(w_in_gdn)
    w_tail = _tail_weight(w_in, w_t, j, main, [heads, heads], mem_width)
    a_block = mem_width // LANES
    w8 = jnp.pad(w_gdn_conv[j], ((0, SUBLANES - width), (0, 0)))
    a_log = _pad_lanes(gdn_a_log[j].reshape(1, heads))
    dt_bias = _pad_lanes(gdn_dt_bias[j].reshape(1, heads))
    group = next(g for g in (8, 4, 2, 1) if heads % g == 0)
    out = {}
    projs = _matmul_groups(x, w_in, layer=j, col0=0, n=main, transposed=w_t)
    tails = _tail_proj(x, w_tail, w_t)
    for name, (batch, t) in dims.items():
        proj, tail = projs[name], tails[name]
        if name == "prompt":
            buf = jnp.zeros((batch, width - 1, 3 * mix_width), F32)
            s0 = jnp.zeros((batch, heads, HEAD_DIM, HEAD_DIM), F32)
        else:
            buf, s0 = state_gdn_conv[j], state_gdn_s[j]
        buf8 = _pad_rows8(buf, 1)
        conv = functools.partial(_gdn_conv, proj, buf8, w8, batch=batch, t=t, mix_width=mix_width, width=width)
        qn = conv(part=0, l2norm=True, scale=HEAD_DIM ** -0.5)
        kn = conv(part=1, l2norm=True)
        vc = conv(part=2, l2norm=False)
        g, beta = _gdn_gates(tail, a_block, a_block + 1, a_log, dt_bias)
        z_arr, z_block0 = proj, 3 * mix_width // (group * HEAD_DIM)
        tp = -(-t // GDN_CHUNK) * GDN_CHUNK
        if tp != t:
            pad = lambda a_: jnp.pad(a_.reshape(batch, t, -1), ((0, 0), (0, tp - t), (0, 0))).reshape(batch * tp, -1)
            qn, kn, vc, g, beta = (pad(a_) for a_ in (qn, kn, vc, g, beta))
            z_arr, z_block0 = pad(proj[:, 3 * mix_width:main]), 0
        gc = _cumsum_rows(g, batch, tp, carry=False)
        u, w, qk, qg, kg = _gdn_intra(qn, kn, vc, gc, beta, batch, tp, heads, group)
        mix, s_new = _gdn_state(u, w, qk, qg, kg, gc, z_arr, z_block0, g_gdn_out[j], s0, batch, tp, heads, group)
        if tp != t:
            mix = mix.reshape(batch, tp, mix_width)[:, :t].reshape(batch * t, mix_width)
        keep = min(t, width - 1)
        newest = proj.reshape(batch, t, -1)[:, t - keep:, :3 * mix_width]
        out[name] = dict(mix=mix, tail=tail, conv=jnp.concatenate([buf, newest], axis=1)[:, keep:], s=s_new)
    return out


def kernel(x_prompt, x_sample, cache_fox_k, cache_fox_v, cache_fox_logf, cache_mem_k, cache_mem_v, state_conv, state_gdn_conv, state_gdn_s, page_table, mem_prompt, g_mix_pre, g_mix_post, g_mlp_pre, g_mlp_post, g_mem, w_mem_kv, w_out, w_mlp_up, w_mlp_down, w_in_fox, b_fox_f, w_in_conv, w_conv, w_in_gdn, w_gdn_conv, gdn_a_log, gdn_dt_bias, g_gdn_out):
    bp, tp_, d = x_prompt.shape
    bs, ts, _ = x_sample.shape
    depth = g_mix_pre.shape[0]
    mem_tokens, mem_heads, mem_hd = cache_mem_k.shape[2:]
    mem_width = mem_heads * mem_hd
    mix_width = d - mem_width
    heads = mix_width // HEAD_DIM
    n_fox, n_pool, page = cache_fox_k.shape[:3]

    y = {"prompt": x_prompt.reshape(bp * tp_, d), "sample": x_sample.reshape(bs * ts, d)}
    dims = {"prompt": (bp, tp_), "sample": (bs, ts)}
    x = {n: _norm_cast(y[n], g_mix_pre[0]) for n in y}
    mem2d = mem_prompt.reshape(bp * mem_tokens, d)
    caches = (cache_fox_k.reshape(n_fox, n_pool, page * heads, HEAD_DIM),
              cache_fox_v.reshape(n_fox, n_pool, page * heads, HEAD_DIM),
              jnp.transpose(cache_fox_logf, (0, 1, 3, 2)))
    mem_k_s = cache_mem_k.reshape(depth, bs, mem_tokens, mem_width)
    mem_v_s = cache_mem_v.reshape(depth, bs, mem_tokens, mem_width)

    res = {k_: [] for k_ in ("fox_k_p", "fox_v_p", "fox_lf_p", "fox_k_s", "fox_v_s", "fox_lf_s", "conv_p", "conv_s",
                             "gconv_p", "gconv_s", "gs_p", "gs_s", "mem_k", "mem_v")}
    for i in range(depth):
        kind, j = i % 3, i // 3
        kv = _matmul(_norm_cast(mem2d, g_mem[i]), w_mem_kv, layer=i)
        res["mem_k"].append(kv[:, :mem_width].reshape(bp, mem_tokens, mem_heads, mem_hd))
        res["mem_v"].append(kv[:, mem_width:].reshape(bp, mem_tokens, mem_heads, mem_hd))
        if kind == 0:
            mixed = _fox_mixer(x, dims, w_in_fox, b_fox_f, j, caches, page_table, heads, mem_width)
            for n, s in (("prompt", "p"), ("sample", "s")):
                res["fox_k_" + s].append(mixed[n]["k"])
                res["fox_v_" + s].append(mixed[n]["v"])
                res["fox_lf_" + s].append(mixed[n]["logf"])
        elif kind == 1:
            mixed = _conv_mixer(x, dims, w_in_conv, w_conv, j, state_conv, heads, mem_width)
            res["conv_p"].append(mixed["prompt"]["buf"])
            res["conv_s"].append(mixed["sample"]["buf"])
        else:
            mixed = _gdn_mixer(x, dims, w_in_gdn, w_gdn_conv, gdn_a_log, gdn_dt_bias, g_gdn_out, j, state_gdn_conv,
                               state_gdn_s, heads, mem_width)
            for n, s in (("prompt", "p"), ("sample", "s")):
                res["gconv_" + s].append(mixed[n]["conv"])
                res["gs_" + s].append(mixed[n]["s"])
        g_next = g_mix_pre[i + 1] if i + 1 < depth else g_mlp_pre[i]
        cat = {}
        for n, (b, t) in dims.items():
            if n == "prompt":
                mem = _mem_attn(mixed[n]["tail"], kv, kv, lambda bb, part: (bb, part), b, t, mem_heads, mem_width,
                                mem_tokens)
            else:
                mem = _mem_attn(mixed[n]["tail"], mem_k_s, mem_v_s, lambda bb, part: (i, bb, 0, 0), b, t, mem_heads,
                                mem_width, mem_tokens)
            cat[n] = (mixed[n]["mix"], mem)
        delta = _matmul_groups(cat, w_out, layer=i)
        x2 = {}
        for n in dims:
            y[n], x2[n] = _resid_norm(delta[n], y[n], g_mix_post[i], g_mlp_pre[i])
        hid = _matmul_groups(x2, w_mlp_up, layer=i, out_dtype=BF16, act="relu2")
        delta = _matmul_groups(hid, w_mlp_down, layer=i, tn=1024, tk=2048)
        for n in dims:
            y[n], x[n] = _resid_norm(delta[n], y[n], g_mlp_post[i], g_next)

    st = lambda k_: jnp.stack(res[k_])
    return (y["prompt"].reshape(bp, tp_, d), y["sample"].reshape(bs, ts, d),
            st("fox_k_p"), st("fox_v_p"), st("fox_lf_p"), st("conv_p"), st("gconv_p"), st("gs_p"),
            st("mem_k"), st("mem_v"),
            st("fox_k_s"), st("fox_v_s"), st("fox_lf_s"), st("conv_s"), st("gconv_s"), st("gs_s"))
```

```python
import functools

import jax
import jax.numpy as jnp
from jax import lax
from jax.experimental import pallas as pl
from jax.experimental.pallas import tpu as pltpu

F32 = jnp.float32
BF16 = jnp.bfloat16
RMS_EPS = 1e-6
L2_EPS = 1e-6
HEAD_DIM = 128
LANES = 128
SUBLANES = 8
GDN_CHUNK = 128
V7X_VMEM_BYTES = 64 * 1024 * 1024
VMEM_CAP = V7X_VMEM_BYTES - 6 * 1024 * 1024
HIGHEST = lax.Precision.HIGHEST
NT_DIMS = (((1,), (1,)), ((), ()))
TN_DIMS = (((0,), (0,)), ((), ()))


def _params(semantics, vmem_bytes):
    limit = int(min(VMEM_CAP, max(16 * 1024 * 1024, vmem_bytes * 5 // 4 + (2 << 20))))
    return pltpu.CompilerParams(dimension_semantics=semantics, vmem_limit_bytes=limit)


def _tile(dim, pref, align=LANES):
    if dim <= pref:
        return dim
    for cand in range(pref - pref % align, 0, -align):
        if dim % cand == 0:
            return cand
    raise ValueError((dim, pref, align))


def _dot(a, b):
    return jnp.dot(a.astype(BF16), b.astype(BF16), preferred_element_type=F32)


def _dot_nt(a, b):
    return lax.dot_general(a.astype(BF16), b.astype(BF16), NT_DIMS, preferred_element_type=F32)


def _dot_tn(a, b):
    return lax.dot_general(a.astype(BF16), b.astype(BF16), TN_DIMS, preferred_element_type=F32)


def _each(fn, *columns):
    return [fn(*args) for args in zip(*columns)]


def _log_sigmoid(x):
    return jnp.minimum(x, 0.0) - jnp.log1p(jnp.exp(-jnp.abs(x)))


def _softplus(x):
    return jnp.maximum(x, 0.0) + jnp.log1p(jnp.exp(-jnp.abs(x)))


def _silu(x):
    return x * jax.nn.sigmoid(x)


def _pick_lane(blk, idx):
    lane = lax.broadcasted_iota(jnp.int32, blk.shape, 1)
    return jnp.sum(jnp.where(lane == idx, blk, 0.0), axis=-1, keepdims=True)


def _norm_cast_kernel(y_ref, g_ref, o_ref):
    y = y_ref[...]
    inv = lax.rsqrt(jnp.mean(y * y, axis=-1, keepdims=True) + RMS_EPS)
    o_ref[...] = (y * inv * g_ref[...]).astype(o_ref.dtype)


def _norm_cast(y, g):
    m, d = y.shape
    tr = _tile(m, 256)
    return pl.pallas_call(
        _norm_cast_kernel,
        grid=(m // tr,),
        in_specs=[pl.BlockSpec((tr, d), lambda i: (i, 0)), pl.BlockSpec((1, d), lambda i: (0, 0))],
        out_specs=pl.BlockSpec((tr, d), lambda i: (i, 0)),
        out_shape=jax.ShapeDtypeStruct((m, d), BF16),
        compiler_params=_params(("parallel",), 2 * tr * d * 6),
        name="norm_cast",
    )(y, g.reshape(1, d))


def _resid_norm_kernel(d_ref, y_ref, gp_ref, gn_ref, yo_ref, xo_ref):
    d = d_ref[...]
    inv = lax.rsqrt(jnp.mean(d * d, axis=-1, keepdims=True) + RMS_EPS)
    yn = y_ref[...] + d * inv * gp_ref[...]
    yo_ref[...] = yn
    inv2 = lax.rsqrt(jnp.mean(yn * yn, axis=-1, keepdims=True) + RMS_EPS)
    xo_ref[...] = (yn * inv2 * gn_ref[...]).astype(xo_ref.dtype)


def _resid_norm(delta, y, g_post, g_next):
    m, d = y.shape
    tr = _tile(m, 256)
    row = pl.BlockSpec((tr, d), lambda i: (i, 0))
    vec = pl.BlockSpec((1, d), lambda i: (0, 0))
    return pl.pallas_call(
        _resid_norm_kernel,
        grid=(m // tr,),
        in_specs=[row, row, vec, vec],
        out_specs=[row, row],
        out_shape=[jax.ShapeDtypeStruct((m, d), F32), jax.ShapeDtypeStruct((m, d), BF16)],
        compiler_params=_params(("parallel",), 2 * tr * d * 14),
        name="resid_norm",
    )(delta, y, g_post.reshape(1, d), g_next.reshape(1, d))


def _mm_kernel(*refs, nk, act, rider, transposed, parts):
    x_refs, refs = refs[:parts], refs[parts:]
    if rider:
        xr_refs, (w_ref, o_ref, or_ref) = refs[:parts], refs[parts:]
    else:
        w_ref, o_ref = refs
    k = pl.program_id(2)
    dims = NT_DIMS if transposed else (((1,), (0,)), ((), ()))

    def product(lhs_refs, out_ref):
        lhs = lhs_refs[0][...] if parts == 1 else jnp.concatenate([r[...] for r in lhs_refs], axis=1)
        part = lax.dot_general(lhs, w_ref[...].astype(BF16), dims, preferred_element_type=F32)
        if nk == 1:
            if act == "relu2":
                part = jnp.square(jnp.maximum(part, 0.0))
            out_ref[...] = part.astype(out_ref.dtype)
        else:
            @pl.when(k == 0)
            def _():
                out_ref[...] = jnp.zeros_like(out_ref)

            out_ref[...] += part

    product(x_refs, o_ref)
    if rider:
        i = pl.program_id(0)

        @pl.when(i == 0)
        def _():
            product(xr_refs, or_ref)

        @pl.when(i > 0)
        def _():
            or_ref[...] = jnp.zeros_like(or_ref)


def _matmul(x, w, *, rider=None, layer=None, col0=0, n=None, out_dtype=F32, act=None, transposed=False,
            tm=1024, tn=512, tk=4096):
    xs = x if isinstance(x, (tuple, list)) else (x,)
    riders = () if rider is None else (rider if isinstance(rider, (tuple, list)) else (rider,))
    m, kdim = xs[0].shape[0], sum(part.shape[1] for part in xs)
    n_axis, k_axis = (-2, -1) if transposed else (-1, -2)
    n = w.shape[n_axis] - col0 if n is None else n
    tm, tn, tk = _tile(m, tm), _tile(n, tn), _tile(kdim, tk)
    assert col0 % tn == 0 and w.shape[k_axis] == kdim
    nk = kdim // tk
    assert nk == 1 or (act is None and out_dtype == F32 and len(xs) == 1)
    cb = col0 // tn
    widths = [tk] if len(xs) == 1 else [part.shape[1] for part in xs]
    lead = () if layer is None else (None,)
    at = (lambda *idx: idx) if layer is None else (lambda *idx: (layer,) + idx)
    if transposed:
        w_spec = pl.BlockSpec(lead + (tn, tk), lambda i, j, k: at(cb + j, k))
    else:
        w_spec = pl.BlockSpec(lead + (tk, tn), lambda i, j, k: at(k, cb + j))
    osz = jnp.dtype(out_dtype).itemsize
    in_specs = [pl.BlockSpec((tm, width), lambda i, j, k: (i, k)) for width in widths]
    out_specs = [pl.BlockSpec((tm, tn), lambda i, j, k: (i, j))]
    out_shape = [jax.ShapeDtypeStruct((m, n), out_dtype)]
    operands = list(xs)
    mr = 0
    if riders:
        mr = riders[0].shape[0]
        spare = min(1, m // tm - 1)
        in_specs += [pl.BlockSpec((mr, width), lambda i, j, k: (0, k)) for width in widths]
        out_specs.append(pl.BlockSpec((None, mr, tn), lambda i, j, k: (jnp.minimum(i, spare), 0, j)))
        out_shape.append(jax.ShapeDtypeStruct((1 + spare, mr, n), out_dtype))
        operands += list(riders)
    vmem = 2 * ((tm + mr) * tk * 2 + tk * tn * 4 + (tm + mr) * tn * osz) + tk * tn * 2 + 2 * tm * tn * 4
    outs = pl.pallas_call(
        functools.partial(_mm_kernel, nk=nk, act=act, rider=bool(riders), transposed=transposed, parts=len(xs)),
        grid=(m // tm, n // tn, nk),
        in_specs=in_specs + [w_spec],
        out_specs=out_specs,
        out_shape=out_shape,
        compiler_params=_params(("parallel", "parallel", "arbitrary"), vmem),
        name="matmul",
    )(*operands, w)
    return outs[0] if not riders else (outs[0], outs[1][0])


def _matmul_groups(x, w, **kw):
    out, out_r = _matmul(x["prompt"], w, rider=x["sample"], **kw)
    return {"prompt": out, "sample": out_r}


def _mem_attn_kernel(q_ref, k_ref, v_ref, o_ref, *, heads):
    dh = q_ref.shape[-1] // heads
    scale = dh ** -0.5
    cols = [slice(hh * dh, (hh + 1) * dh) for hh in range(heads)]
    of = lambda ref: [ref[:, sl] for sl in cols]
    s = _each(lambda q, k: _dot_nt(q, k) * scale, of(q_ref), of(k_ref))
    p = _each(lambda s_: jnp.exp(s_ - jnp.max(s_, axis=-1, keepdims=True)), s)
    o = _each(lambda p_, v: _dot(p_, v) / jnp.sum(p_, axis=-1, keepdims=True), p, of(v_ref))
    for sl, o_ in zip(cols, o):
        o_ref[:, sl] = o_.astype(o_ref.dtype)


def _mem_attn(tail, k_arr, v_arr, kv_index, batch, t, heads, mem_width, mt):
    tt = _tile(t, 512)
    nt = t // tt
    lead = (None,) * (k_arr.ndim - 2)
    k_spec = pl.BlockSpec(lead + (mt, mem_width), lambda b, i: kv_index(b, 0))
    v_spec = pl.BlockSpec(lead + (mt, mem_width), lambda b, i: kv_index(b, 1))
    return pl.pallas_call(
        functools.partial(_mem_attn_kernel, heads=heads),
        grid=(batch, nt),
        in_specs=[pl.BlockSpec((tt, mem_width), lambda b, i: (b * nt + i, 0)), k_spec, v_spec],
        out_specs=pl.BlockSpec((tt, mem_width), lambda b, i: (b * nt + i, 0)),
        out_shape=jax.ShapeDtypeStruct((batch * t, mem_width), BF16),
        compiler_params=_params(("parallel", "parallel"), 2 * (tt * mem_width * 6 + 2 * mt * mem_width * 4) + 4 * tt * mt * 4),
        name="mem_attn",
    )(tail, k_arr, v_arr)


def _logf_kernel(f_ref, b_ref, o_ref):
    o_ref[...] = _log_sigmoid(f_ref[...] + b_ref[...])


def _fox_logf(tail, gate_block, bias_pad):
    m = tail.shape[0]
    tr = _tile(m, 1024)
    return pl.pallas_call(
        _logf_kernel,
        grid=(m // tr,),
        in_specs=[pl.BlockSpec((tr, LANES), lambda i: (i, gate_block)), pl.BlockSpec((1, LANES), lambda i: (0, 0))],
        out_specs=pl.BlockSpec((tr, LANES), lambda i: (i, 0)),
        out_shape=jax.ShapeDtypeStruct((m, LANES), F32),
        compiler_params=_params(("parallel",), 4 * tr * LANES * 4),
        name="fox_logf",
    )(tail, bias_pad)


def _cumsum_kernel(x_ref, o_ref, carry_ref, *, carry):
    tr = x_ref.shape[0]
    r = lax.broadcasted_iota(jnp.int32, (tr, tr), 0)
    c = lax.broadcasted_iota(jnp.int32, (tr, tr), 1)
    lower = (r >= c).astype(F32)
    cs = jnp.dot(lower, x_ref[...], precision=HIGHEST, preferred_element_type=F32)
    if carry:
        @pl.when(pl.program_id(1) == 0)
        def _():
            carry_ref[...] = jnp.zeros_like(carry_ref)

        cs = cs + carry_ref[...]
        carry_ref[...] = cs[tr - 1:tr, :]
    o_ref[...] = cs


def _cumsum_rows(x, batch, t, *, carry):
    tr = LANES
    nt = t // tr
    return pl.pallas_call(
        functools.partial(_cumsum_kernel, carry=carry),
        grid=(batch, nt),
        in_specs=[pl.BlockSpec((tr, LANES), lambda b, i: (b * nt + i, 0))],
        out_specs=pl.BlockSpec((tr, LANES), lambda b, i: (b * nt + i, 0)),
        out_shape=jax.ShapeDtypeStruct(x.shape, F32),
        scratch_shapes=[pltpu.VMEM((1, LANES), F32)],
        compiler_params=_params(("parallel", "arbitrary"), 8 * tr * LANES * 4),
        name="cumsum_rows",
    )(x)


def _fox_prompt_kernel(q_ref, k_ref, v_ref, crow_ref, o_ref, kb_ref, vb_ref, *, n_q, group):
    qi = pl.program_id(2)
    tq = q_ref.shape[0]
    scale = HEAD_DIM ** -0.5
    cols = [slice(i * HEAD_DIM, (i + 1) * HEAD_DIM) for i in range(group)]

    @pl.when(qi == 0)
    def _():
        kb_ref[...] = k_ref[...].astype(BF16)
        vb_ref[...] = v_ref[...].astype(BF16)

    def attend(extent):
        row = lax.broadcasted_iota(jnp.int32, (tq, extent), 0) + (extent - tq)
        col = lax.broadcasted_iota(jnp.int32, (tq, extent), 1)
        s = [_dot_nt(q_ref[:, sl], kb_ref[0:extent, sl]) for sl in cols]
        s = [jnp.where(col <= row, s[i] * scale - crow_ref[i, :, 0:extent], -jnp.inf) for i in range(group)]
        p = _each(lambda s_: jnp.exp(s_ - jnp.max(s_, axis=-1, keepdims=True)), s)
        pv = [_dot(p[i], vb_ref[0:extent, cols[i]]) for i in range(group)]
        for i in range(group):
            o_ref[:, cols[i]] = (pv[i] / jnp.sum(p[i], axis=-1, keepdims=True)).astype(o_ref.dtype)

    for tile in range(n_q):
        pl.when(qi == tile)(functools.partial(attend, (tile + 1) * tq))


def _fox_prompt(q, k, v, c_row, batch, t, heads):
    tq = _tile(t, 256)
    nq = t // tq
    group = next(g for g in (4, 2, 1) if heads % g == 0)
    gw = group * HEAD_DIM
    blk = pl.BlockSpec((tq, gw), lambda b, g, i: (b * nq + i, g))
    full = pl.BlockSpec((t, gw), lambda b, g, i: (b, g))
    return pl.pallas_call(
        functools.partial(_fox_prompt_kernel, n_q=nq, group=group),
        grid=(batch, heads // group, nq),
        in_specs=[blk, full, full, pl.BlockSpec((None, group, 1, t), lambda b, g, i: (b, g, 0, 0))],
        out_specs=blk,
        out_shape=jax.ShapeDtypeStruct((batch * t, heads * HEAD_DIM), BF16),
        scratch_shapes=[pltpu.VMEM((t, gw), BF16), pltpu.VMEM((t, gw), BF16)],
        compiler_params=_params(("parallel", "parallel", "arbitrary"), 5 * t * gw * 4 + 6 * group * tq * t * 4),
        name="fox_prompt",
    )(q, k, v, c_row)


def _fox_decode_kernel(pt_ref, q_ref, kn_ref, vn_ref, lfn_ref, *rest, n_steps, per_step, heads, t_new, page):
    del pt_ref
    pools = [rest[3 * i:3 * i + 3] for i in range(per_step)]
    o_ref, m_ref, l_ref, acc_ref, carry_ref = rest[3 * per_step:]
    p = pl.program_id(1)
    scale = HEAD_DIM ** -0.5
    r = lax.broadcasted_iota(jnp.int32, (page, page), 0)
    c = lax.broadcasted_iota(jnp.int32, (page, page), 1)
    upper = (r <= c).astype(F32)

    @pl.when(p == 0)
    def _():
        m_ref[...] = jnp.full_like(m_ref, -jnp.inf)
        l_ref[...] = jnp.zeros_like(l_ref)
        acc_ref[...] = jnp.zeros_like(acc_ref)
        carry_ref[...] = jnp.zeros_like(carry_ref)

    head_cols = lambda h: slice(h * HEAD_DIM, (h + 1) * HEAD_DIM)

    def attend(k_heads, v_heads, lf_t, mask):
        c_t = jnp.dot(lf_t, upper, precision=HIGHEST, preferred_element_type=F32) + carry_ref[...]
        carry_ref[...] = c_t[:, page - 1:page]
        s = jnp.stack([_dot_nt(q_ref[:, head_cols(h)], k_heads[h]) for h in range(heads)])
        s = s * scale - c_t[:, None, :]
        if mask is not None:
            s = jnp.where(mask[None], s, -jnp.inf)
        m_old = m_ref[...]
        m_new = jnp.maximum(m_old, jnp.max(s, axis=-1, keepdims=True))
        alpha = jnp.exp(m_old - m_new)
        pr = jnp.exp(s - m_new)
        l_ref[...] = alpha * l_ref[...] + jnp.sum(pr, axis=-1, keepdims=True)
        pv = jnp.stack([_dot(pr[h], v_heads[h]) for h in range(heads)])
        acc_ref[...] = alpha * acc_ref[...] + pv
        m_ref[...] = m_new

    @pl.when(p < n_steps)
    def _():
        for kp_ref, vp_ref, lfp_ref in pools:
            attend([kp_ref[pl.ds(h, page, stride=heads), :] for h in range(heads)],
                   [vp_ref[pl.ds(h, page, stride=heads), :] for h in range(heads)], lfp_ref[...], None)

    @pl.when(p == n_steps)
    def _():
        pad = jnp.zeros((page - t_new, HEAD_DIM), F32)
        qrow = lax.broadcasted_iota(jnp.int32, (t_new, page), 0)
        kcol = lax.broadcasted_iota(jnp.int32, (t_new, page), 1)
        attend([jnp.concatenate([kn_ref[:, head_cols(h)], pad], axis=0) for h in range(heads)],
               [jnp.concatenate([vn_ref[:, head_cols(h)], pad], axis=0) for h in range(heads)],
               lfn_ref[...], kcol <= qrow)
        out = acc_ref[...] / l_ref[...]
        for h in range(heads):
            o_ref[:, head_cols(h)] = out[h].astype(o_ref.dtype)


def _fox_decode(q, k, v, lfn_t, k_pool, v_pool, lf_pool_t, page_table, layer, batch, t_new, heads):
    n_pages = page_table.shape[1]
    page = k_pool.shape[2] // heads
    width = heads * HEAD_DIM
    assert page == LANES and t_new <= page
    per_step = next(c for c in (4, 2, 1) if n_pages % c == 0)
    n_steps = n_pages // per_step
    new = pl.BlockSpec((t_new, width), lambda b, p, pt: (b, 0))

    def page_specs(slot):
        at = lambda b, p, pt: (layer, pt[b, jnp.minimum(p, n_steps - 1) * per_step + slot], 0, 0)
        return [pl.BlockSpec((None, None, page * heads, HEAD_DIM), at),
                pl.BlockSpec((None, None, page * heads, HEAD_DIM), at),
                pl.BlockSpec((None, None, heads, page), at)]

    grid_spec = pltpu.PrefetchScalarGridSpec(
        num_scalar_prefetch=1,
        grid=(batch, n_steps + 1),
        in_specs=[new, new, new, pl.BlockSpec((None, heads, page), lambda b, p, pt: (b, 0, 0))]
        + [spec for slot in range(per_step) for spec in page_specs(slot)],
        out_specs=new,
        scratch_shapes=[pltpu.VMEM((heads, t_new, 1), F32), pltpu.VMEM((heads, t_new, 1), F32),
                        pltpu.VMEM((heads, t_new, HEAD_DIM), F32), pltpu.VMEM((heads, 1), F32)],
    )
    return pl.pallas_call(
        functools.partial(_fox_decode_kernel, n_steps=n_steps, per_step=per_step, heads=heads, t_new=t_new,
                          page=page),
        grid_spec=grid_spec,
        out_shape=jax.ShapeDtypeStruct((batch * t_new, width), BF16),
        compiler_params=_params(("parallel", "arbitrary"),
                                4 * per_step * page * width * 4 + 8 * t_new * width * 4 + (4 << 20)),
        name="fox_decode",
    )(page_table, q, k, v, lfn_t, *([k_pool, v_pool, lf_pool_t] * per_step))


def _shortconv_kernel(gb_ref, gc_ref, h_ref, buf_ref, w_ref, o_ref, tail_ref, us_ref, *, width):
    tt = gb_ref.shape[0]

    @pl.when(pl.program_id(2) == 0)
    def _():
        us_ref[0:SUBLANES, :] = buf_ref[...]

    us_ref[SUBLANES:SUBLANES + tt, :] = gc_ref[...] * h_ref[...]
    conv = None
    for k in range(width):
        off = SUBLANES - (width - 1) + k
        term = w_ref[k:k + 1, :] * us_ref[off:off + tt, :]
        conv = term if conv is None else conv + term
    o_ref[...] = (gb_ref[...] * conv).astype(o_ref.dtype)
    last = us_ref[tt:tt + SUBLANES, :]
    tail_ref[...] = last
    us_ref[0:SUBLANES, :] = last


def _shortconv(proj, buf8, w8, batch, t, mix_width, width):
    tt = _tile(t, 1024)
    tc = _tile(mix_width, 512)
    nt, nc = t // tt, mix_width // tc
    col = lambda off: pl.BlockSpec((tt, tc), lambda b, c, i: (b * nt + i, off * nc + c))
    return pl.pallas_call(
        functools.partial(_shortconv_kernel, width=width),
        grid=(batch, nc, nt),
        in_specs=[col(0), col(1), col(2),
                  pl.BlockSpec((None, SUBLANES, tc), lambda b, c, i: (b, 0, c)),
                  pl.BlockSpec((SUBLANES, tc), lambda b, c, i: (0, c))],
        out_specs=[col(0), pl.BlockSpec((None, SUBLANES, tc), lambda b, c, i: (b, 0, c))],
        out_shape=[jax.ShapeDtypeStruct((batch * t, mix_width), BF16),
                   jax.ShapeDtypeStruct((batch, SUBLANES, mix_width), F32)],
        scratch_shapes=[pltpu.VMEM((tt + SUBLANES, tc), F32)],
        compiler_params=_params(("parallel", "parallel", "arbitrary"), 12 * tt * tc * 4),
        name="shortconv",
    )(proj, proj, proj, buf8, w8)


def _gdn_conv_kernel(x_ref, buf_ref, w_ref, o_ref, xs_ref, *, width, l2norm, scale):
    tt, tc = x_ref.shape

    @pl.when(pl.program_id(2) == 0)
    def _():
        xs_ref[0:SUBLANES, :] = buf_ref[...]

    xs_ref[SUBLANES:SUBLANES + tt, :] = x_ref[...]
    conv = None
    for k in range(width):
        off = SUBLANES - (width - 1) + k
        term = w_ref[k:k + 1, :] * xs_ref[off:off + tt, :]
        conv = term if conv is None else conv + term
    y = _silu(conv)
    if l2norm:
        for g in range(tc // HEAD_DIM):
            sl = slice(g * HEAD_DIM, (g + 1) * HEAD_DIM)
            seg = y[:, sl]
            seg = seg * lax.rsqrt(jnp.sum(seg * seg, axis=-1, keepdims=True) + L2_EPS)
            o_ref[:, sl] = seg * scale if scale != 1.0 else seg
    else:
        o_ref[...] = y
    xs_ref[0:SUBLANES, :] = xs_ref[tt:tt + SUBLANES, :]


def _gdn_conv(proj, buf8, w8, part, batch, t, mix_width, width, *, l2norm, scale=1.0):
    tt = _tile(t, 1024)
    tc = _tile(mix_width, 512)
    nt, nc = t // tt, mix_width // tc
    return pl.pallas_call(
        functools.partial(_gdn_conv_kernel, width=width, l2norm=l2norm, scale=scale),
        grid=(batch, nc, nt),
        in_specs=[pl.BlockSpec((tt, tc), lambda b, c, i: (b * nt + i, part * nc + c)),
                  pl.BlockSpec((None, SUBLANES, tc), lambda b, c, i: (b, 0, part * nc + c)),
                  pl.BlockSpec((SUBLANES, tc), lambda b, c, i: (0, part * nc + c))],
        out_specs=pl.BlockSpec((tt, tc), lambda b, c, i: (b * nt + i, c)),
        out_shape=jax.ShapeDtypeStruct((batch * t, mix_width), F32),
        scratch_shapes=[pltpu.VMEM((tt + SUBLANES, tc), F32)],
        compiler_params=_params(("parallel", "parallel", "arbitrary"), 10 * tt * tc * 4),
        name="gdn_conv",
    )(proj, buf8, w8)


def _gdn_gate_kernel(a_ref, bt_ref, alog_ref, dt_ref, g_ref, beta_ref):
    g_ref[...] = -jnp.exp(alog_ref[...]) * _softplus(a_ref[...] + dt_ref[...])
    beta_ref[...] = jax.nn.sigmoid(bt_ref[...])


def _gdn_gates(tail, a_block, bt_block, a_log_pad, dt_pad):
    m = tail.shape[0]
    tr = _tile(m, 1024)
    blk = lambda off: pl.BlockSpec((tr, LANES), lambda i: (i, off))
    vec = pl.BlockSpec((1, LANES), lambda i: (0, 0))
    return pl.pallas_call(
        _gdn_gate_kernel,
        grid=(m // tr,),
        in_specs=[blk(a_block), blk(bt_block), vec, vec],
        out_specs=[blk(0), blk(0)],
        out_shape=[jax.ShapeDtypeStruct((m, LANES), F32)] * 2,
        compiler_params=_params(("parallel",), 8 * tr * LANES * 4),
        name="gdn_gates",
    )(tail, tail, a_log_pad, dt_pad)


def _unit_lower_inverse_offdiag(mats, ri, ci):
    n = mats[0].shape[0]
    blk = lambda idx, size: jnp.right_shift(idx, size.bit_length() - 1)
    base = SUBLANES
    in_base = blk(ri, base) == blk(ci, base)
    a0 = _each(lambda a: jnp.where(in_base, a, 0.0), mats)
    low = _each(lambda a: -a, a0)
    pw = _each(_dot, a0, a0)
    low = _each(lambda l, p, lp: l + p + lp, low, pw, _each(_dot, low, pw))
    pw = _each(_dot, pw, pw)
    low = _each(lambda l, p, lp: l + p + lp, low, pw, _each(_dot, low, pw))
    s = base
    while s < n:
        sub = (blk(ri, 2 * s) == blk(ci, 2 * s)) & (blk(ri, s) != blk(ci, s))
        off = _each(lambda a: jnp.where(sub, a, 0.0), mats)
        x = _each(lambda o, lo: o + lo, off, _each(_dot, low, off))
        low = _each(lambda l, x_, xl: l - (x_ + xl), low, x, _each(_dot, x, low))
        s *= 2
    return low


def _gdn_intra_kernel(q_ref, k_ref, v_ref, gc_ref, beta_ref, u_ref, w_ref, qk_ref, qg_ref, kg_ref, *, group):
    hg = pl.program_id(1)
    n = q_ref.shape[0]
    ri = lax.broadcasted_iota(jnp.int32, (n, n), 0)
    ci = lax.broadcasted_iota(jnp.int32, (n, n), 1)
    incl = ri >= ci
    cols = [slice(i * HEAD_DIM, (i + 1) * HEAD_DIM) for i in range(group)]
    q = [q_ref[:, sl] for sl in cols]
    k = [k_ref[:, sl] for sl in cols]
    v = [v_ref[:, sl] for sl in cols]
    gcol = [_pick_lane(gc_ref[...], hg * group + i) for i in range(group)]
    bcol = [_pick_lane(beta_ref[...], hg * group + i) for i in range(group)]

    def decay_of(g):
        gmat = jnp.broadcast_to(g, (n, n))
        return jnp.where(incl, jnp.exp(jnp.where(incl, gmat - gmat.T, 0.0)), 0.0)

    decay = _each(decay_of, gcol)
    kb = _each(lambda k_, b: k_ * b, k, bcol)
    a = _each(lambda kk, d: jnp.where(ri > ci, kk * d, 0.0), _each(_dot_nt, kb, k), decay)
    low = _unit_lower_inverse_offdiag(a, ri, ci)
    eg = _each(jnp.exp, gcol)
    rhs = _each(lambda v_, b, kb_, e: jnp.concatenate([v_ * b, kb_ * e], axis=1), v, bcol, kb, eg)
    uw = _each(lambda r, lr: r + lr, rhs, _each(_dot, low, rhs))
    qk = _each(lambda x, d: x * d, _each(_dot_nt, q, k), decay)
    for i, sl in enumerate(cols):
        u_ref[:, sl] = uw[i][:, :HEAD_DIM]
        w_ref[:, sl] = uw[i][:, HEAD_DIM:].astype(w_ref.dtype)
        qk_ref[:, sl] = qk[i].astype(qk_ref.dtype)
        qg_ref[:, sl] = (q[i] * eg[i]).astype(qg_ref.dtype)
        kg_ref[:, sl] = (k[i] * jnp.exp(gcol[i][n - 1:n, :] - gcol[i])).astype(kg_ref.dtype)


def _gdn_intra(qn, kn, vc, gc, beta, batch, t, heads, group):
    n = GDN_CHUNK
    nc = t // n
    wide = pl.BlockSpec((n, group * HEAD_DIM), lambda b, g, c: (b * nc + c, g))
    gate = pl.BlockSpec((n, LANES), lambda b, g, c: (b * nc + c, 0))
    shape = lambda dt: jax.ShapeDtypeStruct((batch * t, heads * HEAD_DIM), dt)
    return pl.pallas_call(
        functools.partial(_gdn_intra_kernel, group=group),
        grid=(batch, heads // group, nc),
        in_specs=[wide, wide, wide, gate, gate],
        out_specs=[wide] * 5,
        out_shape=[shape(F32)] + [shape(BF16)] * 4,
        compiler_params=_params(("parallel", "parallel", "parallel"), 2 * 8 * n * group * HEAD_DIM * 4 + (8 << 20)),
        name="gdn_intra",
    )(qn, kn, vc, gc, beta)


def _gdn_state_kernel(u_ref, w_ref, qk_ref, qg_ref, kg_ref, gc_ref, z_ref, gout_ref, s0_ref, o_ref, sout_ref,
                      s_ref, *, group, n_chunks):
    hg = pl.program_id(1)
    c = pl.program_id(2)
    n = u_ref.shape[0]

    @pl.when(c == 0)
    def _():
        s_ref[...] = s0_ref[...]

    cols = [slice(i * HEAD_DIM, (i + 1) * HEAD_DIM) for i in range(group)]
    of = lambda ref: [ref[:, sl] for sl in cols]
    s = [s_ref[i] for i in range(group)]
    v_new = _each(lambda u, ws: u - ws, of(u_ref), _each(_dot, of(w_ref), s))
    o = _each(lambda x, y: x + y, _each(_dot, of(qg_ref), s), _each(_dot, of(qk_ref), v_new))
    upd = _each(_dot_tn, of(kg_ref), v_new)
    for i, sl in enumerate(cols):
        g_last = _pick_lane(gc_ref[n - 1:n, :], hg * group + i)
        s_ref[i] = s[i] * jnp.exp(g_last) + upd[i]
        on = o[i] * lax.rsqrt(jnp.mean(o[i] * o[i], axis=-1, keepdims=True) + RMS_EPS) * gout_ref[...]
        o_ref[:, sl] = (on * _silu(z_ref[:, sl])).astype(o_ref.dtype)

    @pl.when(c == n_chunks - 1)
    def _():
        sout_ref[...] = s_ref[...]


def _gdn_state(u, w, qk, qg, kg, gc, z_arr, z_block0, g_out, s0, batch, t, heads, group):
    n = GDN_CHUNK
    nc = t // n
    gw = group * HEAD_DIM
    wide = pl.BlockSpec((n, gw), lambda b, g, c: (b * nc + c, g))
    state = pl.BlockSpec((None, group, HEAD_DIM, HEAD_DIM), lambda b, g, c: (b, g, 0, 0))
    return pl.pallas_call(
        functools.partial(_gdn_state_kernel, group=group, n_chunks=nc),
        grid=(batch, heads // group, nc),
        in_specs=[wide] * 5 + [pl.BlockSpec((n, LANES), lambda b, g, c: (b * nc + c, 0)),
                               pl.BlockSpec((n, gw), lambda b, g, c: (b * nc + c, z_block0 + g)),
                               pl.BlockSpec((1, HEAD_DIM), lambda b, g, c: (0, 0)), state],
        out_specs=[wide, state],
        out_shape=[jax.ShapeDtypeStruct((batch * t, heads * HEAD_DIM), BF16),
                   jax.ShapeDtypeStruct((batch, heads, HEAD_DIM, HEAD_DIM), F32)],
        scratch_shapes=[pltpu.VMEM((group, HEAD_DIM, HEAD_DIM), F32)],
        compiler_params=_params(("parallel", "parallel", "arbitrary"), 2 * 8 * n * gw * 4 + 6 * group * HEAD_DIM * HEAD_DIM * 4 + (4 << 20)),
        name="gdn_state",
    )(u, w, qk, qg, kg, gc, z_arr, g_out.reshape(1, HEAD_DIM), s0)


def _pad_lanes(v, width=LANES):
    return jnp.pad(v, [(0, 0)] * (v.ndim - 1) + [(0, width - v.shape[-1])])


def _pad_rows8(a, rows_axis):
    pad = [(0, 0)] * a.ndim
    pad[rows_axis] = (SUBLANES - a.shape[rows_axis], 0)
    return jnp.pad(a, pad)


def _in_weight(w_stack):
    if w_stack.shape[2] % LANES:
        return jnp.swapaxes(w_stack, 1, 2), True
    return w_stack, False


def _tail_weight(w, transposed, j, main, parts, mem_width):
    n_axis = 0 if transposed else 1
    total = w.shape[1 + n_axis]
    take = (lambda a, b: w[j, a:b, :]) if transposed else (lambda a, b: w[j, :, a:b])
    pad = [(0, 0), (0, 0)]
    cols = [take(total - mem_width, total)]
    off = main
    for width in parts:
        pad[n_axis] = (0, LANES - width)
        cols.append(jnp.pad(take(off, off + width), pad))
        off += width
    return jnp.concatenate(cols, axis=n_axis)


def _tail_proj(x, w_tail, transposed):
    return _matmul_groups(x, w_tail, transposed=transposed)


def _fox_mixer(x, dims, w_in_fox, b_fox_f, j, caches, page_table, heads, mem_width):
    mix_width = heads * HEAD_DIM
    main = 3 * mix_width
    w_in, w_t = _in_weight(w_in_fox)
    w_tail = _tail_weight(w_in, w_t, j, main, [heads], mem_width)
    gate_block = mem_width // LANES
    bias = _pad_lanes(b_fox_f[j].reshape(1, heads))
    out = {}
    qkv = [_matmul_groups(x, w_in, layer=j, col0=part * mix_width, n=mix_width, transposed=w_t) for part in range(3)]
    tails = _tail_proj(x, w_tail, w_t)
    for name, (batch, t) in dims.items():
        q, k, v = (part[name] for part in qkv)
        tail = tails[name]
        logf = _fox_logf(tail, gate_block, bias)
        if name == "prompt":
            c_tok = _cumsum_rows(logf, batch, t, carry=True)
            c_row = jnp.transpose(c_tok.reshape(batch, t, LANES)[:, :, :heads], (0, 2, 1))[:, :, None, :]
            mix = _fox_prompt(q, k, v, c_row, batch, t, heads)
        else:
            k_pool, v_pool, lf_pool_t = caches
            lfn_t = jnp.transpose(logf.reshape(batch, t, LANES)[:, :, :heads], (0, 2, 1))
            lfn_t = _pad_lanes(lfn_t, lf_pool_t.shape[-1])
            mix = _fox_decode(q, k, v, lfn_t, k_pool, v_pool, lf_pool_t, page_table, j, batch, t, heads)
        shp = (batch, t, heads, HEAD_DIM)
        out[name] = dict(mix=mix, tail=tail, k=k.reshape(shp), v=v.reshape(shp),
                         logf=logf[:, :heads].reshape(batch, t, heads))
    return out


def _conv_mixer(x, dims, w_in_conv, w_conv, j, state_conv, heads, mem_width):
    mix_width = heads * HEAD_DIM
    main = 3 * mix_width
    width = w_conv.shape[1]
    w8 = jnp.pad(w_conv[j], ((0, SUBLANES - width), (0, 0)))
    out = {}
    w_in, w_t = _in_weight(w_in_conv)
    projs = _matmul_groups(x, w_in, layer=j, col0=0, n=main, transposed=w_t)
    tails = _tail_proj(x, _tail_weight(w_in, w_t, j, main, [], mem_width), w_t)
    for name, (batch, t) in dims.items():
        proj, tail = projs[name], tails[name]
        if name == "prompt":
            buf8 = jnp.zeros((batch, SUBLANES, mix_width), F32)
        else:
            buf8 = _pad_rows8(state_conv[j], 1)
        mix, last8 = _shortconv(proj, buf8, w8, batch, t, mix_width, width)
        out[name] = dict(mix=mix, tail=tail, buf=last8[:, SUBLANES - (width - 1):, :])
    return out


def _gdn_mixer(x, dims, w_in_gdn, w_gdn_conv, gdn_a_log, gdn_dt_bias, g_gdn_out, j, state_gdn_conv, state_gdn_s,
               heads, mem_width):
    mix_width = heads * HEAD_DIM
    main = 4 * mix_width
    width = w_gdn_conv.shape[1]
    w_in, w_t = _in_weight(w_in_gdn)
    w_tail = _tail_weight(w_in, w_t, j, main, [heads, heads], mem_width)
    a_block = mem_width // LANES
    w8 = jnp.pad(w_gdn_conv[j], ((0, SUBLANES - width), (0, 0)))
    a_log = _pad_lanes(gdn_a_log[j].reshape(1, heads))
    dt_bias = _pad_lanes(gdn_dt_bias[j].reshape(1, heads))
    group = next(g for g in (8, 4, 2, 1) if heads % g == 0)
    out = {}
    projs = _matmul_groups(x, w_in, layer=j, col0=0, n=main, transposed=w_t)
    tails = _tail_proj(x, w_tail, w_t)
    for name, (batch, t) in dims.items():
        proj, tail = projs[name], tails[name]
        if name == "prompt":
            buf = jnp.zeros((batch, width - 1, 3 * mix_width), F32)
            s0 = jnp.zeros((batch, heads, HEAD_DIM, HEAD_DIM), F32)
        else:
            buf, s0 = state_gdn_conv[j], state_gdn_s[j]
        buf8 = _pad_rows8(buf, 1)
        conv = functools.partial(_gdn_conv, proj, buf8, w8, batch=batch, t=t, mix_width=mix_width, width=width)
        qn = conv(part=0, l2norm=True, scale=HEAD_DIM ** -0.5)
        kn = conv(part=1, l2norm=True)
        vc = conv(part=2, l2norm=False)
        g, beta = _gdn_gates(tail, a_block, a_block + 1, a_log, dt_bias)
        z_arr, z_block0 = proj, 3 * mix_width // (group * HEAD_DIM)
        tp = -(-t // GDN_CHUNK) * GDN_CHUNK
        if tp != t:
            pad = lambda a_: jnp.pad(a_.reshape(batch, t, -1), ((0, 0), (0, tp - t), (0, 0))).reshape(batch * tp, -1)
            qn, kn, vc, g, beta = (pad(a_) for a_ in (qn, kn, vc, g, beta))
            z_arr, z_block0 = pad(proj[:, 3 * mix_width:main]), 0
        gc = _cumsum_rows(g, batch, tp, carry=False)
        u, w, qk, qg, kg = _gdn_intra(qn, kn, vc, gc, beta, batch, tp, heads, group)
        mix, s_new = _gdn_state(u, w, qk, qg, kg, gc, z_arr, z_block0, g_gdn_out[j], s0, batch, tp, heads, group)
        if tp != t:
            mix = mix.reshape(batch, tp, mix_width)[:, :t].reshape(batch * t, mix_width)
        keep = min(t, width - 1)
        newest = proj.reshape(batch, t, -1)[:, t - keep:, :3 * mix_width]
        out[name] = dict(mix=mix, tail=tail, conv=jnp.concatenate([buf, newest], axis=1)[:, keep:], s=s_new)
    return out


def kernel(x_prompt, x_sample, cache_fox_k, cache_fox_v, cache_fox_logf, cache_mem_k, cache_mem_v, state_conv, state_gdn_conv, state_gdn_s, page_table, mem_prompt, g_mix_pre, g_mix_post, g_mlp_pre, g_mlp_post, g_mem, w_mem_kv, w_out, w_mlp_up, w_mlp_down, w_in_fox, b_fox_f, w_in_conv, w_conv, w_in_gdn, w_gdn_conv, gdn_a_log, gdn_dt_bias, g_gdn_out):
    bp, tp_, d = x_prompt.shape
    bs, ts, _ = x_sample.shape
    depth = g_mix_pre.shape[0]
    mem_tokens, mem_heads, mem_hd = cache_mem_k.shape[2:]
    mem_width = mem_heads * mem_hd
    mix_width = d - mem_width
    heads = mix_width // HEAD_DIM
    n_fox, n_pool, page = cache_fox_k.shape[:3]

    y = {"prompt": x_prompt.reshape(bp * tp_, d), "sample": x_sample.reshape(bs * ts, d)}
    dims = {"prompt": (bp, tp_), "sample": (bs, ts)}
    x = {n: _norm_cast(y[n], g_mix_pre[0]) for n in y}
    mem2d = mem_prompt.reshape(bp * mem_tokens, d)
    caches = (cache_fox_k.reshape(n_fox, n_pool, page * heads, HEAD_DIM),
              cache_fox_v.reshape(n_fox, n_pool, page * heads, HEAD_DIM),
              jnp.transpose(cache_fox_logf, (0, 1, 3, 2)))
    mem_k_s = cache_mem_k.reshape(depth, bs, mem_tokens, mem_width)
    mem_v_s = cache_mem_v.reshape(depth, bs, mem_tokens, mem_width)

    res = {k_: [] for k_ in ("fox_k_p", "fox_v_p", "fox_lf_p", "fox_k_s", "fox_v_s", "fox_lf_s", "conv_p", "conv_s",
                             "gconv_p", "gconv_s", "gs_p", "gs_s", "mem_k", "mem_v")}
    for i in range(depth):
        kind, j = i % 3, i // 3
        kv = _matmul(_norm_cast(mem2d, g_mem[i]), w_mem_kv, layer=i)
        res["mem_k"].append(kv[:, :mem_width].reshape(bp, mem_tokens, mem_heads, mem_hd))
        res["mem_v"].append(kv[:, mem_width:].reshape(bp, mem_tokens, mem_heads, mem_hd))
        if kind == 0:
            mixed = _fox_mixer(x, dims, w_in_fox, b_fox_f, j, caches, page_table, heads, mem_width)
            for n, s in (("prompt", "p"), ("sample", "s")):
                res["fox_k_" + s].append(mixed[n]["k"])
                res["fox_v_" + s].append(mixed[n]["v"])
                res["fox_lf_" + s].append(mixed[n]["logf"])
        elif kind == 1:
            mixed = _conv_mixer(x, dims, w_in_conv, w_conv, j, state_conv, heads, mem_width)
            res["conv_p"].append(mixed["prompt"]["buf"])
            res["conv_s"].append(mixed["sample"]["buf"])
        else:
            mixed = _gdn_mixer(x, dims, w_in_gdn, w_gdn_conv, gdn_a_log, gdn_dt_bias, g_gdn_out, j, state_gdn_conv,
                               state_gdn_s, heads, mem_width)
            for n, s in (("prompt", "p"), ("sample", "s")):
                res["gconv_" + s].append(mixed[n]["conv"])
                res["gs_" + s].append(mixed[n]["s"])
        g_next = g_mix_pre[i + 1] if i + 1 < depth else g_mlp_pre[i]
        cat = {}
        for n, (b, t) in dims.items():
            if n == "prompt":
                mem = _mem_attn(mixed[n]["tail"], kv, kv, lambda bb, part: (bb, part), b, t, mem_heads, mem_width,
                                mem_tokens)
            else:
                mem = _mem_attn(mixed[n]["tail"], mem_k_s, mem_v_s, lambda bb, part: (i, bb, 0, 0), b, t, mem_heads,
                                mem_width, mem_tokens)
            cat[n] = (mixed[n]["mix"], mem)
        delta = _matmul_groups(cat, w_out, layer=i)
        x2 = {}
        for n in dims:
            y[n], x2[n] = _resid_norm(delta[n], y[n], g_mix_post[i], g_mlp_pre[i])
        hid = _matmul_groups(x2, w_mlp_up, layer=i, out_dtype=BF16, act="relu2")
        delta = _matmul_groups(hid, w_mlp_down, layer=i, tn=1024, tk=2048)
        for n in dims:
            y[n], x[n] = _resid_norm(delta[n], y[n], g_mlp_post[i], g_next)

    st = lambda k_: jnp.stack(res[k_])
    return (y["prompt"].reshape(bp, tp_, d), y["sample"].reshape(bs, ts, d),
            st("fox_k_p"), st("fox_v_p"), st("fox_lf_p"), st("conv_p"), st("gconv_p"), st("gs_p"),
            st("mem_k"), st("mem_v"),
            st("fox_k_s"), st("fox_v_s"), st("fox_lf_s"), st("conv_s"), st("gconv_s"), st("gs_s"))
```

```python
import functools

import jax
import jax.numpy as jnp
from jax import lax
from jax.experimental import pallas as pl
from jax.experimental.pallas import tpu as pltpu

F32 = jnp.float32
BF16 = jnp.bfloat16
RMS_EPS = 1e-6
L2_EPS = 1e-6
HEAD_DIM = 128
LANES = 128
SUBLANES = 8
GDN_CHUNK = 128
V7X_VMEM_BYTES = 64 * 1024 * 1024
VMEM_CAP = V7X_VMEM_BYTES - 6 * 1024 * 1024
HIGHEST = lax.Precision.HIGHEST
NT_DIMS = (((1,), (1,)), ((), ()))
TN_DIMS = (((0,), (0,)), ((), ()))


def _params(semantics, vmem_bytes):
    limit = int(min(VMEM_CAP, max(16 * 1024 * 1024, vmem_bytes * 5 // 4 + (2 << 20))))
    return pltpu.CompilerParams(dimension_semantics=semantics, vmem_limit_bytes=limit)


def _tile(dim, pref, align=LANES):
    if dim <= pref:
        return dim
    for cand in range(pref - pref % align, 0, -align):
        if dim % cand == 0:
            return cand
    raise ValueError((dim, pref, align))


def _dot(a, b):
    return jnp.dot(a.astype(BF16), b.astype(BF16), preferred_element_type=F32)


def _dot_nt(a, b):
    return lax.dot_general(a.astype(BF16), b.astype(BF16), NT_DIMS, preferred_element_type=F32)


def _dot_tn(a, b):
    return lax.dot_general(a.astype(BF16), b.astype(BF16), TN_DIMS, preferred_element_type=F32)


def _each(fn, *columns):
    return [fn(*args) for args in zip(*columns)]


def _log_sigmoid(x):
    return jnp.minimum(x, 0.0) - jnp.log1p(jnp.exp(-jnp.abs(x)))


def _softplus(x):
    return jnp.maximum(x, 0.0) + jnp.log1p(jnp.exp(-jnp.abs(x)))


def _silu(x):
    return x * jax.nn.sigmoid(x)


def _pick_lane(blk, idx):
    lane = lax.broadcasted_iota(jnp.int32, blk.shape, 1)
    return jnp.sum(jnp.where(lane == idx, blk, 0.0), axis=-1, keepdims=True)


def _norm_cast_kernel(y_ref, g_ref, o_ref):
    y = y_ref[...]
    inv = lax.rsqrt(jnp.mean(y * y, axis=-1, keepdims=True) + RMS_EPS)
    o_ref[...] = (y * inv * g_ref[...]).astype(o_ref.dtype)


def _norm_cast(y, g):
    m, d = y.shape
    tr = _tile(m, 256)
    return pl.pallas_call(
        _norm_cast_kernel,
        grid=(m // tr,),
        in_specs=[pl.BlockSpec((tr, d), lambda i: (i, 0)), pl.BlockSpec((1, d), lambda i: (0, 0))],
        out_specs=pl.BlockSpec((tr, d), lambda i: (i, 0)),
        out_shape=jax.ShapeDtypeStruct((m, d), BF16),
        compiler_params=_params(("parallel",), 2 * tr * d * 6),
        name="norm_cast",
    )(y, g.reshape(1, d))


def _resid_norm_kernel(d_ref, y_ref, gp_ref, gn_ref, yo_ref, xo_ref):
    d = d_ref[...]
    inv = lax.rsqrt(jnp.mean(d * d, axis=-1, keepdims=True) + RMS_EPS)
    yn = y_ref[...] + d * inv * gp_ref[...]
    yo_ref[...] = yn
    inv2 = lax.rsqrt(jnp.mean(yn * yn, axis=-1, keepdims=True) + RMS_EPS)
    xo_ref[...] = (yn * inv2 * gn_ref[...]).astype(xo_ref.dtype)


def _resid_norm(delta, y, g_post, g_next):
    m, d = y.shape
    tr = _tile(m, 256)
    row = pl.BlockSpec((tr, d), lambda i: (i, 0))
    vec = pl.BlockSpec((1, d), lambda i: (0, 0))
    return pl.pallas_call(
        _resid_norm_kernel,
        grid=(m // tr,),
        in_specs=[row, row, vec, vec],
        out_specs=[row, row],
        out_shape=[jax.ShapeDtypeStruct((m, d), F32), jax.ShapeDtypeStruct((m, d), BF16)],
        compiler_params=_params(("parallel",), 2 * tr * d * 14),
        name="resid_norm",
    )(delta, y, g_post.reshape(1, d), g_next.reshape(1, d))


def _mm_kernel(*refs, nk, act, rider, transposed, parts):
    x_refs, refs = refs[:parts], refs[parts:]
    if rider:
        xr_refs, (w_ref, o_ref, or_ref) = refs[:parts], refs[parts:]
    else:
        w_ref, o_ref = refs
    k = pl.program_id(2)
    dims = NT_DIMS if transposed else (((1,), (0,)), ((), ()))

    def product(lhs_refs, out_ref):
        lhs = lhs_refs[0][...] if parts == 1 else jnp.concatenate([r[...] for r in lhs_refs], axis=1)
        part = lax.dot_general(lhs, w_ref[...].astype(BF16), dims, preferred_element_type=F32)
        if nk == 1:
            if act == "relu2":
                part = jnp.square(jnp.maximum(part, 0.0))
            out_ref[...] = part.astype(out_ref.dtype)
        else:
            @pl.when(k == 0)
            def _():
                out_ref[...] = jnp.zeros_like(out_ref)

            out_ref[...] += part

    product(x_refs, o_ref)
    if rider:
        i = pl.program_id(0)

        @pl.when(i == 0)
        def _():
            product(xr_refs, or_ref)

        @pl.when(i > 0)
        def _():
            or_ref[...] = jnp.zeros_like(or_ref)


def _matmul(x, w, *, rider=None, layer=None, col0=0, n=None, out_dtype=F32, act=None, transposed=False,
            tm=1024, tn=512, tk=4096):
    xs = x if isinstance(x, (tuple, list)) else (x,)
    riders = () if rider is None else (rider if isinstance(rider, (tuple, list)) else (rider,))
    m, kdim = xs[0].shape[0], sum(part.shape[1] for part in xs)
    n_axis, k_axis = (-2, -1) if transposed else (-1, -2)
    n = w.shape[n_axis] - col0 if n is None else n
    tm, tn, tk = _tile(m, tm), _tile(n, tn), _tile(kdim, tk)
    assert col0 % tn == 0 and w.shape[k_axis] == kdim
    nk = kdim // tk
    assert nk == 1 or (act is None and out_dtype == F32 and len(xs) == 1)
    cb = col0 // tn
    widths = [tk] if len(xs) == 1 else [part.shape[1] for part in xs]
    lead = () if layer is None else (None,)
    at = (lambda *idx: idx) if layer is None else (lambda *idx: (layer,) + idx)
    if transposed:
        w_spec = pl.BlockSpec(lead + (tn, tk), lambda i, j, k: at(cb + j, k))
    else:
        w_spec = pl.BlockSpec(lead + (tk, tn), lambda i, j, k: at(k, cb + j))
    osz = jnp.dtype(out_dtype).itemsize
    in_specs = [pl.BlockSpec((tm, width), lambda i, j, k: (i, k)) for width in widths]
    out_specs = [pl.BlockSpec((tm, tn), lambda i, j, k: (i, j))]
    out_shape = [jax.ShapeDtypeStruct((m, n), out_dtype)]
    operands = list(xs)
    mr = 0
    if riders:
        mr = riders[0].shape[0]
        spare = min(1, m // tm - 1)
        in_specs += [pl.BlockSpec((mr, width), lambda i, j, k: (0, k)) for width in widths]
        out_specs.append(pl.BlockSpec((None, mr, tn), lambda i, j, k: (jnp.minimum(i, spare), 0, j)))
        out_shape.append(jax.ShapeDtypeStruct((1 + spare, mr, n), out_dtype))
        operands += list(riders)
    vmem = 2 * ((tm + mr) * tk * 2 + tk * tn * 4 + (tm + mr) * tn * osz) + tk * tn * 2 + 2 * tm * tn * 4
    outs = pl.pallas_call(
        functools.partial(_mm_kernel, nk=nk, act=act, rider=bool(riders), transposed=transposed, parts=len(xs)),
        grid=(m // tm, n // tn, nk),
        in_specs=in_specs + [w_spec],
        out_specs=out_specs,
        out_shape=out_shape,
        compiler_params=_params(("parallel", "parallel", "arbitrary"), vmem),
        name="matmul",
    )(*operands, w)
    return outs[0] if not riders else (outs[0], outs[1][0])


def _matmul_groups(x, w, **kw):
    out, out_r = _matmul(x["prompt"], w, rider=x["sample"], **kw)
    return {"prompt": out, "sample": out_r}


def _mem_attn_kernel(q_ref, k_ref, v_ref, o_ref, *, heads):
    dh = q_ref.shape[-1] // heads
    scale = dh ** -0.5
    cols = [slice(hh * dh, (hh + 1) * dh) for hh in range(heads)]
    of = lambda ref: [ref[:, sl] for sl in cols]
    s = _each(lambda q, k: _dot_nt(q, k) * scale, of(q_ref), of(k_ref))
    p = _each(lambda s_: jnp.exp(s_ - jnp.max(s_, axis=-1, keepdims=True)), s)
    o = _each(lambda p_, v: _dot(p_, v) / jnp.sum(p_, axis=-1, keepdims=True), p, of(v_ref))
    for sl, o_ in zip(cols, o):
        o_ref[:, sl] = o_.astype(o_ref.dtype)


def _mem_attn(tail, k_arr, v_arr, kv_index, batch, t, heads, mem_width, mt):
    tt = _tile(t, 512)
    nt = t // tt
    lead = (None,) * (k_arr.ndim - 2)
    k_spec = pl.BlockSpec(lead + (mt, mem_width), lambda b, i: kv_index(b, 0))
    v_spec = pl.BlockSpec(lead + (mt, mem_width), lambda b, i: kv_index(b, 1))
    return pl.pallas_call(
        functools.partial(_mem_attn_kernel, heads=heads),
        grid=(batch, nt),
        in_specs=[pl.BlockSpec((tt, mem_width), lambda b, i: (b * nt + i, 0)), k_spec, v_spec],
        out_specs=pl.BlockSpec((tt, mem_width), lambda b, i: (b * nt + i, 0)),
        out_shape=jax.ShapeDtypeStruct((batch * t, mem_width), BF16),
        compiler_params=_params(("parallel", "parallel"), 2 * (tt * mem_width * 6 + 2 * mt * mem_width * 4) + 4 * tt * mt * 4),
        name="mem_attn",
    )(tail, k_arr, v_arr)


def _logf_kernel(f_ref, b_ref, o_ref):
    o_ref[...] = _log_sigmoid(f_ref[...] + b_ref[...])


def _fox_logf(tail, gate_block, bias_pad):
    m = tail.shape[0]
    tr = _tile(m, 1024)
    return pl.pallas_call(
        _logf_kernel,
        grid=(m // tr,),
        in_specs=[pl.BlockSpec((tr, LANES), lambda i: (i, gate_block)), pl.BlockSpec((1, LANES), lambda i: (0, 0))],
        out_specs=pl.BlockSpec((tr, LANES), lambda i: (i, 0)),
        out_shape=jax.ShapeDtypeStruct((m, LANES), F32),
        compiler_params=_params(("parallel",), 4 * tr * LANES * 4),
        name="fox_logf",
    )(tail, bias_pad)


def _cumsum_kernel(x_ref, o_ref, carry_ref, *, carry):
    tr = x_ref.shape[0]
    r = lax.broadcasted_iota(jnp.int32, (tr, tr), 0)
    c = lax.broadcasted_iota(jnp.int32, (tr, tr), 1)
    lower = (r >= c).astype(F32)
    cs = jnp.dot(lower, x_ref[...], precision=HIGHEST, preferred_element_type=F32)
    if carry:
        @pl.when(pl.program_id(1) == 0)
        def _():
            carry_ref[...] = jnp.zeros_like(carry_ref)

        cs = cs + carry_ref[...]
        carry_ref[...] = cs[tr - 1:tr, :]
    o_ref[...] = cs


def _cumsum_rows(x, batch, t, *, carry):
    tr = LANES
    nt = t // tr
    return pl.pallas_call(
        functools.partial(_cumsum_kernel, carry=carry),
        grid=(batch, nt),
        in_specs=[pl.BlockSpec((tr, LANES), lambda b, i: (b * nt + i, 0))],
        out_specs=pl.BlockSpec((tr, LANES), lambda b, i: (b * nt + i, 0)),
        out_shape=jax.ShapeDtypeStruct(x.shape, F32),
        scratch_shapes=[pltpu.VMEM((1, LANES), F32)],
        compiler_params=_params(("parallel", "arbitrary"), 8 * tr * LANES * 4),
        name="cumsum_rows",
    )(x)


def _fox_prompt_kernel(q_ref, k_ref, v_ref, crow_ref, o_ref, kb_ref, vb_ref, *, n_q, group):
    qi = pl.program_id(2)
    tq = q_ref.shape[0]
    scale = HEAD_DIM ** -0.5
    cols = [slice(i * HEAD_DIM, (i + 1) * HEAD_DIM) for i in range(group)]

    @pl.when(qi == 0)
    def _():
        kb_ref[...] = k_ref[...].astype(BF16)
        vb_ref[...] = v_ref[...].astype(BF16)

    def attend(extent):
        row = lax.broadcasted_iota(jnp.int32, (tq, extent), 0) + (extent - tq)
        col = lax.broadcasted_iota(jnp.int32, (tq, extent), 1)
        s = [_dot_nt(q_ref[:, sl], kb_ref[0:extent, sl]) for sl in cols]
        s = [jnp.where(col <= row, s[i] * scale - crow_ref[i, :, 0:extent], -jnp.inf) for i in range(group)]
        p = _each(lambda s_: jnp.exp(s_ - jnp.max(s_, axis=-1, keepdims=True)), s)
        pv = [_dot(p[i], vb_ref[0:extent, cols[i]]) for i in range(group)]
        for i in range(group):
            o_ref[:, cols[i]] = (pv[i] / jnp.sum(p[i], axis=-1, keepdims=True)).astype(o_ref.dtype)

    for tile in range(n_q):
        pl.when(qi == tile)(functools.partial(attend, (tile + 1) * tq))


def _fox_prompt(q, k, v, c_row, batch, t, heads):
    tq = _tile(t, 256)
    nq = t // tq
    group = next(g for g in (4, 2, 1) if heads % g == 0)
    gw = group * HEAD_DIM
    blk = pl.BlockSpec((tq, gw), lambda b, g, i: (b * nq + i, g))
    full = pl.BlockSpec((t, gw), lambda b, g, i: (b, g))
    return pl.pallas_call(
        functools.partial(_fox_prompt_kernel, n_q=nq, group=group),
        grid=(batch, heads // group, nq),
        in_specs=[blk, full, full, pl.BlockSpec((None, group, 1, t), lambda b, g, i: (b, g, 0, 0))],
        out_specs=blk,
        out_shape=jax.ShapeDtypeStruct((batch * t, heads * HEAD_DIM), BF16),
        scratch_shapes=[pltpu.VMEM((t, gw), BF16), pltpu.VMEM((t, gw), BF16)],
        compiler_params=_params(("parallel", "parallel", "arbitrary"), 5 * t * gw * 4 + 6 * group * tq * t * 4),
        name="fox_prompt",
    )(q, k, v, c_row)


def _fox_decode_kernel(pt_ref, q_ref, kn_ref, vn_ref, lfn_ref, *rest, n_steps, per_step, heads, t_new, page):
    del pt_ref
    pools = [rest[3 * i:3 * i + 3] for i in range(per_step)]
    o_ref, m_ref, l_ref, acc_ref, carry_ref = rest[3 * per_step:]
    p = pl.program_id(1)
    scale = HEAD_DIM ** -0.5
    r = lax.broadcasted_iota(jnp.int32, (page, page), 0)
    c = lax.broadcasted_iota(jnp.int32, (page, page), 1)
    upper = (r <= c).astype(F32)

    @pl.when(p == 0)
    def _():
        m_ref[...] = jnp.full_like(m_ref, -jnp.inf)
        l_ref[...] = jnp.zeros_like(l_ref)
        acc_ref[...] = jnp.zeros_like(acc_ref)
        carry_ref[...] = jnp.zeros_like(carry_ref)

    head_cols = lambda h: slice(h * HEAD_DIM, (h + 1) * HEAD_DIM)

    def attend(k_heads, v_heads, lf_t, mask):
        c_t = jnp.dot(lf_t, upper, precision=HIGHEST, preferred_element_type=F32) + carry_ref[...]
        carry_ref[...] = c_t[:, page - 1:page]
        s = jnp.stack([_dot_nt(q_ref[:, head_cols(h)], k_heads[h]) for h in range(heads)])
        s = s * scale - c_t[:, None, :]
        if mask is not None:
            s = jnp.where(mask[None], s, -jnp.inf)
        m_old = m_ref[...]
        m_new = jnp.maximum(m_old, jnp.max(s, axis=-1, keepdims=True))
        alpha = jnp.exp(m_old - m_new)
        pr = jnp.exp(s - m_new)
        l_ref[...] = alpha * l_ref[...] + jnp.sum(pr, axis=-1, keepdims=True)
        pv = jnp.stack([_dot(pr[h], v_heads[h]) for h in range(heads)])
        acc_ref[...] = alpha * acc_ref[...] + pv
        m_ref[...] = m_new

    @pl.when(p < n_steps)
    def _():
        for kp_ref, vp_ref, lfp_ref in pools:
            attend([kp_ref[pl.ds(h, page, stride=heads), :] for h in range(heads)],
                   [vp_ref[pl.ds(h, page, stride=heads), :] for h in range(heads)], lfp_ref[...], None)

    @pl.when(p == n_steps)
    def _():
        pad = jnp.zeros((page - t_new, HEAD_DIM), F32)
        qrow = lax.broadcasted_iota(jnp.int32, (t_new, page), 0)
        kcol = lax.broadcasted_iota(jnp.int32, (t_new, page), 1)
        attend([jnp.concatenate([kn_ref[:, head_cols(h)], pad], axis=0) for h in range(heads)],
               [jnp.concatenate([vn_ref[:, head_cols(h)], pad], axis=0) for h in range(heads)],
               lfn_ref[...], kcol <= qrow)
        out = acc_ref[...] / l_ref[...]
        for h in range(heads):
            o_ref[:, head_cols(h)] = out[h].astype(o_ref.dtype)


def _fox_decode(q, k, v, lfn_t, k_pool, v_pool, lf_pool_t, page_table, layer, batch, t_new, heads):
    n_pages = page_table.shape[1]
    page = k_pool.shape[2] // heads
    width = heads * HEAD_DIM
    assert page == LANES and t_new <= page
    per_step = next(c for c in (4, 2, 1) if n_pages % c == 0)
    n_steps = n_pages // per_step
    new = pl.BlockSpec((t_new, width), lambda b, p, pt: (b, 0))

    def page_specs(slot):
        at = lambda b, p, pt: (layer, pt[b, jnp.minimum(p, n_steps - 1) * per_step + slot], 0, 0)
        return [pl.BlockSpec((None, None, page * heads, HEAD_DIM), at),
                pl.BlockSpec((None, None, page * heads, HEAD_DIM), at),
                pl.BlockSpec((None, None, heads, page), at)]

    grid_spec = pltpu.PrefetchScalarGridSpec(
        num_scalar_prefetch=1,
        grid=(batch, n_steps + 1),
        in_specs=[new, new, new, pl.BlockSpec((None, heads, page), lambda b, p, pt: (b, 0, 0))]
        + [spec for slot in range(per_step) for spec in page_specs(slot)],
        out_specs=new,
        scratch_shapes=[pltpu.VMEM((heads, t_new, 1), F32), pltpu.VMEM((heads, t_new, 1), F32),
                        pltpu.VMEM((heads, t_new, HEAD_DIM), F32), pltpu.VMEM((heads, 1), F32)],
    )
    return pl.pallas_call(
        functools.partial(_fox_decode_kernel, n_steps=n_steps, per_step=per_step, heads=heads, t_new=t_new,
                          page=page),
        grid_spec=grid_spec,
        out_shape=jax.ShapeDtypeStruct((batch * t_new, width), BF16),
        compiler_params=_params(("parallel", "arbitrary"),
                                4 * per_step * page * width * 4 + 8 * t_new * width * 4 + (4 << 20)),
        name="fox_decode",
    )(page_table, q, k, v, lfn_t, *([k_pool, v_pool, lf_pool_t] * per_step))


def _shortconv_kernel(gb_ref, gc_ref, h_ref, buf_ref, w_ref, o_ref, tail_ref, us_ref, *, width):
    tt = gb_ref.shape[0]

    @pl.when(pl.program_id(2) == 0)
    def _():
        us_ref[0:SUBLANES, :] = buf_ref[...]

    us_ref[SUBLANES:SUBLANES + tt, :] = gc_ref[...] * h_ref[...]
    conv = None
    for k in range(width):
        off = SUBLANES - (width - 1) + k
        term = w_ref[k:k + 1, :] * us_ref[off:off + tt, :]
        conv = term if conv is None else conv + term
    o_ref[...] = (gb_ref[...] * conv).astype(o_ref.dtype)
    last = us_ref[tt:tt + SUBLANES, :]
    tail_ref[...] = last
    us_ref[0:SUBLANES, :] = last


def _shortconv(proj, buf8, w8, batch, t, mix_width, width):
    tt = _tile(t, 1024)
    tc = _tile(mix_width, 512)
    nt, nc = t // tt, mix_width // tc
    col = lambda off: pl.BlockSpec((tt, tc), lambda b, c, i: (b * nt + i, off * nc + c))
    return pl.pallas_call(
        functools.partial(_shortconv_kernel, width=width),
        grid=(batch, nc, nt),
        in_specs=[col(0), col(1), col(2),
                  pl.BlockSpec((None, SUBLANES, tc), lambda b, c, i: (b, 0, c)),
                  pl.BlockSpec((SUBLANES, tc), lambda b, c, i: (0, c))],
        out_specs=[col(0), pl.BlockSpec((None, SUBLANES, tc), lambda b, c, i: (b, 0, c))],
        out_shape=[jax.ShapeDtypeStruct((batch * t, mix_width), BF16),
                   jax.ShapeDtypeStruct((batch, SUBLANES, mix_width), F32)],
        scratch_shapes=[pltpu.VMEM((tt + SUBLANES, tc), F32)],
        compiler_params=_params(("parallel", "parallel", "arbitrary"), 12 * tt * tc * 4),
        name="shortconv",
    )(proj, proj, proj, buf8, w8)


def _gdn_conv_kernel(x_ref, buf_ref, w_ref, o_ref, xs_ref, *, width, l2norm, scale):
    tt, tc = x_ref.shape

    @pl.when(pl.program_id(2) == 0)
    def _():
        xs_ref[0:SUBLANES, :] = buf_ref[...]

    xs_ref[SUBLANES:SUBLANES + tt, :] = x_ref[...]
    conv = None
    for k in range(width):
        off = SUBLANES - (width - 1) + k
        term = w_ref[k:k + 1, :] * xs_ref[off:off + tt, :]
        conv = term if conv is None else conv + term
    y = _silu(conv)
    if l2norm:
        for g in range(tc // HEAD_DIM):
            sl = slice(g * HEAD_DIM, (g + 1) * HEAD_DIM)
            seg = y[:, sl]
            seg = seg * lax.rsqrt(jnp.sum(seg * seg, axis=-1, keepdims=True) + L2_EPS)
            o_ref[:, sl] = seg * scale if scale != 1.0 else seg
    else:
        o_ref[...] = y
    xs_ref[0:SUBLANES, :] = xs_ref[tt:tt + SUBLANES, :]


def _gdn_conv(proj, buf8, w8, part, batch, t, mix_width, width, *, l2norm, scale=1.0):
    tt = _tile(t, 1024)
    tc = _tile(mix_width, 512)
    nt, nc = t // tt, mix_width // tc
    return pl.pallas_call(
        functools.partial(_gdn_conv_kernel, width=width, l2norm=l2norm, scale=scale),
        grid=(batch, nc, nt),
        in_specs=[pl.BlockSpec((tt, tc), lambda b, c, i: (b * nt + i, part * nc + c)),
                  pl.BlockSpec((None, SUBLANES, tc), lambda b, c, i: (b, 0, part * nc + c)),
                  pl.BlockSpec((SUBLANES, tc), lambda b, c, i: (0, part * nc + c))],
        out_specs=pl.BlockSpec((tt, tc), lambda b, c, i: (b * nt + i, c)),
        out_shape=jax.ShapeDtypeStruct((batch * t, mix_width), F32),
        scratch_shapes=[pltpu.VMEM((tt + SUBLANES, tc), F32)],
        compiler_params=_params(("parallel", "parallel", "arbitrary"), 10 * tt * tc * 4),
        name="gdn_conv",
    )(proj, buf8, w8)


def _gdn_gate_kernel(a_ref, bt_ref, alog_ref, dt_ref, g_ref, beta_ref):
    g_ref[...] = -jnp.exp(alog_ref[...]) * _softplus(a_ref[...] + dt_ref[...])
    beta_ref[...] = jax.nn.sigmoid(bt_ref[...])


def _gdn_gates(tail, a_block, bt_block, a_log_pad, dt_pad):
    m = tail.shape[0]
    tr = _tile(m, 1024)
    blk = lambda off: pl.BlockSpec((tr, LANES), lambda i: (i, off))
    vec = pl.BlockSpec((1, LANES), lambda i: (0, 0))
    return pl.pallas_call(
        _gdn_gate_kernel,
        grid=(m // tr,),
        in_specs=[blk(a_block), blk(bt_block), vec, vec],
        out_specs=[blk(0), blk(0)],
        out_shape=[jax.ShapeDtypeStruct((m, LANES), F32)] * 2,
        compiler_params=_params(("parallel",), 8 * tr * LANES * 4),
        name="gdn_gates",
    )(tail, tail, a_log_pad, dt_pad)


def _unit_lower_inverse_offdiag(mats, ri, ci):
    n = mats[0].shape[0]
    blk = lambda idx, size: jnp.right_shift(idx, size.bit_length() - 1)
    base = SUBLANES
    in_base = blk(ri, base) == blk(ci, base)
    a0 = _each(lambda a: jnp.where(in_base, a, 0.0), mats)
    low = _each(lambda a: -a, a0)
    pw = _each(_dot, a0, a0)
    low = _each(lambda l, p, lp: l + p + lp, low, pw, _each(_dot, low, pw))
    pw = _each(_dot, pw, pw)
    low = _each(lambda l, p, lp: l + p + lp, low, pw, _each(_dot, low, pw))
    s = base
    while s < n:
        sub = (blk(ri, 2 * s) == blk(ci, 2 * s)) & (blk(ri, s) != blk(ci, s))
        off = _each(lambda a: jnp.where(sub, a, 0.0), mats)
        x = _each(lambda o, lo: o + lo, off, _each(_dot, low, off))
        low = _each(lambda l, x_, xl: l - (x_ + xl), low, x, _each(_dot, x, low))
        s *= 2
    return low


def _gdn_intra_kernel(q_ref, k_ref, v_ref, gc_ref, beta_ref, u_ref, w_ref, qk_ref, qg_ref, kg_ref, *, group):
    hg = pl.program_id(1)
    n = q_ref.shape[0]
    ri = lax.broadcasted_iota(jnp.int32, (n, n), 0)
    ci = lax.broadcasted_iota(jnp.int32, (n, n), 1)
    incl = ri >= ci
    cols = [slice(i * HEAD_DIM, (i + 1) * HEAD_DIM) for i in range(group)]
    q = [q_ref[:, sl] for sl in cols]
    k = [k_ref[:, sl] for sl in cols]
    v = [v_ref[:, sl] for sl in cols]
    gcol = [_pick_lane(gc_ref[...], hg * group + i) for i in range(group)]
    bcol = [_pick_lane(beta_ref[...], hg * group + i) for i in range(group)]

    def decay_of(g):
        gmat = jnp.broadcast_to(g, (n, n))
        return jnp.where(incl, jnp.exp(jnp.where(incl, gmat - gmat.T, 0.0)), 0.0)

    decay = _each(decay_of, gcol)
    kb = _each(lambda k_, b: k_ * b, k, bcol)
    a = _each(lambda kk, d: jnp.where(ri > ci, kk * d, 0.0), _each(_dot_nt, kb, k), decay)
    low = _unit_lower_inverse_offdiag(a, ri, ci)
    eg = _each(jnp.exp, gcol)
    rhs = _each(lambda v_, b, kb_, e: jnp.concatenate([v_ * b, kb_ * e], axis=1), v, bcol, kb, eg)
    uw = _each(lambda r, lr: r + lr, rhs, _each(_dot, low, rhs))
    qk = _each(lambda x, d: x * d, _each(_dot_nt, q, k), decay)
    for i, sl in enumerate(cols):
        u_ref[:, sl] = uw[i][:, :HEAD_DIM]
        w_ref[:, sl] = uw[i][:, HEAD_DIM:].astype(w_ref.dtype)
        qk_ref[:, sl] = qk[i].astype(qk_ref.dtype)
        qg_ref[:, sl] = (q[i] * eg[i]).astype(qg_ref.dtype)
        kg_ref[:, sl] = (k[i] * jnp.exp(gcol[i][n - 1:n, :] - gcol[i])).astype(kg_ref.dtype)


def _gdn_intra(qn, kn, vc, gc, beta, batch, t, heads, group):
    n = GDN_CHUNK
    nc = t // n
    wide = pl.BlockSpec((n, group * HEAD_DIM), lambda b, g, c: (b * nc + c, g))
    gate = pl.BlockSpec((n, LANES), lambda b, g, c: (b * nc + c, 0))
    shape = lambda dt: jax.ShapeDtypeStruct((batch * t, heads * HEAD_DIM), dt)
    return pl.pallas_call(
        functools.partial(_gdn_intra_kernel, group=group),
        grid=(batch, heads // group, nc),
        in_specs=[wide, wide, wide, gate, gate],
        out_specs=[wide] * 5,
        out_shape=[shape(F32)] + [shape(BF16)] * 4,
        compiler_params=_params(("parallel", "parallel", "parallel"), 2 * 8 * n * group * HEAD_DIM * 4 + (8 << 20)),
        name="gdn_intra",
    )(qn, kn, vc, gc, beta)


def _gdn_state_kernel(u_ref, w_ref, qk_ref, qg_ref, kg_ref, gc_ref, z_ref, gout_ref, s0_ref, o_ref, sout_ref,
                      s_ref, *, group, n_chunks):
    hg = pl.program_id(1)
    c = pl.program_id(2)
    n = u_ref.shape[0]

    @pl.when(c == 0)
    def _():
        s_ref[...] = s0_ref[...]

    cols = [slice(i * HEAD_DIM, (i + 1) * HEAD_DIM) for i in range(group)]
    of = lambda ref: [ref[:, sl] for sl in cols]
    s = [s_ref[i] for i in range(group)]
    v_new = _each(lambda u, ws: u - ws, of(u_ref), _each(_dot, of(w_ref), s))
    o = _each(lambda x, y: x + y, _each(_dot, of(qg_ref), s), _each(_dot, of(qk_ref), v_new))
    upd = _each(_dot_tn, of(kg_ref), v_new)
    for i, sl in enumerate(cols):
        g_last = _pick_lane(gc_ref[n - 1:n, :], hg * group + i)
        s_ref[i] = s[i] * jnp.exp(g_last) + upd[i]
        on = o[i] * lax.rsqrt(jnp.mean(o[i] * o[i], axis=-1, keepdims=True) + RMS_EPS) * gout_ref[...]
        o_ref[:, sl] = (on * _silu(z_ref[:, sl])).astype(o_ref.dtype)

    @pl.when(c == n_chunks - 1)
    def _():
        sout_ref[...] = s_ref[...]


def _gdn_state(u, w, qk, qg, kg, gc, z_arr, z_block0, g_out, s0, batch, t, heads, group):
    n = GDN_CHUNK
    nc = t // n
    gw = group * HEAD_DIM
    wide = pl.BlockSpec((n, gw), lambda b, g, c: (b * nc + c, g))
    state = pl.BlockSpec((None, group, HEAD_DIM, HEAD_DIM), lambda b, g, c: (b, g, 0, 0))
    return pl.pallas_call(
        functools.partial(_gdn_state_kernel, group=group, n_chunks=nc),
        grid=(batch, heads // group, nc),
        in_specs=[wide] * 5 + [pl.BlockSpec((n, LANES), lambda b, g, c: (b * nc + c, 0)),
                               pl.BlockSpec((n, gw), lambda b, g, c: (b * nc + c, z_block0 + g)),
                               pl.BlockSpec((1, HEAD_DIM), lambda b, g, c: (0, 0)), state],
        out_specs=[wide, state],
        out_shape=[jax.ShapeDtypeStruct((batch * t, heads * HEAD_DIM), BF16),
                   jax.ShapeDtypeStruct((batch, heads, HEAD_DIM, HEAD_DIM), F32)],
        scratch_shapes=[pltpu.VMEM((group, HEAD_DIM, HEAD_DIM), F32)],
        compiler_params=_params(("parallel", "parallel", "arbitrary"), 2 * 8 * n * gw * 4 + 6 * group * HEAD_DIM * HEAD_DIM * 4 + (4 << 20)),
        name="gdn_state",
    )(u, w, qk, qg, kg, gc, z_arr, g_out.reshape(1, HEAD_DIM), s0)


def _pad_lanes(v, width=LANES):
    return jnp.pad(v, [(0, 0)] * (v.ndim - 1) + [(0, width - v.shape[-1])])


def _pad_rows8(a, rows_axis):
    pad = [(0, 0)] * a.ndim
    pad[rows_axis] = (SUBLANES - a.shape[rows_axis], 0)
    return jnp.pad(a, pad)


def _in_weight(w_stack):
    if w_stack.shape[2] % LANES:
        return jnp.swapaxes(w_stack, 1, 2), True
    return w_stack, False


def _tail_weight(w, transposed, j, main, parts, mem_width):
    n_axis = 0 if transposed else 1
    total = w.shape[1 + n_axis]
    take = (lambda a, b: w[j, a:b, :]) if transposed else (lambda a, b: w[j, :, a:b])
    pad = [(0, 0), (0, 0)]
    cols = [take(total - mem_width, total)]
    off = main
    for width in parts:
        pad[n_axis] = (0, LANES - width)
        cols.append(jnp.pad(take(off, off + width), pad))
        off += width
    return jnp.concatenate(cols, axis=n_axis)


def _tail_proj(x, w_tail, transposed):
    return _matmul_groups(x, w_tail, transposed=transposed)


def _fox_mixer(x, dims, w_in_fox, b_fox_f, j, caches, page_table, heads, mem_width):
    mix_width = heads * HEAD_DIM
    main = 3 * mix_width
    w_in, w_t = _in_weight(w_in_fox)
    w_tail = _tail_weight(w_in, w_t, j, main, [heads], mem_width)
    gate_block = mem_width // LANES
    bias = _pad_lanes(b_fox_f[j].reshape(1, heads))
    out = {}
    qkv = [_matmul_groups(x, w_in, layer=j, col0=part * mix_width, n=mix_width, transposed=w_t) for part in range(3)]
    tails = _tail_proj(x, w_tail, w_t)
    for name, (batch, t) in dims.items():
        q, k, v = (part[name] for part in qkv)
        tail = tails[name]
        logf = _fox_logf(tail, gate_block, bias)
        if name == "prompt":
            c_tok = _cumsum_rows(logf, batch, t, carry=True)
            c_row = jnp.transpose(c_tok.reshape(batch, t, LANES)[:, :, :heads], (0, 2, 1))[:, :, None, :]
            mix = _fox_prompt(q, k, v, c_row, batch, t, heads)
        else:
            k_pool, v_pool, lf_pool_t = caches
            lfn_t = jnp.transpose(logf.reshape(batch, t, LANES)[:, :, :heads], (0, 2, 1))
            lfn_t = _pad_lanes(lfn_t, lf_pool_t.shape[-1])
            mix = _fox_decode(q, k, v, lfn_t, k_pool, v_pool, lf_pool_t, page_table, j, batch, t, heads)
        shp = (batch, t, heads, HEAD_DIM)
        out[name] = dict(mix=mix, tail=tail, k=k.reshape(shp), v=v.reshape(shp),
                         logf=logf[:, :heads].reshape(batch, t, heads))
    return out


def _conv_mixer(x, dims, w_in_conv, w_conv, j, state_conv, heads, mem_width):
    mix_width = heads * HEAD_DIM
    main = 3 * mix_width
    width = w_conv.shape[1]
    w8 = jnp.pad(w_conv[j], ((0, SUBLANES - width), (0, 0)))
    out = {}
    w_in, w_t = _in_weight(w_in_conv)
    projs = _matmul_groups(x, w_in, layer=j, col0=0, n=main, transposed=w_t)
    tails = _tail_proj(x, _tail_weight(w_in, w_t, j, main, [], mem_width), w_t)
    for name, (batch, t) in dims.items():
        proj, tail = projs[name], tails[name]
        if name == "prompt":
            buf8 = jnp.zeros((batch, SUBLANES, mix_width), F32)
        else:
            buf8 = _pad_rows8(state_conv[j], 1)
        mix, last8 = _shortconv(proj, buf8, w8, batch, t, mix_width, width)
        out[name] = dict(mix=mix, tail=tail, buf=last8[:, SUBLANES - (width - 1):, :])
    return out


def _gdn_mixer(x, dims, w_in_gdn, w_gdn_conv, gdn_a_log, gdn_dt_bias, g_gdn_out, j, state_gdn_conv, state_gdn_s,
               heads, mem_width):
    mix_width = heads * HEAD_DIM
    main = 4 * mix_width
    width = w_gdn_conv.shape[1]
    w_in, w_t = _in_weight(w_in_gdn)
    w_tail = _tail_weight(w_in, w_t, j, main, [heads, heads], mem_width)
    a_block = mem_width // LANES
    w8 = jnp.pad(w_gdn_conv[j], ((0, SUBLANES - width), (0, 0)))
    a_log = _pad_lanes(gdn_a_log[j].reshape(1, heads))
    dt_bias = _pad_lanes(gdn_dt_bias[j].reshape(1, heads))
    group = next(g for g in (8, 4, 2, 1) if heads % g == 0)
    out = {}
    projs = _matmul_groups(x, w_in, layer=j, col0=0, n=main, transposed=w_t)
    tails = _tail_proj(x, w_tail, w_t)
    for name, (batch, t) in dims.items():
        proj, tail = projs[name], tails[name]
        if name == "prompt":
            buf = jnp.zeros((batch, width - 1, 3 * mix_width), F32)
            s0 = jnp.zeros((batch, heads, HEAD_DIM, HEAD_DIM), F32)
        else:
            buf, s0 = state_gdn_conv[j], state_gdn_s[j]
        buf8 = _pad_rows8(buf, 1)
        conv = functools.partial(_gdn_conv, proj, buf8, w8, batch=batch, t=t, mix_width=mix_width, width=width)
        qn = conv(part=0, l2norm=True, scale=HEAD_DIM ** -0.5)
        kn = conv(part=1, l2norm=True)
        vc = conv(part=2, l2norm=False)
        g, beta = _gdn_gates(tail, a_block, a_block + 1, a_log, dt_bias)
        z_arr, z_block0 = proj, 3 * mix_width // (group * HEAD_DIM)
        tp = -(-t // GDN_CHUNK) * GDN_CHUNK
        if tp != t:
            pad = lambda a_: jnp.pad(a_.reshape(batch, t, -1), ((0, 0), (0, tp - t), (0, 0))).reshape(batch * tp, -1)
            qn, kn, vc, g, beta = (pad(a_) for a_ in (qn, kn, vc, g, beta))
            z_arr, z_block0 = pad(proj[:, 3 * mix_width:main]), 0
        gc = _cumsum_rows(g, batch, tp, carry=False)
        u, w, qk, qg, kg = _gdn_intra(qn, kn, vc, gc, beta, batch, tp, heads, group)
        mix, s_new = _gdn_state(u, w, qk, qg, kg, gc, z_arr, z_block0, g_gdn_out[j], s0, batch, tp, heads, group)
        if tp != t:
            mix = mix.reshape(batch, tp, mix_width)[:, :t].reshape(batch * t, mix_width)
        keep = min(t, width - 1)
        newest = proj.reshape(batch, t, -1)[:, t - keep:, :3 * mix_width]
        out[name] = dict(mix=mix, tail=tail, conv=jnp.concatenate([buf, newest], axis=1)[:, keep:], s=s_new)
    return out


def kernel(x_prompt, x_sample, cache_fox_k, cache_fox_v, cache_fox_logf, cache_mem_k, cache_mem_v, state_conv, state_gdn_conv, state_gdn_s, page_table, mem_prompt, g_mix_pre, g_mix_post, g_mlp_pre, g_mlp_post, g_mem, w_mem_kv, w_out, w_mlp_up, w_mlp_down, w_in_fox, b_fox_f, w_in_conv, w_conv, w_in_gdn, w_gdn_conv, gdn_a_log, gdn_dt_bias, g_gdn_out):
    bp, tp_, d = x_prompt.shape
    bs, ts, _ = x_sample.shape
    depth = g_mix_pre.shape[0]
    mem_tokens, mem_heads, mem_hd = cache_mem_k.shape[2:]
    mem_width = mem_heads * mem_hd
    mix_width = d - mem_width
    heads = mix_width // HEAD_DIM
    n_fox, n_pool, page = cache_fox_k.shape[:3]

    y = {"prompt": x_prompt.reshape(bp * tp_, d), "sample": x_sample.reshape(bs * ts, d)}
    dims = {"prompt": (bp, tp_), "sample": (bs, ts)}
    x = {n: _norm_cast(y[n], g_mix_pre[0]) for n in y}
    mem2d = mem_prompt.reshape(bp * mem_tokens, d)
    caches = (cache_fox_k.reshape(n_fox, n_pool, page * heads, HEAD_DIM),
              cache_fox_v.reshape(n_fox, n_pool, page * heads, HEAD_DIM),
              jnp.transpose(cache_fox_logf, (0, 1, 3, 2)))
    mem_k_s = cache_mem_k.reshape(depth, bs, mem_tokens, mem_width)
    mem_v_s = cache_mem_v.reshape(depth, bs, mem_tokens, mem_width)

    res = {k_: [] for k_ in ("fox_k_p", "fox_v_p", "fox_lf_p", "fox_k_s", "fox_v_s", "fox_lf_s", "conv_p", "conv_s",
                             "gconv_p", "gconv_s", "gs_p", "gs_s", "mem_k", "mem_v")}
    for i in range(depth):
        kind, j = i % 3, i // 3
        kv = _matmul(_norm_cast(mem2d, g_mem[i]), w_mem_kv, layer=i)
        res["mem_k"].append(kv[:, :mem_width].reshape(bp, mem_tokens, mem_heads, mem_hd))
        res["mem_v"].append(kv[:, mem_width:].reshape(bp, mem_tokens, mem_heads, mem_hd))
        if kind == 0:
            mixed = _fox_mixer(x, dims, w_in_fox, b_fox_f, j, caches, page_table, heads, mem_width)
            for n, s in (("prompt", "p"), ("sample", "s")):
                res["fox_k_" + s].append(mixed[n]["k"])
                res["fox_v_" + s].append(mixed[n]["v"])
                res["fox_lf_" + s].append(mixed[n]["logf"])
        elif kind == 1:
            mixed = _conv_mixer(x, dims, w_in_conv, w_conv, j, state_conv, heads, mem_width)
            res["conv_p"].append(mixed["prompt"]["buf"])
            res["conv_s"].append(mixed["sample"]["buf"])
        else:
            mixed = _gdn_mixer(x, dims, w_in_gdn, w_gdn_conv, gdn_a_log, gdn_dt_bias, g_gdn_out, j, state_gdn_conv,
                               state_gdn_s, heads, mem_width)
            for n, s in (("prompt", "p"), ("sample", "s")):
                res["gconv_" + s].append(mixed[n]["conv"])
                res["gs_" + s].append(mixed[n]["s"])
        g_next = g_mix_pre[i + 1] if i + 1 < depth else g_mlp_pre[i]
        cat = {}
        for n, (b, t) in dims.items():
            if n == "prompt":
                mem = _mem_attn(mixed[n]["tail"], kv, kv, lambda bb, part: (bb, part), b, t, mem_heads, mem_width,
                                mem_tokens)
            else:
                mem = _mem_attn(mixed[n]["tail"], mem_k_s, mem_v_s, lambda bb, part: (i, bb, 0, 0), b, t, mem_heads,
                                mem_width, mem_tokens)
            cat[n] = (mixed[n]["mix"], mem)
        delta = _matmul_groups(cat, w_out, layer=i)
        x2 = {}
        for n in dims:
            y[n], x2[n] = _resid_norm(delta[n], y[n], g_mix_post[i], g_mlp_pre[i])
        hid = _matmul_groups(x2, w_mlp_up, layer=i, out_dtype=BF16, act="relu2")
        delta = _matmul_groups(hid, w_mlp_down, layer=i, tm=2048, tn=512, tk=2048)
        for n in dims:
            y[n], x[n] = _resid_norm(delta[n], y[n], g_mlp_post[i], g_next)

    st = lambda k_: jnp.stack(res[k_])
    return (y["prompt"].reshape(bp, tp_, d), y["sample"].reshape(bs, ts, d),
            st("fox_k_p"), st("fox_v_p"), st("fox_lf_p"), st("conv_p"), st("gconv_p"), st("gs_p"),
            st("mem_k"), st("mem_v"),
            st("fox_k_s"), st("fox_v_s"), st("fox_lf_s"), st("conv_s"), st("gconv_s"), st("gs_s"))
```

```python
import functools

import jax
import jax.numpy as jnp
from jax import lax
from jax.experimental import pallas as pl
from jax.experimental.pallas import tpu as pltpu

F32 = jnp.float32
BF16 = jnp.bfloat16
RMS_EPS = 1e-6
L2_EPS = 1e-6
HEAD_DIM = 128
LANES = 128
SUBLANES = 8
GDN_CHUNK = 128
V7X_VMEM_BYTES = 64 * 1024 * 1024
VMEM_CAP = V7X_VMEM_BYTES - 6 * 1024 * 1024
HIGHEST = lax.Precision.HIGHEST
NT_DIMS = (((1,), (1,)), ((), ()))
TN_DIMS = (((0,), (0,)), ((), ()))


def _params(semantics, vmem_bytes):
    limit = int(min(VMEM_CAP, max(16 * 1024 * 1024, vmem_bytes * 5 // 4 + (2 << 20))))
    return pltpu.CompilerParams(dimension_semantics=semantics, vmem_limit_bytes=limit)


def _tile(dim, pref, align=LANES):
    if dim <= pref:
        return dim
    for cand in range(pref - pref % align, 0, -align):
        if dim % cand == 0:
            return cand
    raise ValueError((dim, pref, align))


def _dot(a, b):
    return jnp.dot(a.astype(BF16), b.astype(BF16), preferred_element_type=F32)


def _dot_nt(a, b):
    return lax.dot_general(a.astype(BF16), b.astype(BF16), NT_DIMS, preferred_element_type=F32)


def _dot_tn(a, b):
    return lax.dot_general(a.astype(BF16), b.astype(BF16), TN_DIMS, preferred_element_type=F32)


def _each(fn, *columns):
    return [fn(*args) for args in zip(*columns)]


def _log_sigmoid(x):
    return jnp.minimum(x, 0.0) - jnp.log1p(jnp.exp(-jnp.abs(x)))


def _softplus(x):
    return jnp.maximum(x, 0.0) + jnp.log1p(jnp.exp(-jnp.abs(x)))


def _silu(x):
    return x * jax.nn.sigmoid(x)


def _pick_lane(blk, idx):
    lane = lax.broadcasted_iota(jnp.int32, blk.shape, 1)
    return jnp.sum(jnp.where(lane == idx, blk, 0.0), axis=-1, keepdims=True)


def _norm_cast_kernel(y_ref, g_ref, o_ref):
    y = y_ref[...]
    inv = lax.rsqrt(jnp.mean(y * y, axis=-1, keepdims=True) + RMS_EPS)
    o_ref[...] = (y * inv * g_ref[...]).astype(o_ref.dtype)


def _norm_cast(y, g):
    m, d = y.shape
    tr = _tile(m, 256)
    return pl.pallas_call(
        _norm_cast_kernel,
        grid=(m // tr,),
        in_specs=[pl.BlockSpec((tr, d), lambda i: (i, 0)), pl.BlockSpec((1, d), lambda i: (0, 0))],
        out_specs=pl.BlockSpec((tr, d), lambda i: (i, 0)),
        out_shape=jax.ShapeDtypeStruct((m, d), BF16),
        compiler_params=_params(("parallel",), 2 * tr * d * 6),
        name="norm_cast",
    )(y, g.reshape(1, d))


def _resid_norm_kernel(d_ref, y_ref, gp_ref, gn_ref, yo_ref, xo_ref):
    d = d_ref[...]
    inv = lax.rsqrt(jnp.mean(d * d, axis=-1, keepdims=True) + RMS_EPS)
    yn = y_ref[...] + d * inv * gp_ref[...]
    yo_ref[...] = yn
    inv2 = lax.rsqrt(jnp.mean(yn * yn, axis=-1, keepdims=True) + RMS_EPS)
    xo_ref[...] = (yn * inv2 * gn_ref[...]).astype(xo_ref.dtype)


def _resid_norm(delta, y, g_post, g_next):
    m, d = y.shape
    tr = _tile(m, 256)
    row = pl.BlockSpec((tr, d), lambda i: (i, 0))
    vec = pl.BlockSpec((1, d), lambda i: (0, 0))
    return pl.pallas_call(
        _resid_norm_kernel,
        grid=(m // tr,),
        in_specs=[row, row, vec, vec],
        out_specs=[row, row],
        out_shape=[jax.ShapeDtypeStruct((m, d), F32), jax.ShapeDtypeStruct((m, d), BF16)],
        compiler_params=_params(("parallel",), 2 * tr * d * 14),
        name="resid_norm",
    )(delta, y, g_post.reshape(1, d), g_next.reshape(1, d))


def _mm_kernel(*refs, nk, act, rider, transposed, parts):
    x_refs, refs = refs[:parts], refs[parts:]
    if rider:
        xr_refs, (w_ref, o_ref, or_ref) = refs[:parts], refs[parts:]
    else:
        w_ref, o_ref = refs
    k = pl.program_id(2)
    dims = NT_DIMS if transposed else (((1,), (0,)), ((), ()))

    def product(lhs_refs, out_ref):
        lhs = lhs_refs[0][...] if parts == 1 else jnp.concatenate([r[...] for r in lhs_refs], axis=1)
        part = lax.dot_general(lhs, w_ref[...].astype(BF16), dims, preferred_element_type=F32)
        if nk == 1:
            if act == "relu2":
                part = jnp.square(jnp.maximum(part, 0.0))
            out_ref[...] = part.astype(out_ref.dtype)
        else:
            @pl.when(k == 0)
            def _():
                out_ref[...] = jnp.zeros_like(out_ref)

            out_ref[...] += part

    product(x_refs, o_ref)
    if rider:
        i = pl.program_id(0)

        @pl.when(i == 0)
        def _():
            product(xr_refs, or_ref)

        @pl.when(i > 0)
        def _():
            or_ref[...] = jnp.zeros_like(or_ref)


def _matmul(x, w, *, rider=None, layer=None, col0=0, n=None, out_dtype=F32, act=None, transposed=False,
            tall=False, tm=1024, tn=512, tk=4096):
    xs = x if isinstance(x, (tuple, list)) else (x,)
    riders = () if rider is None else (rider if isinstance(rider, (tuple, list)) else (rider,))
    m, kdim = xs[0].shape[0], sum(part.shape[1] for part in xs)
    n_axis, k_axis = (-2, -1) if transposed else (-1, -2)
    n = w.shape[n_axis] - col0 if n is None else n
    tm, tn, tk = _tile(m, tm), _tile(n, tn), _tile(kdim, tk)
    assert col0 % tn == 0 and w.shape[k_axis] == kdim
    nk = kdim // tk
    assert nk == 1 or (act is None and out_dtype == F32 and len(xs) == 1)
    lhs_single_buffer = nk == 1 and tall and m % (2 * tm) == 0
    if lhs_single_buffer:
        tm *= 2
    cb = col0 // tn
    widths = [tk] if len(xs) == 1 else [part.shape[1] for part in xs]
    lead = () if layer is None else (None,)
    at = (lambda *idx: idx) if layer is None else (lambda *idx: (layer,) + idx)
    if transposed:
        w_spec = pl.BlockSpec(lead + (tn, tk), lambda i, j, k: at(cb + j, k))
    else:
        w_spec = pl.BlockSpec(lead + (tk, tn), lambda i, j, k: at(k, cb + j))
    osz = jnp.dtype(out_dtype).itemsize
    lhs_mode = dict(pipeline_mode=pl.Buffered(1)) if lhs_single_buffer else {}
    in_specs = [pl.BlockSpec((tm, width), lambda i, j, k: (i, k), **lhs_mode) for width in widths]
    out_specs = [pl.BlockSpec((tm, tn), lambda i, j, k: (i, j))]
    out_shape = [jax.ShapeDtypeStruct((m, n), out_dtype)]
    operands = list(xs)
    mr = 0
    if riders:
        mr = riders[0].shape[0]
        spare = min(1, m // tm - 1)
        in_specs += [pl.BlockSpec((mr, width), lambda i, j, k: (0, k)) for width in widths]
        out_specs.append(pl.BlockSpec((None, mr, tn), lambda i, j, k: (jnp.minimum(i, spare), 0, j)))
        out_shape.append(jax.ShapeDtypeStruct((1 + spare, mr, n), out_dtype))
        operands += list(riders)
    vmem = 2 * ((tm + mr) * tk * 2 + tk * tn * 4 + (tm + mr) * tn * osz) + tk * tn * 2 + 2 * tm * tn * 4
    outs = pl.pallas_call(
        functools.partial(_mm_kernel, nk=nk, act=act, rider=bool(riders), transposed=transposed, parts=len(xs)),
        grid=(m // tm, n // tn, nk),
        in_specs=in_specs + [w_spec],
        out_specs=out_specs,
        out_shape=out_shape,
        compiler_params=_params(("parallel", "parallel", "arbitrary"), vmem),
        name="matmul",
    )(*operands, w)
    return outs[0] if not riders else (outs[0], outs[1][0])


def _matmul_groups(x, w, **kw):
    out, out_r = _matmul(x["prompt"], w, rider=x["sample"], tall=True, **kw)
    return {"prompt": out, "sample": out_r}


def _mem_attn_kernel(q_ref, k_ref, v_ref, o_ref, *, heads):
    dh = q_ref.shape[-1] // heads
    scale = dh ** -0.5
    cols = [slice(hh * dh, (hh + 1) * dh) for hh in range(heads)]
    of = lambda ref: [ref[:, sl] for sl in cols]
    s = _each(lambda q, k: _dot_nt(q, k) * scale, of(q_ref), of(k_ref))
    p = _each(lambda s_: jnp.exp(s_ - jnp.max(s_, axis=-1, keepdims=True)), s)
    o = _each(lambda p_, v: _dot(p_, v) / jnp.sum(p_, axis=-1, keepdims=True), p, of(v_ref))
    for sl, o_ in zip(cols, o):
        o_ref[:, sl] = o_.astype(o_ref.dtype)


def _mem_attn(tail, k_arr, v_arr, kv_index, batch, t, heads, mem_width, mt):
    tt = _tile(t, 512)
    nt = t // tt
    lead = (None,) * (k_arr.ndim - 2)
    k_spec = pl.BlockSpec(lead + (mt, mem_width), lambda b, i: kv_index(b, 0))
    v_spec = pl.BlockSpec(lead + (mt, mem_width), lambda b, i: kv_index(b, 1))
    return pl.pallas_call(
        functools.partial(_mem_attn_kernel, heads=heads),
        grid=(batch, nt),
        in_specs=[pl.BlockSpec((tt, mem_width), lambda b, i: (b * nt + i, 0)), k_spec, v_spec],
        out_specs=pl.BlockSpec((tt, mem_width), lambda b, i: (b * nt + i, 0)),
        out_shape=jax.ShapeDtypeStruct((batch * t, mem_width), BF16),
        compiler_params=_params(("parallel", "parallel"), 2 * (tt * mem_width * 6 + 2 * mt * mem_width * 4) + 4 * tt * mt * 4),
        name="mem_attn",
    )(tail, k_arr, v_arr)


def _logf_kernel(f_ref, b_ref, o_ref):
    o_ref[...] = _log_sigmoid(f_ref[...] + b_ref[...])


def _fox_logf(tail, gate_block, bias_pad):
    m = tail.shape[0]
    tr = _tile(m, 1024)
    return pl.pallas_call(
        _logf_kernel,
        grid=(m // tr,),
        in_specs=[pl.BlockSpec((tr, LANES), lambda i: (i, gate_block)), pl.BlockSpec((1, LANES), lambda i: (0, 0))],
        out_specs=pl.BlockSpec((tr, LANES), lambda i: (i, 0)),
        out_shape=jax.ShapeDtypeStruct((m, LANES), F32),
        compiler_params=_params(("parallel",), 4 * tr * LANES * 4),
        name="fox_logf",
    )(tail, bias_pad)


def _cumsum_kernel(x_ref, o_ref, carry_ref, *, carry):
    tr = x_ref.shape[0]
    r = lax.broadcasted_iota(jnp.int32, (tr, tr), 0)
    c = lax.broadcasted_iota(jnp.int32, (tr, tr), 1)
    lower = (r >= c).astype(F32)
    cs = jnp.dot(lower, x_ref[...], precision=HIGHEST, preferred_element_type=F32)
    if carry:
        @pl.when(pl.program_id(1) == 0)
        def _():
            carry_ref[...] = jnp.zeros_like(carry_ref)

        cs = cs + carry_ref[...]
        carry_ref[...] = cs[tr - 1:tr, :]
    o_ref[...] = cs


def _cumsum_rows(x, batch, t, *, carry):
    tr = LANES
    nt = t // tr
    return pl.pallas_call(
        functools.partial(_cumsum_kernel, carry=carry),
        grid=(batch, nt),
        in_specs=[pl.BlockSpec((tr, LANES), lambda b, i: (b * nt + i, 0))],
        out_specs=pl.BlockSpec((tr, LANES), lambda b, i: (b * nt + i, 0)),
        out_shape=jax.ShapeDtypeStruct(x.shape, F32),
        scratch_shapes=[pltpu.VMEM((1, LANES), F32)],
        compiler_params=_params(("parallel", "arbitrary"), 8 * tr * LANES * 4),
        name="cumsum_rows",
    )(x)


def _fox_prompt_kernel(q_ref, k_ref, v_ref, crow_ref, o_ref, kb_ref, vb_ref, *, n_q, group):
    qi = pl.program_id(2)
    tq = q_ref.shape[0]
    scale = HEAD_DIM ** -0.5
    cols = [slice(i * HEAD_DIM, (i + 1) * HEAD_DIM) for i in range(group)]

    @pl.when(qi == 0)
    def _():
        kb_ref[...] = k_ref[...].astype(BF16)
        vb_ref[...] = v_ref[...].astype(BF16)

    def attend(extent):
        row = lax.broadcasted_iota(jnp.int32, (tq, extent), 0) + (extent - tq)
        col = lax.broadcasted_iota(jnp.int32, (tq, extent), 1)
        s = [_dot_nt(q_ref[:, sl], kb_ref[0:extent, sl]) for sl in cols]
        s = [jnp.where(col <= row, s[i] * scale - crow_ref[i, :, 0:extent], -jnp.inf) for i in range(group)]
        p = _each(lambda s_: jnp.exp(s_ - jnp.max(s_, axis=-1, keepdims=True)), s)
        pv = [_dot(p[i], vb_ref[0:extent, cols[i]]) for i in range(group)]
        for i in range(group):
            o_ref[:, cols[i]] = (pv[i] / jnp.sum(p[i], axis=-1, keepdims=True)).astype(o_ref.dtype)

    for tile in range(n_q):
        pl.when(qi == tile)(functools.partial(attend, (tile + 1) * tq))


def _fox_prompt(q, k, v, c_row, batch, t, heads):
    tq = _tile(t, 256)
    nq = t // tq
    group = next(g for g in (4, 2, 1) if heads % g == 0)
    gw = group * HEAD_DIM
    blk = pl.BlockSpec((tq, gw), lambda b, g, i: (b * nq + i, g))
    full = pl.BlockSpec((t, gw), lambda b, g, i: (b, g))
    return pl.pallas_call(
        functools.partial(_fox_prompt_kernel, n_q=nq, group=group),
        grid=(batch, heads // group, nq),
        in_specs=[blk, full, full, pl.BlockSpec((None, group, 1, t), lambda b, g, i: (b, g, 0, 0))],
        out_specs=blk,
        out_shape=jax.ShapeDtypeStruct((batch * t, heads * HEAD_DIM), BF16),
        scratch_shapes=[pltpu.VMEM((t, gw), BF16), pltpu.VMEM((t, gw), BF16)],
        compiler_params=_params(("parallel", "parallel", "arbitrary"), 5 * t * gw * 4 + 6 * group * tq * t * 4),
        name="fox_prompt",
    )(q, k, v, c_row)


def _fox_decode_kernel(pt_ref, q_ref, kn_ref, vn_ref, lfn_ref, *rest, n_steps, per_step, heads, t_new, page):
    del pt_ref
    pools = [rest[3 * i:3 * i + 3] for i in range(per_step)]
    o_ref, m_ref, l_ref, acc_ref, carry_ref = rest[3 * per_step:]
    p = pl.program_id(1)
    scale = HEAD_DIM ** -0.5
    r = lax.broadcasted_iota(jnp.int32, (page, page), 0)
    c = lax.broadcasted_iota(jnp.int32, (page, page), 1)
    upper = (r <= c).astype(F32)

    @pl.when(p == 0)
    def _():
        m_ref[...] = jnp.full_like(m_ref, -jnp.inf)
        l_ref[...] = jnp.zeros_like(l_ref)
        acc_ref[...] = jnp.zeros_like(acc_ref)
        carry_ref[...] = jnp.zeros_like(carry_ref)

    head_cols = lambda h: slice(h * HEAD_DIM, (h + 1) * HEAD_DIM)

    def attend(k_heads, v_heads, lf_t, mask):
        c_t = jnp.dot(lf_t, upper, precision=HIGHEST, preferred_element_type=F32) + carry_ref[...]
        carry_ref[...] = c_t[:, page - 1:page]
        s = jnp.stack([_dot_nt(q_ref[:, head_cols(h)], k_heads[h]) for h in range(heads)])
        s = s * scale - c_t[:, None, :]
        if mask is not None:
            s = jnp.where(mask[None], s, -jnp.inf)
        m_old = m_ref[...]
        m_new = jnp.maximum(m_old, jnp.max(s, axis=-1, keepdims=True))
        alpha = jnp.exp(m_old - m_new)
        pr = jnp.exp(s - m_new)
        l_ref[...] = alpha * l_ref[...] + jnp.sum(pr, axis=-1, keepdims=True)
        pv = jnp.stack([_dot(pr[h], v_heads[h]) for h in range(heads)])
        acc_ref[...] = alpha * acc_ref[...] + pv
        m_ref[...] = m_new

    @pl.when(p < n_steps)
    def _():
        for kp_ref, vp_ref, lfp_ref in pools:
            attend([kp_ref[pl.ds(h, page, stride=heads), :] for h in range(heads)],
                   [vp_ref[pl.ds(h, page, stride=heads), :] for h in range(heads)], lfp_ref[...], None)

    @pl.when(p == n_steps)
    def _():
        pad = jnp.zeros((page - t_new, HEAD_DIM), F32)
        qrow = lax.broadcasted_iota(jnp.int32, (t_new, page), 0)
        kcol = lax.broadcasted_iota(jnp.int32, (t_new, page), 1)
        attend([jnp.concatenate([kn_ref[:, head_cols(h)], pad], axis=0) for h in range(heads)],
               [jnp.concatenate([vn_ref[:, head_cols(h)], pad], axis=0) for h in range(heads)],
               lfn_ref[...], kcol <= qrow)
        out = acc_ref[...] / l_ref[...]
        for h in range(heads):
            o_ref[:, head_cols(h)] = out[h].astype(o_ref.dtype)


def _fox_decode(q, k, v, lfn_t, k_pool, v_pool, lf_pool_t, page_table, layer, batch, t_new, heads):
    n_pages = page_table.shape[1]
    page = k_pool.shape[2] // heads
    width = heads * HEAD_DIM
    assert page == LANES and t_new <= page
    per_step = next(c for c in (4, 2, 1) if n_pages % c == 0)
    n_steps = n_pages // per_step
    new = pl.BlockSpec((t_new, width), lambda b, p, pt: (b, 0))

    def page_specs(slot):
        at = lambda b, p, pt: (layer, pt[b, jnp.minimum(p, n_steps - 1) * per_step + slot], 0, 0)
        return [pl.BlockSpec((None, None, page * heads, HEAD_DIM), at),
                pl.BlockSpec((None, None, page * heads, HEAD_DIM), at),
                pl.BlockSpec((None, None, heads, page), at)]

    grid_spec = pltpu.PrefetchScalarGridSpec(
        num_scalar_prefetch=1,
        grid=(batch, n_steps + 1),
        in_specs=[new, new, new, pl.BlockSpec((None, heads, page), lambda b, p, pt: (b, 0, 0))]
        + [spec for slot in range(per_step) for spec in page_specs(slot)],
        out_specs=new,
        scratch_shapes=[pltpu.VMEM((heads, t_new, 1), F32), pltpu.VMEM((heads, t_new, 1), F32),
                        pltpu.VMEM((heads, t_new, HEAD_DIM), F32), pltpu.VMEM((heads, 1), F32)],
    )
    return pl.pallas_call(
        functools.partial(_fox_decode_kernel, n_steps=n_steps, per_step=per_step, heads=heads, t_new=t_new,
                          page=page),
        grid_spec=grid_spec,
        out_shape=jax.ShapeDtypeStruct((batch * t_new, width), BF16),
        compiler_params=_params(("parallel", "arbitrary"),
                                4 * per_step * page * width * 4 + 8 * t_new * width * 4 + (4 << 20)),
        name="fox_decode",
    )(page_table, q, k, v, lfn_t, *([k_pool, v_pool, lf_pool_t] * per_step))


def _shortconv_kernel(gb_ref, gc_ref, h_ref, buf_ref, w_ref, o_ref, tail_ref, us_ref, *, width):
    tt = gb_ref.shape[0]

    @pl.when(pl.program_id(2) == 0)
    def _():
        us_ref[0:SUBLANES, :] = buf_ref[...]

    us_ref[SUBLANES:SUBLANES + tt, :] = gc_ref[...] * h_ref[...]
    conv = None
    for k in range(width):
        off = SUBLANES - (width - 1) + k
        term = w_ref[k:k + 1, :] * us_ref[off:off + tt, :]
        conv = term if conv is None else conv + term
    o_ref[...] = (gb_ref[...] * conv).astype(o_ref.dtype)
    last = us_ref[tt:tt + SUBLANES, :]
    tail_ref[...] = last
    us_ref[0:SUBLANES, :] = last


def _shortconv(proj, buf8, w8, batch, t, mix_width, width):
    tt = _tile(t, 1024)
    tc = _tile(mix_width, 512)
    nt, nc = t // tt, mix_width // tc
    col = lambda off: pl.BlockSpec((tt, tc), lambda b, c, i: (b * nt + i, off * nc + c))
    return pl.pallas_call(
        functools.partial(_shortconv_kernel, width=width),
        grid=(batch, nc, nt),
        in_specs=[col(0), col(1), col(2),
                  pl.BlockSpec((None, SUBLANES, tc), lambda b, c, i: (b, 0, c)),
                  pl.BlockSpec((SUBLANES, tc), lambda b, c, i: (0, c))],
        out_specs=[col(0), pl.BlockSpec((None, SUBLANES, tc), lambda b, c, i: (b, 0, c))],
        out_shape=[jax.ShapeDtypeStruct((batch * t, mix_width), BF16),
                   jax.ShapeDtypeStruct((batch, SUBLANES, mix_width), F32)],
        scratch_shapes=[pltpu.VMEM((tt + SUBLANES, tc), F32)],
        compiler_params=_params(("parallel", "parallel", "arbitrary"), 12 * tt * tc * 4),
        name="shortconv",
    )(proj, proj, proj, buf8, w8)


def _gdn_conv_kernel(x_ref, buf_ref, w_ref, o_ref, xs_ref, *, width, l2norm, scale):
    tt, tc = x_ref.shape

    @pl.when(pl.program_id(2) == 0)
    def _():
        xs_ref[0:SUBLANES, :] = buf_ref[...]

    xs_ref[SUBLANES:SUBLANES + tt, :] = x_ref[...]
    conv = None
    for k in range(width):
        off = SUBLANES - (width - 1) + k
        term = w_ref[k:k + 1, :] * xs_ref[off:off + tt, :]
        conv = term if conv is None else conv + term
    y = _silu(conv)
    if l2norm:
        for g in range(tc // HEAD_DIM):
            sl = slice(g * HEAD_DIM, (g + 1) * HEAD_DIM)
            seg = y[:, sl]
            seg = seg * lax.rsqrt(jnp.sum(seg * seg, axis=-1, keepdims=True) + L2_EPS)
            o_ref[:, sl] = seg * scale if scale != 1.0 else seg
    else:
        o_ref[...] = y
    xs_ref[0:SUBLANES, :] = xs_ref[tt:tt + SUBLANES, :]


def _gdn_conv(proj, buf8, w8, part, batch, t, mix_width, width, *, l2norm, scale=1.0):
    tt = _tile(t, 1024)
    tc = _tile(mix_width, 512)
    nt, nc = t // tt, mix_width // tc
    return pl.pallas_call(
        functools.partial(_gdn_conv_kernel, width=width, l2norm=l2norm, scale=scale),
        grid=(batch, nc, nt),
        in_specs=[pl.BlockSpec((tt, tc), lambda b, c, i: (b * nt + i, part * nc + c)),
                  pl.BlockSpec((None, SUBLANES, tc), lambda b, c, i: (b, 0, part * nc + c)),
                  pl.BlockSpec((SUBLANES, tc), lambda b, c, i: (0, part * nc + c))],
        out_specs=pl.BlockSpec((tt, tc), lambda b, c, i: (b * nt + i, c)),
        out_shape=jax.ShapeDtypeStruct((batch * t, mix_width), F32),
        scratch_shapes=[pltpu.VMEM((tt + SUBLANES, tc), F32)],
        compiler_params=_params(("parallel", "parallel", "arbitrary"), 10 * tt * tc * 4),
        name="gdn_conv",
    )(proj, buf8, w8)


def _gdn_gate_kernel(a_ref, bt_ref, alog_ref, dt_ref, g_ref, beta_ref):
    g_ref[...] = -jnp.exp(alog_ref[...]) * _softplus(a_ref[...] + dt_ref[...])
    beta_ref[...] = jax.nn.sigmoid(bt_ref[...])


def _gdn_gates(tail, a_block, bt_block, a_log_pad, dt_pad):
    m = tail.shape[0]
    tr = _tile(m, 1024)
    blk = lambda off: pl.BlockSpec((tr, LANES), lambda i: (i, off))
    vec = pl.BlockSpec((1, LANES), lambda i: (0, 0))
    return pl.pallas_call(
        _gdn_gate_kernel,
        grid=(m // tr,),
        in_specs=[blk(a_block), blk(bt_block), vec, vec],
        out_specs=[blk(0), blk(0)],
        out_shape=[jax.ShapeDtypeStruct((m, LANES), F32)] * 2,
        compiler_params=_params(("parallel",), 8 * tr * LANES * 4),
        name="gdn_gates",
    )(tail, tail, a_log_pad, dt_pad)


def _unit_lower_inverse_offdiag(mats, ri, ci):
    n = mats[0].shape[0]
    blk = lambda idx, size: jnp.right_shift(idx, size.bit_length() - 1)
    base = SUBLANES
    in_base = blk(ri, base) == blk(ci, base)
    a0 = _each(lambda a: jnp.where(in_base, a, 0.0), mats)
    low = _each(lambda a: -a, a0)
    pw = _each(_dot, a0, a0)
    low = _each(lambda l, p, lp: l + p + lp, low, pw, _each(_dot, low, pw))
    pw = _each(_dot, pw, pw)
    low = _each(lambda l, p, lp: l + p + lp, low, pw, _each(_dot, low, pw))
    s = base
    while s < n:
        sub = (blk(ri, 2 * s) == blk(ci, 2 * s)) & (blk(ri, s) != blk(ci, s))
        off = _each(lambda a: jnp.where(sub, a, 0.0), mats)
        x = _each(lambda o, lo: o + lo, off, _each(_dot, low, off))
        low = _each(lambda l, x_, xl: l - (x_ + xl), low, x, _each(_dot, x, low))
        s *= 2
    return low


def _gdn_intra_kernel(q_ref, k_ref, v_ref, gc_ref, beta_ref, u_ref, w_ref, qk_ref, qg_ref, kg_ref, *, group):
    hg = pl.program_id(1)
    n = q_ref.shape[0]
    ri = lax.broadcasted_iota(jnp.int32, (n, n), 0)
    ci = lax.broadcasted_iota(jnp.int32, (n, n), 1)
    incl = ri >= ci
    cols = [slice(i * HEAD_DIM, (i + 1) * HEAD_DIM) for i in range(group)]
    q = [q_ref[:, sl] for sl in cols]
    k = [k_ref[:, sl] for sl in cols]
    v = [v_ref[:, sl] for sl in cols]
    gcol = [_pick_lane(gc_ref[...], hg * group + i) for i in range(group)]
    bcol = [_pick_lane(beta_ref[...], hg * group + i) for i in range(group)]

    def decay_of(g):
        gmat = jnp.broadcast_to(g, (n, n))
        return jnp.where(incl, jnp.exp(jnp.where(incl, gmat - gmat.T, 0.0)), 0.0)

    decay = _each(decay_of, gcol)
    kb = _each(lambda k_, b: k_ * b, k, bcol)
    a = _each(lambda kk, d: jnp.where(ri > ci, kk * d, 0.0), _each(_dot_nt, kb, k), decay)
    low = _unit_lower_inverse_offdiag(a, ri, ci)
    eg = _each(jnp.exp, gcol)
    rhs = _each(lambda v_, b, kb_, e: jnp.concatenate([v_ * b, kb_ * e], axis=1), v, bcol, kb, eg)
    uw = _each(lambda r, lr: r + lr, rhs, _each(_dot, low, rhs))
    qk = _each(lambda x, d: x * d, _each(_dot_nt, q, k), decay)
    for i, sl in enumerate(cols):
        u_ref[:, sl] = uw[i][:, :HEAD_DIM]
        w_ref[:, sl] = uw[i][:, HEAD_DIM:].astype(w_ref.dtype)
        qk_ref[:, sl] = qk[i].astype(qk_ref.dtype)
        qg_ref[:, sl] = (q[i] * eg[i]).astype(qg_ref.dtype)
        kg_ref[:, sl] = (k[i] * jnp.exp(gcol[i][n - 1:n, :] - gcol[i])).astype(kg_ref.dtype)


def _gdn_intra(qn, kn, vc, gc, beta, batch, t, heads, group):
    n = GDN_CHUNK
    nc = t // n
    wide = pl.BlockSpec((n, group * HEAD_DIM), lambda b, g, c: (b * nc + c, g))
    gate = pl.BlockSpec((n, LANES), lambda b, g, c: (b * nc + c, 0))
    shape = lambda dt: jax.ShapeDtypeStruct((batch * t, heads * HEAD_DIM), dt)
    return pl.pallas_call(
        functools.partial(_gdn_intra_kernel, group=group),
        grid=(batch, heads // group, nc),
        in_specs=[wide, wide, wide, gate, gate],
        out_specs=[wide] * 5,
        out_shape=[shape(F32)] + [shape(BF16)] * 4,
        compiler_params=_params(("parallel", "parallel", "parallel"), 2 * 8 * n * group * HEAD_DIM * 4 + (8 << 20)),
        name="gdn_intra",
    )(qn, kn, vc, gc, beta)


def _gdn_state_kernel(u_ref, w_ref, qk_ref, qg_ref, kg_ref, gc_ref, z_ref, gout_ref, s0_ref, o_ref, sout_ref,
                      s_ref, *, group, n_chunks):
    hg = pl.program_id(1)
    c = pl.program_id(2)
    n = u_ref.shape[0]

    @pl.when(c == 0)
    def _():
        s_ref[...] = s0_ref[...]

    cols = [slice(i * HEAD_DIM, (i + 1) * HEAD_DIM) for i in range(group)]
    of = lambda ref: [ref[:, sl] for sl in cols]
    s = [s_ref[i] for i in range(group)]
    v_new = _each(lambda u, ws: u - ws, of(u_ref), _each(_dot, of(w_ref), s))
    o = _each(lambda x, y: x + y, _each(_dot, of(qg_ref), s), _each(_dot, of(qk_ref), v_new))
    upd = _each(_dot_tn, of(kg_ref), v_new)
    for i, sl in enumerate(cols):
        g_last = _pick_lane(gc_ref[n - 1:n, :], hg * group + i)
        s_ref[i] = s[i] * jnp.exp(g_last) + upd[i]
        on = o[i] * lax.rsqrt(jnp.mean(o[i] * o[i], axis=-1, keepdims=True) + RMS_EPS) * gout_ref[...]
        o_ref[:, sl] = (on * _silu(z_ref[:, sl])).astype(o_ref.dtype)

    @pl.when(c == n_chunks - 1)
    def _():
        sout_ref[...] = s_ref[...]


def _gdn_state(u, w, qk, qg, kg, gc, z_arr, z_block0, g_out, s0, batch, t, heads, group):
    n = GDN_CHUNK
    nc = t // n
    gw = group * HEAD_DIM
    wide = pl.BlockSpec((n, gw), lambda b, g, c: (b * nc + c, g))
    state = pl.BlockSpec((None, group, HEAD_DIM, HEAD_DIM), lambda b, g, c: (b, g, 0, 0))
    return pl.pallas_call(
        functools.partial(_gdn_state_kernel, group=group, n_chunks=nc),
        grid=(batch, heads // group, nc),
        in_specs=[wide] * 5 + [pl.BlockSpec((n, LANES), lambda b, g, c: (b * nc + c, 0)),
                               pl.BlockSpec((n, gw), lambda b, g, c: (b * nc + c, z_block0 + g)),
                               pl.BlockSpec((1, HEAD_DIM), lambda b, g, c: (0, 0)), state],
        out_specs=[wide, state],
        out_shape=[jax.ShapeDtypeStruct((batch * t, heads * HEAD_DIM), BF16),
                   jax.ShapeDtypeStruct((batch, heads, HEAD_DIM, HEAD_DIM), F32)],
        scratch_shapes=[pltpu.VMEM((group, HEAD_DIM, HEAD_DIM), F32)],
        compiler_params=_params(("parallel", "parallel", "arbitrary"), 2 * 8 * n * gw * 4 + 6 * group * HEAD_DIM * HEAD_DIM * 4 + (4 << 20)),
        name="gdn_state",
    )(u, w, qk, qg, kg, gc, z_arr, g_out.reshape(1, HEAD_DIM), s0)


def _pad_lanes(v, width=LANES):
    return jnp.pad(v, [(0, 0)] * (v.ndim - 1) + [(0, width - v.shape[-1])])


def _pad_rows8(a, rows_axis):
    pad = [(0, 0)] * a.ndim
    pad[rows_axis] = (SUBLANES - a.shape[rows_axis], 0)
    return jnp.pad(a, pad)


def _in_weight(w_stack):
    if w_stack.shape[2] % LANES:
        return jnp.swapaxes(w_stack, 1, 2), True
    return w_stack, False


def _tail_weight(w, transposed, j, main, parts, mem_width):
    n_axis = 0 if transposed else 1
    total = w.shape[1 + n_axis]
    take = (lambda a, b: w[j, a:b, :]) if transposed else (lambda a, b: w[j, :, a:b])
    pad = [(0, 0), (0, 0)]
    cols = [take(total - mem_width, total)]
    off = main
    for width in parts:
        pad[n_axis] = (0, LANES - width)
        cols.append(jnp.pad(take(off, off + width), pad))
        off += width
    return jnp.concatenate(cols, axis=n_axis)


def _tail_proj(x, w_tail, transposed):
    return _matmul_groups(x, w_tail, transposed=transposed)


def _fox_mixer(x, dims, w_in_fox, b_fox_f, j, caches, page_table, heads, mem_width):
    mix_width = heads * HEAD_DIM
    main = 3 * mix_width
    w_in, w_t = _in_weight(w_in_fox)
    w_tail = _tail_weight(w_in, w_t, j, main, [heads], mem_width)
    gate_block = mem_width // LANES
    bias = _pad_lanes(b_fox_f[j].reshape(1, heads))
    out = {}
    qkv = [_matmul_groups(x, w_in, layer=j, col0=part * mix_width, n=mix_width, transposed=w_t) for part in range(3)]
    tails = _tail_proj(x, w_tail, w_t)
    for name, (batch, t) in dims.items():
        q, k, v = (part[name] for part in qkv)
        tail = tails[name]
        logf = _fox_logf(tail, gate_block, bias)
        if name == "prompt":
            c_tok = _cumsum_rows(logf, batch, t, carry=True)
            c_row = jnp.transpose(c_tok.reshape(batch, t, LANES)[:, :, :heads], (0, 2, 1))[:, :, None, :]
            mix = _fox_prompt(q, k, v, c_row, batch, t, heads)
        else:
            k_pool, v_pool, lf_pool_t = caches
            lfn_t = jnp.transpose(logf.reshape(batch, t, LANES)[:, :, :heads], (0, 2, 1))
            lfn_t = _pad_lanes(lfn_t, lf_pool_t.shape[-1])
            mix = _fox_decode(q, k, v, lfn_t, k_pool, v_pool, lf_pool_t, page_table, j, batch, t, heads)
        shp = (batch, t, heads, HEAD_DIM)
        out[name] = dict(mix=mix, tail=tail, k=k.reshape(shp), v=v.reshape(shp),
                         logf=logf[:, :heads].reshape(batch, t, heads))
    return out


def _conv_mixer(x, dims, w_in_conv, w_conv, j, state_conv, heads, mem_width):
    mix_width = heads * HEAD_DIM
    main = 3 * mix_width
    width = w_conv.shape[1]
    w8 = jnp.pad(w_conv[j], ((0, SUBLANES - width), (0, 0)))
    out = {}
    w_in, w_t = _in_weight(w_in_conv)
    projs = _matmul_groups(x, w_in, layer=j, col0=0, n=main, transposed=w_t)
    tails = _tail_proj(x, _tail_weight(w_in, w_t, j, main, [], mem_width), w_t)
    for name, (batch, t) in dims.items():
        proj, tail = projs[name], tails[name]
        if name == "prompt":
            buf8 = jnp.zeros((batch, SUBLANES, mix_width), F32)
        else:
            buf8 = _pad_rows8(state_conv[j], 1)
        mix, last8 = _shortconv(proj, buf8, w8, batch, t, mix_width, width)
        out[name] = dict(mix=mix, tail=tail, buf=last8[:, SUBLANES - (width - 1):, :])
    return out


def _gdn_mixer(x, dims, w_in_gdn, w_gdn_conv, gdn_a_log, gdn_dt_bias, g_gdn_out, j, state_gdn_conv, state_gdn_s,
               heads, mem_width):
    mix_width = heads * HEAD_DIM
    main = 4 * mix_width
    width = w_gdn_conv.shape[1]
    w_in, w_t = _in_weight(w_in_gdn)
    w_tail = _tail_weight(w_in, w_t, j, main, [heads, heads], mem_width)
    a_block = mem_width // LANES
    w8 = jnp.pad(w_gdn_conv[j], ((0, SUBLANES - width), (0, 0)))
    a_log = _pad_lanes(gdn_a_log[j].reshape(1, heads))
    dt_bias = _pad_lanes(gdn_dt_bias[j].reshape(1, heads))
    group = next(g for g in (8, 4, 2, 1) if heads % g == 0)
    out = {}
    projs = _matmul_groups(x, w_in, layer=j, col0=0, n=main, transposed=w_t)
    tails = _tail_proj(x, w_tail, w_t)
    for name, (batch, t) in dims.items():
        proj, tail = projs[name], tails[name]
        if name == "prompt":
            buf = jnp.zeros((batch, width - 1, 3 * mix_width), F32)
            s0 = jnp.zeros((batch, heads, HEAD_DIM, HEAD_DIM), F32)
        else:
            buf, s0 = state_gdn_conv[j], state_gdn_s[j]
        buf8 = _pad_rows8(buf, 1)
        conv = functools.partial(_gdn_conv, proj, buf8, w8, batch=batch, t=t, mix_width=mix_width, width=width)
        qn = conv(part=0, l2norm=True, scale=HEAD_DIM ** -0.5)
        kn = conv(part=1, l2norm=True)
        vc = conv(part=2, l2norm=False)
        g, beta = _gdn_gates(tail, a_block, a_block + 1, a_log, dt_bias)
        z_arr, z_block0 = proj, 3 * mix_width // (group * HEAD_DIM)
        tp = -(-t // GDN_CHUNK) * GDN_CHUNK
        if tp != t:
            pad = lambda a_: jnp.pad(a_.reshape(batch, t, -1), ((0, 0), (0, tp - t), (0, 0))).reshape(batch * tp, -1)
            qn, kn, vc, g, beta = (pad(a_) for a_ in (qn, kn, vc, g, beta))
            z_arr, z_block0 = pad(proj[:, 3 * mix_width:main]), 0
        gc = _cumsum_rows(g, batch, tp, carry=False)
        u, w, qk, qg, kg = _gdn_intra(qn, kn, vc, gc, beta, batch, tp, heads, group)
        mix, s_new = _gdn_state(u, w, qk, qg, kg, gc, z_arr, z_block0, g_gdn_out[j], s0, batch, tp, heads, group)
        if tp != t:
            mix = mix.reshape(batch, tp, mix_width)[:, :t].reshape(batch * t, mix_width)
        keep = min(t, width - 1)
        newest = proj.reshape(batch, t, -1)[:, t - keep:, :3 * mix_width]
        out[name] = dict(mix=mix, tail=tail, conv=jnp.concatenate([buf, newest], axis=1)[:, keep:], s=s_new)
    return out


def kernel(x_prompt, x_sample, cache_fox_k, cache_fox_v, cache_fox_logf, cache_mem_k, cache_mem_v, state_conv, state_gdn_conv, state_gdn_s, page_table, mem_prompt, g_mix_pre, g_mix_post, g_mlp_pre, g_mlp_post, g_mem, w_mem_kv, w_out, w_mlp_up, w_mlp_down, w_in_fox, b_fox_f, w_in_conv, w_conv, w_in_gdn, w_gdn_conv, gdn_a_log, gdn_dt_bias, g_gdn_out):
    bp, tp_, d = x_prompt.shape
    bs, ts, _ = x_sample.shape
    depth = g_mix_pre.shape[0]
    mem_tokens, mem_heads, mem_hd = cache_mem_k.shape[2:]
    mem_width = mem_heads * mem_hd
    mix_width = d - mem_width
    heads = mix_width // HEAD_DIM
    n_fox, n_pool, page = cache_fox_k.shape[:3]

    y = {"prompt": x_prompt.reshape(bp * tp_, d), "sample": x_sample.reshape(bs * ts, d)}
    dims = {"prompt": (bp, tp_), "sample": (bs, ts)}
    x = {n: _norm_cast(y[n], g_mix_pre[0]) for n in y}
    mem2d = mem_prompt.reshape(bp * mem_tokens, d)
    caches = (cache_fox_k.reshape(n_fox, n_pool, page * heads, HEAD_DIM),
              cache_fox_v.reshape(n_fox, n_pool, page * heads, HEAD_DIM),
              jnp.transpose(cache_fox_logf, (0, 1, 3, 2)))
    mem_k_s = cache_mem_k.reshape(depth, bs, mem_tokens, mem_width)
    mem_v_s = cache_mem_v.reshape(depth, bs, mem_tokens, mem_width)

    res = {k_: [] for k_ in ("fox_k_p", "fox_v_p", "fox_lf_p", "fox_k_s", "fox_v_s", "fox_lf_s", "conv_p", "conv_s",
                             "gconv_p", "gconv_s", "gs_p", "gs_s", "mem_k", "mem_v")}
    for i in range(depth):
        kind, j = i % 3, i // 3
        kv = _matmul(_norm_cast(mem2d, g_mem[i]), w_mem_kv, layer=i)
        res["mem_k"].append(kv[:, :mem_width].reshape(bp, mem_tokens, mem_heads, mem_hd))
        res["mem_v"].append(kv[:, mem_width:].reshape(bp, mem_tokens, mem_heads, mem_hd))
        if kind == 0:
            mixed = _fox_mixer(x, dims, w_in_fox, b_fox_f, j, caches, page_table, heads, mem_width)
            for n, s in (("prompt", "p"), ("sample", "s")):
                res["fox_k_" + s].append(mixed[n]["k"])
                res["fox_v_" + s].append(mixed[n]["v"])
                res["fox_lf_" + s].append(mixed[n]["logf"])
        elif kind == 1:
            mixed = _conv_mixer(x, dims, w_in_conv, w_conv, j, state_conv, heads, mem_width)
            res["conv_p"].append(mixed["prompt"]["buf"])
            res["conv_s"].append(mixed["sample"]["buf"])
        else:
            mixed = _gdn_mixer(x, dims, w_in_gdn, w_gdn_conv, gdn_a_log, gdn_dt_bias, g_gdn_out, j, state_gdn_conv,
                               state_gdn_s, heads, mem_width)
            for n, s in (("prompt", "p"), ("sample", "s")):
                res["gconv_" + s].append(mixed[n]["conv"])
                res["gs_" + s].append(mixed[n]["s"])
        g_next = g_mix_pre[i + 1] if i + 1 < depth else g_mlp_pre[i]
        cat = {}
        for n, (b, t) in dims.items():
            if n == "prompt":
                mem = _mem_attn(mixed[n]["tail"], kv, kv, lambda bb, part: (bb, part), b, t, mem_heads, mem_width,
                                mem_tokens)
            else:
                mem = _mem_attn(mixed[n]["tail"], mem_k_s, mem_v_s, lambda bb, part: (i, bb, 0, 0), b, t, mem_heads,
                                mem_width, mem_tokens)
            cat[n] = (mixed[n]["mix"], mem)
        delta = _matmul_groups(cat, w_out, layer=i)
        x2 = {}
        for n in dims:
            y[n], x2[n] = _resid_norm(delta[n], y[n], g_mix_post[i], g_mlp_pre[i])
        hid = _matmul_groups(x2, w_mlp_up, layer=i, out_dtype=BF16, act="relu2")
        delta = _matmul_groups(hid, w_mlp_down, layer=i, tn=1024, tk=2048)
        for n in dims:
            y[n], x[n] = _resid_norm(delta[n], y[n], g_mlp_post[i], g_next)

    st = lambda k_: jnp.stack(res[k_])
    return (y["prompt"].reshape(bp, tp_, d), y["sample"].reshape(bs, ts, d),
            st("fox_k_p"), st("fox_v_p"), st("fox_lf_p"), st("conv_p"), st("gconv_p"), st("gs_p"),
            st("mem_k"), st("mem_v"),
            st("fox_k_s"), st("fox_v_s"), st("fox_lf_s"), st("conv_s"), st("gconv_s"), st("gs_s"))
```

```python
import functools

import jax
import jax.numpy as jnp
from jax import lax
from jax.experimental import pallas as pl
from jax.experimental.pallas import tpu as pltpu

F32 = jnp.float32
BF16 = jnp.bfloat16
RMS_EPS = 1e-6
L2_EPS = 1e-6
HEAD_DIM = 128
LANES = 128
SUBLANES = 8
GDN_CHUNK = 128
V7X_VMEM_BYTES = 64 * 1024 * 1024
VMEM_CAP = V7X_VMEM_BYTES - 6 * 1024 * 1024
HIGHEST = lax.Precision.HIGHEST
NT_DIMS = (((1,), (1,)), ((), ()))
TN_DIMS = (((0,), (0,)), ((), ()))


def _params(semantics, vmem_bytes):
    limit = int(min(VMEM_CAP, max(16 * 1024 * 1024, vmem_bytes * 5 // 4 + (2 << 20))))
    return pltpu.CompilerParams(dimension_semantics=semantics, vmem_limit_bytes=limit)


def _tile(dim, pref, align=LANES):
    if dim <= pref:
        return dim
    for cand in range(pref - pref % align, 0, -align):
        if dim % cand == 0:
            return cand
    raise ValueError((dim, pref, align))


def _dot(a, b):
    return jnp.dot(a.astype(BF16), b.astype(BF16), preferred_element_type=F32)


def _dot_nt(a, b):
    return lax.dot_general(a.astype(BF16), b.astype(BF16), NT_DIMS, preferred_element_type=F32)


def _dot_tn(a, b):
    return lax.dot_general(a.astype(BF16), b.astype(BF16), TN_DIMS, preferred_element_type=F32)


def _each(fn, *columns):
    return [fn(*args) for args in zip(*columns)]


def _log_sigmoid(x):
    return jnp.minimum(x, 0.0) - jnp.log1p(jnp.exp(-jnp.abs(x)))


def _softplus(x):
    return jnp.maximum(x, 0.0) + jnp.log1p(jnp.exp(-jnp.abs(x)))


def _silu(x):
    return x * jax.nn.sigmoid(x)


def _pick_lane(blk, idx):
    lane = lax.broadcasted_iota(jnp.int32, blk.shape, 1)
    return jnp.sum(jnp.where(lane == idx, blk, 0.0), axis=-1, keepdims=True)


def _norm_cast_kernel(y_ref, g_ref, o_ref):
    y = y_ref[...]
    inv = lax.rsqrt(jnp.mean(y * y, axis=-1, keepdims=True) + RMS_EPS)
    o_ref[...] = (y * inv * g_ref[...]).astype(o_ref.dtype)


def _norm_cast(y, g):
    m, d = y.shape
    tr = _tile(m, 256)
    return pl.pallas_call(
        _norm_cast_kernel,
        grid=(m // tr,),
        in_specs=[pl.BlockSpec((tr, d), lambda i: (i, 0)), pl.BlockSpec((1, d), lambda i: (0, 0))],
        out_specs=pl.BlockSpec((tr, d), lambda i: (i, 0)),
        out_shape=jax.ShapeDtypeStruct((m, d), BF16),
        compiler_params=_params(("parallel",), 2 * tr * d * 6),
        name="norm_cast",
    )(y, g.reshape(1, d))


def _resid_norm_kernel(d_ref, y_ref, gp_ref, gn_ref, yo_ref, xo_ref):
    d = d_ref[...]
    inv = lax.rsqrt(jnp.mean(d * d, axis=-1, keepdims=True) + RMS_EPS)
    yn = y_ref[...] + d * inv * gp_ref[...]
    yo_ref[...] = yn
    inv2 = lax.rsqrt(jnp.mean(yn * yn, axis=-1, keepdims=True) + RMS_EPS)
    xo_ref[...] = (yn * inv2 * gn_ref[...]).astype(xo_ref.dtype)


def _resid_norm(delta, y, g_post, g_next):
    m, d = y.shape
    tr = _tile(m, 256)
    row = pl.BlockSpec((tr, d), lambda i: (i, 0))
    vec = pl.BlockSpec((1, d), lambda i: (0, 0))
    return pl.pallas_call(
        _resid_norm_kernel,
        grid=(m // tr,),
        in_specs=[row, row, vec, vec],
        out_specs=[row, row],
        out_shape=[jax.ShapeDtypeStruct((m, d), F32), jax.ShapeDtypeStruct((m, d), BF16)],
        compiler_params=_params(("parallel",), 2 * tr * d * 14),
        name="resid_norm",
    )(delta, y, g_post.reshape(1, d), g_next.reshape(1, d))


def _mm_kernel(*refs, nk, act, rider, transposed, parts):
    x_refs, refs = refs[:parts], refs[parts:]
    if rider:
        xr_refs, (w_ref, o_ref, or_ref) = refs[:parts], refs[parts:]
    else:
        w_ref, o_ref = refs
    k = pl.program_id(2)
    dims = NT_DIMS if transposed else (((1,), (0,)), ((), ()))

    def product(lhs_refs, out_ref):
        lhs = lhs_refs[0][...] if parts == 1 else jnp.concatenate([r[...] for r in lhs_refs], axis=1)
        part = lax.dot_general(lhs, w_ref[...].astype(BF16), dims, preferred_element_type=F32)
        if nk == 1:
            if act == "relu2":
                part = jnp.square(jnp.maximum(part, 0.0))
            out_ref[...] = part.astype(out_ref.dtype)
        else:
            @pl.when(k == 0)
            def _():
                out_ref[...] = jnp.zeros_like(out_ref)

            out_ref[...] += part

    product(x_refs, o_ref)
    if rider:
        i = pl.program_id(0)

        @pl.when(i == 0)
        def _():
            product(xr_refs, or_ref)

        @pl.when(i > 0)
        def _():
            or_ref[...] = jnp.zeros_like(or_ref)


def _matmul(x, w, *, rider=None, layer=None, col0=0, n=None, out_dtype=F32, act=None, transposed=False,
            tall=False, tm=1024, tn=512, tk=4096):
    xs = x if isinstance(x, (tuple, list)) else (x,)
    riders = () if rider is None else (rider if isinstance(rider, (tuple, list)) else (rider,))
    m, kdim = xs[0].shape[0], sum(part.shape[1] for part in xs)
    n_axis, k_axis = (-2, -1) if transposed else (-1, -2)
    n = w.shape[n_axis] - col0 if n is None else n
    tm, tn, tk = _tile(m, tm), _tile(n, tn), _tile(kdim, tk)
    assert col0 % tn == 0 and w.shape[k_axis] == kdim
    nk = kdim // tk
    assert nk == 1 or (act is None and out_dtype == F32 and len(xs) == 1)
    lhs_single_buffer = nk == 1 and tall and m % (2 * tm) == 0
    if lhs_single_buffer:
        tm *= 2
    cb = col0 // tn
    widths = [tk] if len(xs) == 1 else [part.shape[1] for part in xs]
    lead = () if layer is None else (None,)
    at = (lambda *idx: idx) if layer is None else (lambda *idx: (layer,) + idx)
    if transposed:
        w_spec = pl.BlockSpec(lead + (tn, tk), lambda i, j, k: at(cb + j, k))
    else:
        w_spec = pl.BlockSpec(lead + (tk, tn), lambda i, j, k: at(k, cb + j))
    osz = jnp.dtype(out_dtype).itemsize
    lhs_mode = dict(pipeline_mode=pl.Buffered(1)) if lhs_single_buffer else {}
    in_specs = [pl.BlockSpec((tm, width), lambda i, j, k: (i, k), **lhs_mode) for width in widths]
    out_specs = [pl.BlockSpec((tm, tn), lambda i, j, k: (i, j))]
    out_shape = [jax.ShapeDtypeStruct((m, n), out_dtype)]
    operands = list(xs)
    mr = 0
    if riders:
        mr = riders[0].shape[0]
        spare = min(1, m // tm - 1)
        in_specs += [pl.BlockSpec((mr, width), lambda i, j, k: (0, k)) for width in widths]
        out_specs.append(pl.BlockSpec((None, mr, tn), lambda i, j, k: (jnp.minimum(i, spare), 0, j)))
        out_shape.append(jax.ShapeDtypeStruct((1 + spare, mr, n), out_dtype))
        operands += list(riders)
    vmem = 2 * ((tm + mr) * tk * 2 + tk * tn * 4 + (tm + mr) * tn * osz) + tk * tn * 2 + 2 * tm * tn * 4
    outs = pl.pallas_call(
        functools.partial(_mm_kernel, nk=nk, act=act, rider=bool(riders), transposed=transposed, parts=len(xs)),
        grid=(m // tm, n // tn, nk),
        in_specs=in_specs + [w_spec],
        out_specs=out_specs,
        out_shape=out_shape,
        compiler_params=_params(("parallel", "parallel", "arbitrary"), vmem),
        name="matmul",
    )(*operands, w)
    return outs[0] if not riders else (outs[0], outs[1][0])


def _matmul_groups(x, w, **kw):
    out, out_r = _matmul(x["prompt"], w, rider=x["sample"], tall=True, **kw)
    return {"prompt": out, "sample": out_r}


def _mem_attn_kernel(q_ref, k_ref, v_ref, o_ref, *, heads):
    dh = q_ref.shape[-1] // heads
    scale = dh ** -0.5
    cols = [slice(hh * dh, (hh + 1) * dh) for hh in range(heads)]
    of = lambda ref: [ref[:, sl] for sl in cols]
    s = _each(lambda q, k: _dot_nt(q, k) * scale, of(q_ref), of(k_ref))
    p = _each(lambda s_: jnp.exp(s_ - jnp.max(s_, axis=-1, keepdims=True)), s)
    o = _each(lambda p_, v: _dot(p_, v) / jnp.sum(p_, axis=-1, keepdims=True), p, of(v_ref))
    for sl, o_ in zip(cols, o):
        o_ref[:, sl] = o_.astype(o_ref.dtype)


def _mem_attn(tail, k_arr, v_arr, kv_index, batch, t, heads, mem_width, mt):
    tt = _tile(t, 512)
    nt = t // tt
    lead = (None,) * (k_arr.ndim - 2)
    k_spec = pl.BlockSpec(lead + (mt, mem_width), lambda b, i: kv_index(b, 0))
    v_spec = pl.BlockSpec(lead + (mt, mem_width), lambda b, i: kv_index(b, 1))
    return pl.pallas_call(
        functools.partial(_mem_attn_kernel, heads=heads),
        grid=(batch, nt),
        in_specs=[pl.BlockSpec((tt, mem_width), lambda b, i: (b * nt + i, 0)), k_spec, v_spec],
        out_specs=pl.BlockSpec((tt, mem_width), lambda b, i: (b * nt + i, 0)),
        out_shape=jax.ShapeDtypeStruct((batch * t, mem_width), BF16),
        compiler_params=_params(("parallel", "parallel"), 2 * (tt * mem_width * 6 + 2 * mt * mem_width * 4) + 4 * tt * mt * 4),
        name="mem_attn",
    )(tail, k_arr, v_arr)


def _logf_kernel(f_ref, b_ref, o_ref):
    o_ref[...] = _log_sigmoid(f_ref[...] + b_ref[...])


def _fox_logf(tail, gate_block, bias_pad):
    m = tail.shape[0]
    tr = _tile(m, 1024)
    return pl.pallas_call(
        _logf_kernel,
        grid=(m // tr,),
        in_specs=[pl.BlockSpec((tr, LANES), lambda i: (i, gate_block)), pl.BlockSpec((1, LANES), lambda i: (0, 0))],
        out_specs=pl.BlockSpec((tr, LANES), lambda i: (i, 0)),
        out_shape=jax.ShapeDtypeStruct((m, LANES), F32),
        compiler_params=_params(("parallel",), 4 * tr * LANES * 4),
        name="fox_logf",
    )(tail, bias_pad)


def _cumsum_kernel(x_ref, o_ref, carry_ref, *, carry):
    tr = x_ref.shape[0]
    r = lax.broadcasted_iota(jnp.int32, (tr, tr), 0)
    c = lax.broadcasted_iota(jnp.int32, (tr, tr), 1)
    lower = (r >= c).astype(F32)
    cs = jnp.dot(lower, x_ref[...], precision=HIGHEST, preferred_element_type=F32)
    if carry:
        @pl.when(pl.program_id(1) == 0)
        def _():
            carry_ref[...] = jnp.zeros_like(carry_ref)

        cs = cs + carry_ref[...]
        carry_ref[...] = cs[tr - 1:tr, :]
    o_ref[...] = cs


def _cumsum_rows(x, batch, t, *, carry):
    tr = LANES
    nt = t // tr
    return pl.pallas_call(
        functools.partial(_cumsum_kernel, carry=carry),
        grid=(batch, nt),
        in_specs=[pl.BlockSpec((tr, LANES), lambda b, i: (b * nt + i, 0))],
        out_specs=pl.BlockSpec((tr, LANES), lambda b, i: (b * nt + i, 0)),
        out_shape=jax.ShapeDtypeStruct(x.shape, F32),
        scratch_shapes=[pltpu.VMEM((1, LANES), F32)],
        compiler_params=_params(("parallel", "arbitrary"), 8 * tr * LANES * 4),
        name="cumsum_rows",
    )(x)


def _fox_prompt_kernel(q_ref, k_ref, v_ref, crow_ref, o_ref, kb_ref, vb_ref, *, n_q, group):
    qi = pl.program_id(2)
    tq = q_ref.shape[0]
    scale = HEAD_DIM ** -0.5
    cols = [slice(i * HEAD_DIM, (i + 1) * HEAD_DIM) for i in range(group)]

    @pl.when(qi == 0)
    def _():
        kb_ref[...] = k_ref[...].astype(BF16)
        vb_ref[...] = v_ref[...].astype(BF16)

    def attend(extent):
        row = lax.broadcasted_iota(jnp.int32, (tq, extent), 0) + (extent - tq)
        col = lax.broadcasted_iota(jnp.int32, (tq, extent), 1)
        s = [_dot_nt(q_ref[:, sl], kb_ref[0:extent, sl]) for sl in cols]
        s = [jnp.where(col <= row, s[i] * scale - crow_ref[i, :, 0:extent], -jnp.inf) for i in range(group)]
        p = _each(lambda s_: jnp.exp(s_ - jnp.max(s_, axis=-1, keepdims=True)), s)
        pv = [_dot(p[i], vb_ref[0:extent, cols[i]]) for i in range(group)]
        for i in range(group):
            o_ref[:, cols[i]] = (pv[i] / jnp.sum(p[i], axis=-1, keepdims=True)).astype(o_ref.dtype)

    for tile in range(n_q):
        pl.when(qi == tile)(functools.partial(attend, (tile + 1) * tq))


def _fox_prompt(q, k, v, c_row, batch, t, heads):
    tq = _tile(t, 256)
    nq = t // tq
    group = next(g for g in (4, 2, 1) if heads % g == 0)
    gw = group * HEAD_DIM
    blk = pl.BlockSpec((tq, gw), lambda b, g, i: (b * nq + i, g))
    full = pl.BlockSpec((t, gw), lambda b, g, i: (b, g))
    return pl.pallas_call(
        functools.partial(_fox_prompt_kernel, n_q=nq, group=group),
        grid=(batch, heads // group, nq),
        in_specs=[blk, full, full, pl.BlockSpec((None, group, 1, t), lambda b, g, i: (b, g, 0, 0))],
        out_specs=blk,
        out_shape=jax.ShapeDtypeStruct((batch * t, heads * HEAD_DIM), BF16),
        scratch_shapes=[pltpu.VMEM((t, gw), BF16), pltpu.VMEM((t, gw), BF16)],
        compiler_params=_params(("parallel", "parallel", "arbitrary"), 5 * t * gw * 4 + 6 * group * tq * t * 4),
        name="fox_prompt",
    )(q, k, v, c_row)


def _fox_decode_kernel(pt_ref, q_ref, kn_ref, vn_ref, lfn_ref, *rest, n_steps, per_step, heads, t_new, page):
    del pt_ref
    pools = [rest[3 * i:3 * i + 3] for i in range(per_step)]
    o_ref, m_ref, l_ref, acc_ref, carry_ref = rest[3 * per_step:]
    p = pl.program_id(1)
    scale = HEAD_DIM ** -0.5
    r = lax.broadcasted_iota(jnp.int32, (page, page), 0)
    c = lax.broadcasted_iota(jnp.int32, (page, page), 1)
    upper = (r <= c).astype(F32)

    @pl.when(p == 0)
    def _():
        m_ref[...] = jnp.full_like(m_ref, -jnp.inf)
        l_ref[...] = jnp.zeros_like(l_ref)
        acc_ref[...] = jnp.zeros_like(acc_ref)
        carry_ref[...] = jnp.zeros_like(carry_ref)

    head_cols = lambda h: slice(h * HEAD_DIM, (h + 1) * HEAD_DIM)

    def attend(k_heads, v_heads, lf_t, mask):
        c_t = jnp.dot(lf_t, upper, precision=HIGHEST, preferred_element_type=F32) + carry_ref[...]
        carry_ref[...] = c_t[:, page - 1:page]
        s = jnp.stack([_dot_nt(q_ref[:, head_cols(h)], k_heads[h]) for h in range(heads)])
        s = s * scale - c_t[:, None, :]
        if mask is not None:
            s = jnp.where(mask[None], s, -jnp.inf)
        m_old = m_ref[...]
        m_new = jnp.maximum(m_old, jnp.max(s, axis=-1, keepdims=True))
        alpha = jnp.exp(m_old - m_new)
        pr = jnp.exp(s - m_new)
        l_ref[...] = alpha * l_ref[...] + jnp.sum(pr, axis=-1, keepdims=True)
        pv = jnp.stack([_dot(pr[h], v_heads[h]) for h in range(heads)])
        acc_ref[...] = alpha * acc_ref[...] + pv
        m_ref[...] = m_new

    @pl.when(p < n_steps)
    def _():
        for kp_ref, vp_ref, lfp_ref in pools:
            attend([kp_ref[pl.ds(h, page, stride=heads), :] for h in range(heads)],
                   [vp_ref[pl.ds(h, page, stride=heads), :] for h in range(heads)], lfp_ref[...], None)

    @pl.when(p == n_steps)
    def _():
        pad = jnp.zeros((page - t_new, HEAD_DIM), F32)
        qrow = lax.broadcasted_iota(jnp.int32, (t_new, page), 0)
        kcol = lax.broadcasted_iota(jnp.int32, (t_new, page), 1)
        attend([jnp.concatenate([kn_ref[:, head_cols(h)], pad], axis=0) for h in range(heads)],
               [jnp.concatenate([vn_ref[:, head_cols(h)], pad], axis=0) for h in range(heads)],
               lfn_ref[...], kcol <= qrow)
        out = acc_ref[...] / l_ref[...]
        for h in range(heads):
            o_ref[:, head_cols(h)] = out[h].astype(o_ref.dtype)


def _fox_decode(q, k, v, lfn_t, k_pool, v_pool, lf_pool_t, page_table, layer, batch, t_new, heads):
    n_pages = page_table.shape[1]
    page = k_pool.shape[2] // heads
    width = heads * HEAD_DIM
    assert page == LANES and t_new <= page
    per_step = next(c for c in (4, 2, 1) if n_pages % c == 0)
    n_steps = n_pages // per_step
    new = pl.BlockSpec((t_new, width), lambda b, p, pt: (b, 0))

    def page_specs(slot):
        at = lambda b, p, pt: (layer, pt[b, jnp.minimum(p, n_steps - 1) * per_step + slot], 0, 0)
        return [pl.BlockSpec((None, None, page * heads, HEAD_DIM), at),
                pl.BlockSpec((None, None, page * heads, HEAD_DIM), at),
                pl.BlockSpec((None, None, heads, page), at)]

    grid_spec = pltpu.PrefetchScalarGridSpec(
        num_scalar_prefetch=1,
        grid=(batch, n_steps + 1),
        in_specs=[new, new, new, pl.BlockSpec((None, heads, page), lambda b, p, pt: (b, 0, 0))]
        + [spec for slot in range(per_step) for spec in page_specs(slot)],
        out_specs=new,
        scratch_shapes=[pltpu.VMEM((heads, t_new, 1), F32), pltpu.VMEM((heads, t_new, 1), F32),
                        pltpu.VMEM((heads, t_new, HEAD_DIM), F32), pltpu.VMEM((heads, 1), F32)],
    )
    return pl.pallas_call(
        functools.partial(_fox_decode_kernel, n_steps=n_steps, per_step=per_step, heads=heads, t_new=t_new,
                          page=page),
        grid_spec=grid_spec,
        out_shape=jax.ShapeDtypeStruct((batch * t_new, width), BF16),
        compiler_params=_params(("parallel", "arbitrary"),
                                4 * per_step * page * width * 4 + 8 * t_new * width * 4 + (4 << 20)),
        name="fox_decode",
    )(page_table, q, k, v, lfn_t, *([k_pool, v_pool, lf_pool_t] * per_step))


def _shortconv_kernel(gb_ref, gc_ref, h_ref, buf_ref, w_ref, o_ref, tail_ref, us_ref, *, width):
    tt = gb_ref.shape[0]

    @pl.when(pl.program_id(2) == 0)
    def _():
        us_ref[0:SUBLANES, :] = buf_ref[...]

    us_ref[SUBLANES:SUBLANES + tt, :] = gc_ref[...] * h_ref[...]
    conv = None
    for k in range(width):
        off = SUBLANES - (width - 1) + k
        term = w_ref[k:k + 1, :] * us_ref[off:off + tt, :]
        conv = term if conv is None else conv + term
    o_ref[...] = (gb_ref[...] * conv).astype(o_ref.dtype)
    last = us_ref[tt:tt + SUBLANES, :]
    tail_ref[...] = last
    us_ref[0:SUBLANES, :] = last


def _shortconv(proj, buf8, w8, batch, t, mix_width, width):
    tt = _tile(t, 1024)
    tc = _tile(mix_width, 512)
    nt, nc = t // tt, mix_width // tc
    col = lambda off: pl.BlockSpec((tt, tc), lambda b, c, i: (b * nt + i, off * nc + c))
    return pl.pallas_call(
        functools.partial(_shortconv_kernel, width=width),
        grid=(batch, nc, nt),
        in_specs=[col(0), col(1), col(2),
                  pl.BlockSpec((None, SUBLANES, tc), lambda b, c, i: (b, 0, c)),
                  pl.BlockSpec((SUBLANES, tc), lambda b, c, i: (0, c))],
        out_specs=[col(0), pl.BlockSpec((None, SUBLANES, tc), lambda b, c, i: (b, 0, c))],
        out_shape=[jax.ShapeDtypeStruct((batch * t, mix_width), BF16),
                   jax.ShapeDtypeStruct((batch, SUBLANES, mix_width), F32)],
        scratch_shapes=[pltpu.VMEM((tt + SUBLANES, tc), F32)],
        compiler_params=_params(("parallel", "parallel", "arbitrary"), 12 * tt * tc * 4),
        name="shortconv",
    )(proj, proj, proj, buf8, w8)


def _gdn_conv_kernel(x_ref, buf_ref, w_ref, o_ref, xs_ref, *, width, l2norm, scale):
    tt, tc = x_ref.shape

    @pl.when(pl.program_id(2) == 0)
    def _():
        xs_ref[0:SUBLANES, :] = buf_ref[...]

    xs_ref[SUBLANES:SUBLANES + tt, :] = x_ref[...]
    conv = None
    for k in range(width):
        off = SUBLANES - (width - 1) + k
        term = w_ref[k:k + 1, :] * xs_ref[off:off + tt, :]
        conv = term if conv is None else conv + term
    y = _silu(conv)
    if l2norm:
        for g in range(tc // HEAD_DIM):
            sl = slice(g * HEAD_DIM, (g + 1) * HEAD_DIM)
            seg = y[:, sl]
            seg = seg * lax.rsqrt(jnp.sum(seg * seg, axis=-1, keepdims=True) + L2_EPS)
            o_ref[:, sl] = seg * scale if scale != 1.0 else seg
    else:
        o_ref[...] = y
    xs_ref[0:SUBLANES, :] = xs_ref[tt:tt + SUBLANES, :]


def _gdn_conv(proj, buf8, w8, part, batch, t, mix_width, width, *, l2norm, scale=1.0):
    tt = _tile(t, 1024)
    tc = _tile(mix_width, 512)
    nt, nc = t // tt, mix_width // tc
    return pl.pallas_call(
        functools.partial(_gdn_conv_kernel, width=width, l2norm=l2norm, scale=scale),
        grid=(batch, nc, nt),
        in_specs=[pl.BlockSpec((tt, tc), lambda b, c, i: (b * nt + i, part * nc + c)),
                  pl.BlockSpec((None, SUBLANES, tc), lambda b, c, i: (b, 0, part * nc + c)),
                  pl.BlockSpec((SUBLANES, tc), lambda b, c, i: (0, part * nc + c))],
        out_specs=pl.BlockSpec((tt, tc), lambda b, c, i: (b * nt + i, c)),
        out_shape=jax.ShapeDtypeStruct((batch * t, mix_width), F32),
        scratch_shapes=[pltpu.VMEM((tt + SUBLANES, tc), F32)],
        compiler_params=_params(("parallel", "parallel", "arbitrary"), 10 * tt * tc * 4),
        name="gdn_conv",
    )(proj, buf8, w8)


def _gdn_gate_kernel(a_ref, bt_ref, alog_ref, dt_ref, g_ref, beta_ref):
    g_ref[...] = -jnp.exp(alog_ref[...]) * _softplus(a_ref[...] + dt_ref[...])
    beta_ref[...] = jax.nn.sigmoid(bt_ref[...])


def _gdn_gates(tail, a_block, bt_block, a_log_pad, dt_pad):
    m = tail.shape[0]
    tr = _tile(m, 1024)
    blk = lambda off: pl.BlockSpec((tr, LANES), lambda i: (i, off))
    vec = pl.BlockSpec((1, LANES), lambda i: (0, 0))
    return pl.pallas_call(
        _gdn_gate_kernel,
        grid=(m // tr,),
        in_specs=[blk(a_block), blk(bt_block), vec, vec],
        out_specs=[blk(0), blk(0)],
        out_shape=[jax.ShapeDtypeStruct((m, LANES), F32)] * 2,
        compiler_params=_params(("parallel",), 8 * tr * LANES * 4),
        name="gdn_gates",
    )(tail, tail, a_log_pad, dt_pad)


def _unit_lower_inverse_offdiag(mats, ri, ci):
    n = mats[0].shape[0]
    blk = lambda idx, size: jnp.right_shift(idx, size.bit_length() - 1)
    base = SUBLANES
    in_base = blk(ri, base) == blk(ci, base)
    a0 = _each(lambda a: jnp.where(in_base, a, 0.0), mats)
    low = _each(lambda a: -a, a0)
    pw = _each(_dot, a0, a0)
    low = _each(lambda l, p, lp: l + p + lp, low, pw, _each(_dot, low, pw))
    pw = _each(_dot, pw, pw)
    low = _each(lambda l, p, lp: l + p + lp, low, pw, _each(_dot, low, pw))
    s = base
    while s < n:
        sub = (blk(ri, 2 * s) == blk(ci, 2 * s)) & (blk(ri, s) != blk(ci, s))
        off = _each(lambda a: jnp.where(sub, a, 0.0), mats)
        x = _each(lambda o, lo: o + lo, off, _each(_dot, low, off))
        low = _each(lambda l, x_, xl: l - (x_ + xl), low, x, _each(_dot, x, low))
        s *= 2
    return low


def _gdn_intra_kernel(q_ref, k_ref, v_ref, gc_ref, beta_ref, u_ref, w_ref, qk_ref, qg_ref, kg_ref, *, group):
    hg = pl.program_id(1)
    n = q_ref.shape[0]
    ri = lax.broadcasted_iota(jnp.int32, (n, n), 0)
    ci = lax.broadcasted_iota(jnp.int32, (n, n), 1)
    incl = ri >= ci
    cols = [slice(i * HEAD_DIM, (i + 1) * HEAD_DIM) for i in range(group)]
    q = [q_ref[:, sl] for sl in cols]
    k = [k_ref[:, sl] for sl in cols]
    v = [v_ref[:, sl] for sl in cols]
    gcol = [_pick_lane(gc_ref[...], hg * group + i) for i in range(group)]
    bcol = [_pick_lane(beta_ref[...], hg * group + i) for i in range(group)]

    def decay_of(g):
        gmat = jnp.broadcast_to(g, (n, n))
        return jnp.where(incl, jnp.exp(jnp.where(incl, gmat - gmat.T, 0.0)), 0.0)

    decay = _each(decay_of, gcol)
    kb = _each(lambda k_, b: k_ * b, k, bcol)
    a = _each(lambda kk, d: jnp.where(ri > ci, kk * d, 0.0), _each(_dot_nt, kb, k), decay)
    low = _unit_lower_inverse_offdiag(a, ri, ci)
    eg = _each(jnp.exp, gcol)
    rhs = _each(lambda v_, b, kb_, e: jnp.concatenate([v_ * b, kb_ * e], axis=1), v, bcol, kb, eg)
    uw = _each(lambda r, lr: r + lr, rhs, _each(_dot, low, rhs))
    qk = _each(lambda x, d: x * d, _each(_dot_nt, q, k), decay)
    for i, sl in enumerate(cols):
        u_ref[:, sl] = uw[i][:, :HEAD_DIM]
        w_ref[:, sl] = uw[i][:, HEAD_DIM:].astype(w_ref.dtype)
        qk_ref[:, sl] = qk[i].astype(qk_ref.dtype)
        qg_ref[:, sl] = (q[i] * eg[i]).astype(qg_ref.dtype)
        kg_ref[:, sl] = (k[i] * jnp.exp(gcol[i][n - 1:n, :] - gcol[i])).astype(kg_ref.dtype)


def _gdn_intra(qn, kn, vc, gc, beta, batch, t, heads, group):
    n = GDN_CHUNK
    nc = t // n
    wide = pl.BlockSpec((n, group * HEAD_DIM), lambda b, g, c: (b * nc + c, g))
    gate = pl.BlockSpec((n, LANES), lambda b, g, c: (b * nc + c, 0))
    shape = lambda dt: jax.ShapeDtypeStruct((batch * t, heads * HEAD_DIM), dt)
    return pl.pallas_call(
        functools.partial(_gdn_intra_kernel, group=group),
        grid=(batch, heads // group, nc),
        in_specs=[wide, wide, wide, gate, gate],
        out_specs=[wide] * 5,
        out_shape=[shape(F32)] + [shape(BF16)] * 4,
        compiler_params=_params(("parallel", "parallel", "parallel"), 2 * 8 * n * group * HEAD_DIM * 4 + (8 << 20)),
        name="gdn_intra",
    )(qn, kn, vc, gc, beta)


def _gdn_state_kernel(u_ref, w_ref, qk_ref, qg_ref, kg_ref, gc_ref, z_ref, gout_ref, s0_ref, o_ref, sout_ref,
                      s_ref, *, group, n_chunks):
    hg = pl.program_id(1)
    c = pl.program_id(2)
    n = u_ref.shape[0]

    @pl.when(c == 0)
    def _():
        s_ref[...] = s0_ref[...]

    cols = [slice(i * HEAD_DIM, (i + 1) * HEAD_DIM) for i in range(group)]
    of = lambda ref: [ref[:, sl] for sl in cols]
    s = [s_ref[i] for i in range(group)]
    v_new = _each(lambda u, ws: u - ws, of(u_ref), _each(_dot, of(w_ref), s))
    o = _each(lambda x, y: x + y, _each(_dot, of(qg_ref), s), _each(_dot, of(qk_ref), v_new))
    upd = _each(_dot_tn, of(kg_ref), v_new)
    for i, sl in enumerate(cols):
        g_last = _pick_lane(gc_ref[n - 1:n, :], hg * group + i)
        s_ref[i] = s[i] * jnp.exp(g_last) + upd[i]
        on = o[i] * lax.rsqrt(jnp.mean(o[i] * o[i], axis=-1, keepdims=True) + RMS_EPS) * gout_ref[...]
        o_ref[:, sl] = (on * _silu(z_ref[:, sl])).astype(o_ref.dtype)

    @pl.when(c == n_chunks - 1)
    def _():
        sout_ref[...] = s_ref[...]


def _gdn_state(u, w, qk, qg, kg, gc, z_arr, z_block0, g_out, s0, batch, t, heads, group):
    n = GDN_CHUNK
    nc = t // n
    gw = group * HEAD_DIM
    wide = pl.BlockSpec((n, gw), lambda b, g, c: (b * nc + c, g))
    state = pl.BlockSpec((None, group, HEAD_DIM, HEAD_DIM), lambda b, g, c: (b, g, 0, 0))
    return pl.pallas_call(
        functools.partial(_gdn_state_kernel, group=group, n_chunks=nc),
        grid=(batch, heads // group, nc),
        in_specs=[wide] * 5 + [pl.BlockSpec((n, LANES), lambda b, g, c: (b * nc + c, 0)),
                               pl.BlockSpec((n, gw), lambda b, g, c: (b * nc + c, z_block0 + g)),
                               pl.BlockSpec((1, HEAD_DIM), lambda b, g, c: (0, 0)), state],
        out_specs=[wide, state],
        out_shape=[jax.ShapeDtypeStruct((batch * t, heads * HEAD_DIM), BF16),
                   jax.ShapeDtypeStruct((batch, heads, HEAD_DIM, HEAD_DIM), F32)],
        scratch_shapes=[pltpu.VMEM((group, HEAD_DIM, HEAD_DIM), F32)],
        compiler_params=_params(("parallel", "parallel", "arbitrary"), 2 * 8 * n * gw * 4 + 6 * group * HEAD_DIM * HEAD_DIM * 4 + (4 << 20)),
        name="gdn_state",
    )(u, w, qk, qg, kg, gc, z_arr, g_out.reshape(1, HEAD_DIM), s0)


def _pad_lanes(v, width=LANES):
    return jnp.pad(v, [(0, 0)] * (v.ndim - 1) + [(0, width - v.shape[-1])])


def _pad_rows8(a, rows_axis):
    pad = [(0, 0)] * a.ndim
    pad[rows_axis] = (SUBLANES - a.shape[rows_axis], 0)
    return jnp.pad(a, pad)


def _in_weight(w_stack):
    if w_stack.shape[2] % LANES:
        return jnp.swapaxes(w_stack, 1, 2), True
    return w_stack, False


def _tail_weight(w, transposed, j, main, parts, mem_width):
    n_axis = 0 if transposed else 1
    total = w.shape[1 + n_axis]
    take = (lambda a, b: w[j, a:b, :]) if transposed else (lambda a, b: w[j, :, a:b])
    pad = [(0, 0), (0, 0)]
    cols = [take(total - mem_width, total)]
    off = main
    for width in parts:
        pad[n_axis] = (0, LANES - width)
        cols.append(jnp.pad(take(off, off + width), pad))
        off += width
    return jnp.concatenate(cols, axis=n_axis)


def _tail_proj(x, w_tail, transposed):
    return _matmul_groups(x, w_tail, transposed=transposed)


def _fox_mixer(x, dims, w_in_fox, b_fox_f, j, caches, page_table, heads, mem_width):
    mix_width = heads * HEAD_DIM
    main = 3 * mix_width
    w_in, w_t = _in_weight(w_in_fox)
    w_tail = _tail_weight(w_in, w_t, j, main, [heads], mem_width)
    gate_block = mem_width // LANES
    bias = _pad_lanes(b_fox_f[j].reshape(1, heads))
    out = {}
    qkv = [_matmul_groups(x, w_in, layer=j, col0=part * mix_width, n=mix_width, transposed=w_t) for part in range(3)]
    tails = _tail_proj(x, w_tail, w_t)
    for name, (batch, t) in dims.items():
        q, k, v = (part[name] for part in qkv)
        tail = tails[name]
        logf = _fox_logf(tail, gate_block, bias)
        if name == "prompt":
            c_tok = _cumsum_rows(logf, batch, t, carry=True)
            c_row = jnp.transpose(c_tok.reshape(batch, t, LANES)[:, :, :heads], (0, 2, 1))[:, :, None, :]
            mix = _fox_prompt(q, k, v, c_row, batch, t, heads)
        else:
            k_pool, v_pool, lf_pool_t = caches
            lfn_t = jnp.transpose(logf.reshape(batch, t, LANES)[:, :, :heads], (0, 2, 1))
            lfn_t = _pad_lanes(lfn_t, lf_pool_t.shape[-1])
            mix = _fox_decode(q, k, v, lfn_t, k_pool, v_pool, lf_pool_t, page_table, j, batch, t, heads)
        shp = (batch, t, heads, HEAD_DIM)
        out[name] = dict(mix=mix, tail=tail, k=k.reshape(shp), v=v.reshape(shp),
                         logf=logf[:, :heads].reshape(batch, t, heads))
    return out


def _conv_mixer(x, dims, w_in_conv, w_conv, j, state_conv, heads, mem_width):
    mix_width = heads * HEAD_DIM
    main = 3 * mix_width
    width = w_conv.shape[1]
    w8 = jnp.pad(w_conv[j], ((0, SUBLANES - width), (0, 0)))
    out = {}
    w_in, w_t = _in_weight(w_in_conv)
    projs = _matmul_groups(x, w_in, layer=j, col0=0, n=main, transposed=w_t)
    tails = _tail_proj(x, _tail_weight(w_in, w_t, j, main, [], mem_width), w_t)
    for name, (batch, t) in dims.items():
        proj, tail = projs[name], tails[name]
        if name == "prompt":
            buf8 = jnp.zeros((batch, SUBLANES, mix_width), F32)
        else:
            buf8 = _pad_rows8(state_conv[j], 1)
        mix, last8 = _shortconv(proj, buf8, w8, batch, t, mix_width, width)
        out[name] = dict(mix=mix, tail=tail, buf=last8[:, SUBLANES - (width - 1):, :])
    return out


def _gdn_mixer(x, dims, w_in_gdn, w_gdn_conv, gdn_a_log, gdn_dt_bias, g_gdn_out, j, state_gdn_conv, state_gdn_s,
               heads, mem_width):
    mix_width = heads * HEAD_DIM
    main = 4 * mix_width
    width = w_gdn_conv.shape[1]
    w_in, w_t = _in_weight(w_in_gdn)
    w_tail = _tail_weight(w_in, w_t, j, main, [heads, heads], mem_width)
    a_block = mem_width // LANES
    w8 = jnp.pad(w_gdn_conv[j], ((0, SUBLANES - width), (0, 0)))
    a_log = _pad_lanes(gdn_a_log[j].reshape(1, heads))
    dt_bias = _pad_lanes(gdn_dt_bias[j].reshape(1, heads))
    group = next(g for g in (12, 8, 4, 2, 1) if heads % g == 0)
    out = {}
    projs = _matmul_groups(x, w_in, layer=j, col0=0, n=main, transposed=w_t)
    tails = _tail_proj(x, w_tail, w_t)
    for name, (batch, t) in dims.items():
        proj, tail = projs[name], tails[name]
        if name == "prompt":
            buf = jnp.zeros((batch, width - 1, 3 * mix_width), F32)
            s0 = jnp.zeros((batch, heads, HEAD_DIM, HEAD_DIM), F32)
        else:
            buf, s0 = state_gdn_conv[j], state_gdn_s[j]
        buf8 = _pad_rows8(buf, 1)
        conv = functools.partial(_gdn_conv, proj, buf8, w8, batch=batch, t=t, mix_width=mix_width, width=width)
        qn = conv(part=0, l2norm=True, scale=HEAD_DIM ** -0.5)
        kn = conv(part=1, l2norm=True)
        vc = conv(part=2, l2norm=False)
        g, beta = _gdn_gates(tail, a_block, a_block + 1, a_log, dt_bias)
        z_arr, z_block0 = proj, 3 * mix_width // (group * HEAD_DIM)
        tp = -(-t // GDN_CHUNK) * GDN_CHUNK
        if tp != t:
            pad = lambda a_: jnp.pad(a_.reshape(batch, t, -1), ((0, 0), (0, tp - t), (0, 0))).reshape(batch * tp, -1)
            qn, kn, vc, g, beta = (pad(a_) for a_ in (qn, kn, vc, g, beta))
            z_arr, z_block0 = pad(proj[:, 3 * mix_width:main]), 0
        gc = _cumsum_rows(g, batch, tp, carry=False)
        u, w, qk, qg, kg = _gdn_intra(qn, kn, vc, gc, beta, batch, tp, heads, group)
        mix, s_new = _gdn_state(u, w, qk, qg, kg, gc, z_arr, z_block0, g_gdn_out[j], s0, batch, tp, heads, group)
        if tp != t:
            mix = mix.reshape(batch, tp, mix_width)[:, :t].reshape(batch * t, mix_width)
        keep = min(t, width - 1)
        newest = proj.reshape(batch, t, -1)[:, t - keep:, :3 * mix_width]
        out[name] = dict(mix=mix, tail=tail, conv=jnp.concatenate([buf, newest], axis=1)[:, keep:], s=s_new)
    return out


def kernel(x_prompt, x_sample, cache_fox_k, cache_fox_v, cache_fox_logf, cache_mem_k, cache_mem_v, state_conv, state_gdn_conv, state_gdn_s, page_table, mem_prompt, g_mix_pre, g_mix_post, g_mlp_pre, g_mlp_post, g_mem, w_mem_kv, w_out, w_mlp_up, w_mlp_down, w_in_fox, b_fox_f, w_in_conv, w_conv, w_in_gdn, w_gdn_conv, gdn_a_log, gdn_dt_bias, g_gdn_out):
    bp, tp_, d = x_prompt.shape
    bs, ts, _ = x_sample.shape
    depth = g_mix_pre.shape[0]
    mem_tokens, mem_heads, mem_hd = cache_mem_k.shape[2:]
    mem_width = mem_heads * mem_hd
    mix_width = d - mem_width
    heads = mix_width // HEAD_DIM
    n_fox, n_pool, page = cache_fox_k.shape[:3]

    y = {"prompt": x_prompt.reshape(bp * tp_, d), "sample": x_sample.reshape(bs * ts, d)}
    dims = {"prompt": (bp, tp_), "sample": (bs, ts)}
    x = {n: _norm_cast(y[n], g_mix_pre[0]) for n in y}
    mem2d = mem_prompt.reshape(bp * mem_tokens, d)
    caches = (cache_fox_k.reshape(n_fox, n_pool, page * heads, HEAD_DIM),
              cache_fox_v.reshape(n_fox, n_pool, page * heads, HEAD_DIM),
              jnp.transpose(cache_fox_logf, (0, 1, 3, 2)))
    mem_k_s = cache_mem_k.reshape(depth, bs, mem_tokens, mem_width)
    mem_v_s = cache_mem_v.reshape(depth, bs, mem_tokens, mem_width)

    res = {k_: [] for k_ in ("fox_k_p", "fox_v_p", "fox_lf_p", "fox_k_s", "fox_v_s", "fox_lf_s", "conv_p", "conv_s",
                             "gconv_p", "gconv_s", "gs_p", "gs_s", "mem_k", "mem_v")}
    for i in range(depth):
        kind, j = i % 3, i // 3
        kv = _matmul(_norm_cast(mem2d, g_mem[i]), w_mem_kv, layer=i)
        res["mem_k"].append(kv[:, :mem_width].reshape(bp, mem_tokens, mem_heads, mem_hd))
        res["mem_v"].append(kv[:, mem_width:].reshape(bp, mem_tokens, mem_heads, mem_hd))
        if kind == 0:
            mixed = _fox_mixer(x, dims, w_in_fox, b_fox_f, j, caches, page_table, heads, mem_width)
            for n, s in (("prompt", "p"), ("sample", "s")):
                res["fox_k_" + s].append(mixed[n]["k"])
                res["fox_v_" + s].append(mixed[n]["v"])
                res["fox_lf_" + s].append(mixed[n]["logf"])
        elif kind == 1:
            mixed = _conv_mixer(x, dims, w_in_conv, w_conv, j, state_conv, heads, mem_width)
            res["conv_p"].append(mixed["prompt"]["buf"])
            res["conv_s"].append(mixed["sample"]["buf"])
        else:
            mixed = _gdn_mixer(x, dims, w_in_gdn, w_gdn_conv, gdn_a_log, gdn_dt_bias, g_gdn_out, j, state_gdn_conv,
                               state_gdn_s, heads, mem_width)
            for n, s in (("prompt", "p"), ("sample", "s")):
                res["gconv_" + s].append(mixed[n]["conv"])
                res["gs_" + s].append(mixed[n]["s"])
        g_next = g_mix_pre[i + 1] if i + 1 < depth else g_mlp_pre[i]
        cat = {}
        for n, (b, t) in dims.items():
            if n == "prompt":
                mem = _mem_attn(mixed[n]["tail"], kv, kv, lambda bb, part: (bb, part), b, t, mem_heads, mem_width,
                                mem_tokens)
            else:
                mem = _mem_attn(mixed[n]["tail"], mem_k_s, mem_v_s, lambda bb, part: (i, bb, 0, 0), b, t, mem_heads,
                                mem_width, mem_tokens)
            cat[n] = (mixed[n]["mix"], mem)
        delta = _matmul_groups(cat, w_out, layer=i)
        x2 = {}
        for n in dims:
            y[n], x2[n] = _resid_norm(delta[n], y[n], g_mix_post[i], g_mlp_pre[i])
        hid = _matmul_groups(x2, w_mlp_up, layer=i, out_dtype=BF16, act="relu2")
        delta = _matmul_groups(hid, w_mlp_down, layer=i, tn=1024, tk=2048)
        for n in dims:
            y[n], x[n] = _resid_norm(delta[n], y[n], g_mlp_post[i], g_next)

    st = lambda k_: jnp.stack(res[k_])
    return (y["prompt"].reshape(bp, tp_, d), y["sample"].reshape(bs, ts, d),
            st("fox_k_p"), st("fox_v_p"), st("fox_lf_p"), st("conv_p"), st("gconv_p"), st("gs_p"),
            st("mem_k"), st("mem_v"),
            st("fox_k_s"), st("fox_v_s"), st("fox_lf_s"), st("conv_s"), st("gconv_s"), st("gs_s"))
```
